```python
import jax
import jax.numpy as jnp
from jax import lax
import numpy as np

D_MODEL = 4096
BATCH = 2
SEQ = 4096
DEPTH = 1

CHUNK = 64
N_META = 16

POOL_WINDOWS = (2, 4, 8, 16)
N_POOL_GROUPS = len(POOL_WINDOWS)
POOL_WIDTH = D_MODEL // 2
POOL_GROUP = POOL_WIDTH // N_POOL_GROUPS

HEAD_DIM = 128
ATTN_WIDTH = D_MODEL // 2
N_HEADS = ATTN_WIDTH // HEAD_DIM
Q_BLOCK = 128

N_BRANCHES = 2
IN_WIDTH = POOL_WIDTH + 3 * ATTN_WIDTH + N_HEADS + N_BRANCHES * D_MODEL

N_EXPERTS = 32
TOP_K = 4
D_EXPERT = (3 * D_MODEL) // 8
SWIGLU_LIMIT = 7.0
SWIGLU_ALPHA = 1.702
EXPERT_BLOCK = 128

NORM_EPS = 1e-6

kernel_name = 'hybrid_pool_fox_moe_encoder'


def rms_norm(x, gain):
    xf = x.astype(jnp.float32)
    y = xf * lax.rsqrt(jnp.mean(xf * xf, axis=-1, keepdims=True) + NORM_EPS)
    return (y * gain.astype(jnp.float32)).astype(x.dtype)


def pool_mixer(u, w_pool_group, pool_scale):
    bsz, s, g, c = u.shape
    uf = u.astype(jnp.float32)
    cs = jnp.pad(jnp.cumsum(uf, axis=1), ((0, 0), (1, 0), (0, 0), (0, 0)))
    t = jnp.arange(1, s + 1, dtype=jnp.float32)
    means = []
    for gi, w in enumerate(POOL_WINDOWS):
        upper = cs[:, 1:, gi]
        lower = jnp.pad(cs[:, :s - w + 1, gi], ((0, 0), (w - 1, 0), (0, 0)))
        count = jnp.minimum(t, float(w))
        means.append((upper - lower) / count[None, :, None])
    pooled = (jnp.stack(means, axis=2) - uf).astype(u.dtype)
    y = jnp.einsum('bsgc,gcd->bsgd', pooled, w_pool_group)
    return y.reshape(bsz, s, g * c) * pool_scale


def _heads_first_padded(a, pad):
    a = jnp.swapaxes(a, 1, 2)
    widths = ((0, 0), (0, 0), (0, pad)) + ((0, 0),) * (a.ndim - 3)
    return jnp.pad(a, widths)


def forgetting_attention(q, k, v, log_f):
    bsz, s, h, dh = q.shape
    n_blocks = -(-s // Q_BLOCK)
    pad = n_blocks * Q_BLOCK - s
    c = jnp.cumsum(log_f, axis=1)
    qh = _heads_first_padded(q, pad)
    kh = _heads_first_padded(k, pad)
    vh = _heads_first_padded(v, pad)
    ch = _heads_first_padded(c, pad)
    scale = HEAD_DIM ** -0.5
    outs = []
    for blk in range(n_blocks):
        lo, hi = blk * Q_BLOCK, (blk + 1) * Q_BLOCK
        logits = jnp.einsum('bhqd,bhkd->bhqk', qh[:, :, lo:hi], kh[:, :, :hi]).astype(jnp.float32) * scale
        logits = logits + ch[:, :, lo:hi, None] - ch[:, :, None, :hi]
        causal = jnp.arange(lo, hi)[:, None] >= jnp.arange(hi)[None, :]
        p = jax.nn.softmax(jnp.where(causal, logits, -jnp.inf), axis=-1).astype(vh.dtype)
        outs.append(jnp.einsum('bhqk,bhkd->bhqd', p, vh[:, :, :hi]))
    o = jnp.concatenate(outs, axis=2)[:, :, :s]
    return jnp.swapaxes(o, 1, 2)


def hybrid_mixer(hn, w_in, b_forget, b_branch_gate, q_norm, k_norm, w_pool_group, pool_scale,
                 w_branch_pool, w_branch_attn, w_out):
    bsz, s, _ = hn.shape
    proj = hn @ w_in
    splits = [POOL_WIDTH, POOL_WIDTH + ATTN_WIDTH, POOL_WIDTH + 2 * ATTN_WIDTH,
              POOL_WIDTH + 3 * ATTN_WIDTH, POOL_WIDTH + 3 * ATTN_WIDTH + N_HEADS]
    u, q, k, v, z_f, z_g = jnp.split(proj, splits, axis=-1)
    y_pool = pool_mixer(u.reshape(bsz, s, N_POOL_GROUPS, POOL_GROUP), w_pool_group, pool_scale)
    q = rms_norm(q.reshape(bsz, s, N_HEADS, HEAD_DIM), q_norm)
    k = rms_norm(k.reshape(bsz, s, N_HEADS, HEAD_DIM), k_norm)
    v = v.reshape(bsz, s, N_HEADS, HEAD_DIM)
    log_f = jax.nn.log_sigmoid((z_f + b_forget).astype(jnp.float32))
    y_attn = forgetting_attention(q, k, v, log_f).reshape(bsz, s, ATTN_WIDTH)
    g_pool, g_attn = jnp.split(jax.nn.sigmoid(z_g + b_branch_gate), N_BRANCHES, axis=-1)
    merged = g_pool * (y_pool @ w_branch_pool) + g_attn * (y_attn @ w_branch_attn)
    return merged @ w_out


def moe_ffn(h, w_router, b_router, w_gate, b_gate, w_up, b_up, w_down, b_down):
    bsz, s, d = h.shape
    xf = h.reshape(-1, d)
    n_tok = xf.shape[0]
    logits = (xf @ w_router).astype(jnp.float32) + b_router.astype(jnp.float32)
    top_val, top_idx = lax.top_k(logits, TOP_K)
    weights = jax.nn.softmax(top_val, axis=-1).astype(h.dtype)
    n_rows = n_tok * TOP_K
    e_flat = top_idx.reshape(-1).astype(jnp.int32)
    g_flat = weights.reshape(-1)
    t_flat = jnp.arange(n_rows, dtype=jnp.int32) // TOP_K
    order = jnp.argsort(e_flat)
    e_sorted = e_flat[order]
    counts = jnp.bincount(e_flat, length=N_EXPERTS).astype(jnp.int32)
    starts = jnp.cumsum(counts) - counts
    padded = (counts + EXPERT_BLOCK - 1) // EXPERT_BLOCK * EXPERT_BLOCK
    pad_end = jnp.cumsum(padded)
    pad_start = pad_end - padded
    dest = pad_start[e_sorted] + jnp.arange(n_rows, dtype=jnp.int32) - starts[e_sorted]
    n_blocks = -(-(n_rows + N_EXPERTS * (EXPERT_BLOCK - 1)) // EXPERT_BLOCK)
    n_pad_rows = n_blocks * EXPERT_BLOCK
    row_tok = jnp.full((n_pad_rows,), n_tok, jnp.int32).at[dest].set(t_flat[order])
    row_gate = jnp.zeros((n_pad_rows,), h.dtype).at[dest].set(g_flat[order])
    block_start = jnp.arange(n_blocks, dtype=jnp.int32) * EXPERT_BLOCK
    block_e = jnp.minimum(jnp.searchsorted(pad_end, block_start, side='right'), N_EXPERTS - 1).astype(jnp.int32)
    xpad = jnp.concatenate([xf, jnp.zeros((1, d), xf.dtype)], axis=0)

    def expert_block(args):
        rows, e = args
        xb = xpad[rows]
        gte = jnp.minimum(xb @ w_gate[e] + b_gate[e], SWIGLU_LIMIT)
        up = jnp.clip(xb @ w_up[e] + b_up[e], -SWIGLU_LIMIT, SWIGLU_LIMIT)
        act = gte * jax.nn.sigmoid(SWIGLU_ALPHA * gte) * (up + 1.0)
        return act @ w_down[e] + b_down[e]

    y = lax.map(expert_block, (row_tok.reshape(n_blocks, EXPERT_BLOCK), block_e))
    y = y.reshape(n_pad_rows, d) * row_gate[:, None]
    out = jax.ops.segment_sum(y, row_tok, num_segments=n_tok + 1)[:n_tok]
    return out.reshape(bsz, s, d)


def setup_inputs(seed: int = 0) -> dict:
    key = jax.random.key(seed)
    ks = jax.random.split(key, 24)

    def nrm(k, shape, scale):
        return jax.random.normal(k, shape, jnp.float32) * scale

    return {
        'x': nrm(ks[0], (BATCH, SEQ, D_MODEL), 1.0),
        'meta_tokens': nrm(ks[1], (N_META, D_MODEL), 1.0),
        'norm_mix': 1.0 + nrm(ks[2], (DEPTH, D_MODEL), 0.02),
        'w_in': nrm(ks[3], (DEPTH, D_MODEL, IN_WIDTH), D_MODEL ** -0.5),
        'b_forget': jax.random.uniform(ks[4], (DEPTH, N_HEADS), jnp.float32, 1.0, 6.0),
        'b_branch_gate': nrm(ks[5], (DEPTH, N_BRANCHES * D_MODEL), 0.02),
        'q_norm': 1.0 + nrm(ks[6], (DEPTH, HEAD_DIM), 0.02),
        'k_norm': 1.0 + nrm(ks[7], (DEPTH, HEAD_DIM), 0.02),
        'w_pool_group': nrm(ks[8], (DEPTH, N_POOL_GROUPS, POOL_GROUP, POOL_GROUP), POOL_GROUP ** -0.5),
        'pool_scale': 1.0 + nrm(ks[9], (DEPTH, POOL_WIDTH), 0.1),
        'w_branch_pool': nrm(ks[10], (DEPTH, POOL_WIDTH, D_MODEL), POOL_WIDTH ** -0.5),
        'w_branch_attn': nrm(ks[11], (DEPTH, ATTN_WIDTH, D_MODEL), ATTN_WIDTH ** -0.5),
        'w_out': nrm(ks[12], (DEPTH, D_MODEL, D_MODEL), D_MODEL ** -0.5),
        'norm_ffn': 1.0 + nrm(ks[13], (DEPTH, D_MODEL), 0.02),
        'w_router': nrm(ks[14], (DEPTH, D_MODEL, N_EXPERTS), D_MODEL ** -0.5),
        'b_router': nrm(ks[15], (DEPTH, N_EXPERTS), 0.01),
        'w_gate': nrm(ks[16], (DEPTH, N_EXPERTS, D_MODEL, D_EXPERT), D_MODEL ** -0.5),
        'b_gate': nrm(ks[17], (DEPTH, N_EXPERTS, D_EXPERT), 0.01),
        'w_up': nrm(ks[18], (DEPTH, N_EXPERTS, D_MODEL, D_EXPERT), D_MODEL ** -0.5),
        'b_up': nrm(ks[19], (DEPTH, N_EXPERTS, D_EXPERT), 0.01),
        'w_down': nrm(ks[20], (DEPTH, N_EXPERTS, D_EXPERT, D_MODEL), D_EXPERT ** -0.5),
        'b_down': nrm(ks[21], (DEPTH, N_EXPERTS, D_MODEL), 0.01),
    }


def reference(x, meta_tokens, norm_mix, w_in, b_forget, b_branch_gate, q_norm, k_norm, w_pool_group,
              pool_scale, w_branch_pool, w_branch_attn, w_out, norm_ffn, w_router, b_router, w_gate,
              b_gate, w_up, b_up, w_down, b_down):
    bsz = x.shape[0]
    meta = jnp.broadcast_to(meta_tokens[None].astype(x.dtype), (bsz, N_META, D_MODEL))
    h = jnp.concatenate([meta, x], axis=1)
    for l in range(DEPTH):
        h = h + hybrid_mixer(rms_norm(h, norm_mix[l]), w_in[l], b_forget[l], b_branch_gate[l],
                             q_norm[l], k_norm[l], w_pool_group[l], pool_scale[l],
                             w_branch_pool[l], w_branch_attn[l], w_out[l])
        h = h + moe_ffn(rms_norm(h, norm_ffn[l]), w_router[l], b_router[l], w_gate[l], b_gate[l],
                        w_up[l], b_up[l], w_down[l], b_down[l])
    return h[:, N_META:]
```

```python
import functools

import jax
import jax.numpy as jnp
from jax import lax
from jax.experimental import pallas as pl
from jax.experimental.pallas import tpu as pltpu

NORM_EPS = 1e-6
HEAD_DIM = 128
POOL_WINDOWS = (2, 4, 8, 16)
TOP_K = 4
SWIGLU_LIMIT = 7.0
SWIGLU_ALPHA = 1.702
EXPERT_BLOCK = 128
LANES = 128
NEG_BIG = -1e30
VMEM_LIMIT = 56 * 1024 * 1024

F32 = jnp.float32
BF16 = jnp.bfloat16


def _params(*sem, vmem=VMEM_LIMIT):
    return pltpu.CompilerParams(dimension_semantics=sem, vmem_limit_bytes=vmem)


def _dot(a, b):
    return jnp.dot(a, b, preferred_element_type=F32)


def _dot_nt(a, b):
    return lax.dot_general(a, b, (((1,), (1,)), ((), ())), preferred_element_type=F32)


def _rms(x, gain):
    ms = jnp.mean(x * x, axis=-1, keepdims=True)
    return x * lax.rsqrt(ms + NORM_EPS) * gain


def _rmsnorm_kernel(x_ref, g_ref, o_ref):
    o_ref[...] = _rms(x_ref[...], g_ref[...]).astype(o_ref.dtype)


def _rmsnorm(x, gain, tm):
    rows, d = x.shape
    return pl.pallas_call(
        _rmsnorm_kernel,
        grid=(rows // tm,),
        in_specs=[pl.BlockSpec((tm, d), lambda i: (i, 0)), pl.BlockSpec((1, d), lambda i: (0, 0))],
        out_specs=pl.BlockSpec((tm, d), lambda i: (i, 0)),
        out_shape=jax.ShapeDtypeStruct((rows, d), BF16),
        compiler_params=_params("arbitrary"),
        name="rmsnorm",
    )(x, gain)


def _proj_kernel(*refs, has_meta, gate_bias):
    it = iter(refs)
    x_ref = next(it)
    xm_ref = next(it) if has_meta else None
    w_ref = next(it)
    b_ref = next(it) if gate_bias else None
    o_ref = next(it)
    om_ref = next(it) if has_meta else None
    wbf_ref = next(it)

    @pl.when(pl.program_id(1) == 0)
    def _():
        wbf_ref[...] = w_ref[...].astype(BF16)
        if has_meta:
            om_ref[...] = _dot(xm_ref[...], wbf_ref[...]).astype(om_ref.dtype)

    acc = _dot(x_ref[...], wbf_ref[...])
    if gate_bias:
        acc = jax.nn.sigmoid(acc + b_ref[...])
    o_ref[...] = acc.astype(o_ref.dtype)


def _proj(x, xm, w, bias, *, col0, ncols, tn, tm, out_dtype, name):
    rows, k = x.shape
    assert col0 % tn == 0 and ncols % tn == 0 and rows % tm == 0
    jb = col0 // tn
    in_specs = [pl.BlockSpec((tm, k), lambda j, i: (i, 0))]
    args = [x]
    if xm is not None:
        in_specs.append(pl.BlockSpec(xm.shape, lambda j, i: (0, 0)))
        args.append(xm)
    in_specs.append(pl.BlockSpec((k, tn), lambda j, i: (0, jb + j)))
    args.append(w)
    if bias is not None:
        in_specs.append(pl.BlockSpec((1, tn), lambda j, i: (0, j)))
        args.append(bias)
    out_specs = [pl.BlockSpec((tm, tn), lambda j, i: (i, j))]
    out_shape = [jax.ShapeDtypeStruct((rows, ncols), out_dtype)]
    if xm is not None:
        out_specs.append(pl.BlockSpec((xm.shape[0], tn), lambda j, i: (0, j)))
        out_shape.append(jax.ShapeDtypeStruct((xm.shape[0], ncols), out_dtype))
    res = pl.pallas_call(
        functools.partial(_proj_kernel, has_meta=xm is not None, gate_bias=bias is not None),
        grid=(ncols // tn, rows // tm),
        in_specs=in_specs,
        out_specs=out_specs,
        out_shape=out_shape,
        scratch_shapes=[pltpu.VMEM((k, tn), BF16)],
        compiler_params=_params("arbitrary", "arbitrary"),
        name=name,
    )(*args)
    return res if xm is not None else res[0]


def _cumsum_kernel(zf_ref, zfm_ref, bf_ref, cr_ref, mb_ref, *, n_meta):
    seq = zf_ref.shape[0]
    bf = bf_ref[...]
    r = lax.broadcasted_iota(jnp.int32, (LANES, LANES), 0)
    c = lax.broadcasted_iota(jnp.int32, (LANES, LANES), 1)
    tri = (c <= r).astype(F32)
    sfx = (c > r).astype(F32)

    def prefix(mat, x):
        return jnp.dot(mat, x, preferred_element_type=F32, precision=lax.Precision.HIGHEST)

    n_meta_pad = zfm_ref.shape[0]
    lfm = jax.nn.log_sigmoid(zfm_ref[...] + bf)
    rows = lax.broadcasted_iota(jnp.int32, (n_meta_pad, LANES), 0)
    lfm = jnp.where(rows < n_meta, lfm, 0.0)
    mbias = jnp.where(rows < n_meta, prefix(sfx, lfm), NEG_BIG)
    mbias_t = mbias.T
    for h in range(mb_ref.shape[0]):
        mb_ref[h] = mbias_t[h:h + 1, :]

    def body(blk, carry):
        off = pl.multiple_of(blk * LANES, LANES)
        lf = jax.nn.log_sigmoid(zf_ref[pl.ds(off, LANES), :] + bf)
        cs = prefix(tri, lf) + carry
        cs_t = cs.T
        for h in range(cr_ref.shape[0]):
            cr_ref[h, :, pl.ds(off, LANES)] = cs_t[h:h + 1, :]
        return cs[LANES - 1:LANES, :]

    lax.fori_loop(0, seq // LANES, body, jnp.zeros((1, LANES), F32))


def _cumsum(zf, zfm_pad, bf_pad, batch, n_heads, n_meta):
    t = zf.shape[0]
    seq = t // batch
    return pl.pallas_call(
        functools.partial(_cumsum_kernel, n_meta=n_meta),
        grid=(batch,),
        in_specs=[pl.BlockSpec((seq, LANES), lambda b: (b, 0)),
                  pl.BlockSpec(zfm_pad.shape, lambda b: (0, 0)),
                  pl.BlockSpec((1, LANES), lambda b: (0, 0))],
        out_specs=[pl.BlockSpec((None, n_heads, 1, seq), lambda b: (b, 0, 0, 0)),
                   pl.BlockSpec((n_heads, 1, LANES), lambda b: (0, 0, 0))],
        out_shape=[jax.ShapeDtypeStruct((batch, n_heads, 1, seq), F32),
                   jax.ShapeDtypeStruct((n_heads, 1, LANES), F32)],
        compiler_params=_params("arbitrary"),
        name="forget_cumsum",
    )(zf, zfm_pad, bf_pad)


def _pool_kernel(u_ref, um_ref, wp_ref, ps_ref, o_ref, ext_ref, wbf_ref):
    ts = u_ref.shape[0]
    halo = um_ref.shape[0]
    gw = wp_ref.shape[1]
    i = pl.program_id(1)

    @pl.when((pl.program_id(0) == 0) & (i == 0))
    def _():
        wbf_ref[...] = wp_ref[...].astype(BF16)

    @pl.when(i == 0)
    def _():
        ext_ref[0:halo, :] = um_ref[...].astype(F32)

    @pl.when(i > 0)
    def _():
        ext_ref[0:halo, :] = ext_ref[ts:ts + halo, :]

    ext_ref[halo:halo + ts, :] = u_ref[...].astype(F32)
    for g, w in enumerate(POOL_WINDOWS):
        cols = slice(g * gw, (g + 1) * gw)
        uf = ext_ref[halo:halo + ts, cols]
        acc = uf
        for d in range(1, w):
            acc = acc + ext_ref[halo - d:halo - d + ts, cols]
        pooled = acc * (1.0 / w) - uf
        y = _dot(pooled.astype(BF16), wbf_ref[g]) * ps_ref[:, cols]
        o_ref[:, cols] = y.astype(o_ref.dtype)


def _pool(uqkv, uqkv_meta, w_pool, pool_scale, batch, ts):
    t = uqkv.shape[0]
    seq = t // batch
    ng, gw, _ = w_pool.shape
    pw = ng * gw
    halo = uqkv_meta.shape[0]
    assert halo >= max(POOL_WINDOWS) and seq % ts == 0
    nt = seq // ts
    return pl.pallas_call(
        _pool_kernel,
        grid=(batch, nt),
        in_specs=[pl.BlockSpec((ts, pw), lambda b, i: (b * nt + i, 0)),
                  pl.BlockSpec((halo, pw), lambda b, i: (0, 0)),
                  pl.BlockSpec(w_pool.shape, lambda b, i: (0, 0, 0)),
                  pl.BlockSpec((1, pw), lambda b, i: (0, 0))],
        out_specs=pl.BlockSpec((ts, pw), lambda b, i: (b * nt + i, 0)),
        out_shape=jax.ShapeDtypeStruct((t, pw), BF16),
        scratch_shapes=[pltpu.VMEM((halo + ts, pw), F32), pltpu.VMEM(w_pool.shape, BF16)],
        compiler_params=_params("arbitrary", "arbitrary"),
        name="pool_mixer",
    )(uqkv, uqkv_meta, w_pool, pool_scale)


def _attn_kernel(q_ref, k_ref, v_ref, km_ref, vm_ref, cr_ref, mb_ref, qg_ref, kg_ref, o_ref,
                 kn_ref, kmn_ref, vmp_ref):
    tq = q_ref.shape[0]
    n_meta = km_ref.shape[0]
    qi = pl.program_id(2)

    @pl.when(qi == 0)
    def _():
        kn_ref[...] = _rms(k_ref[...].astype(F32), kg_ref[...]).astype(BF16)
        kmn_ref[...] = jnp.zeros_like(kmn_ref)
        vmp_ref[...] = jnp.zeros_like(vmp_ref)
        kmn_ref[0:n_meta, :] = _rms(km_ref[...].astype(F32), kg_ref[...]).astype(BF16)
        vmp_ref[0:n_meta, :] = vm_ref[...]

    qn = (_rms(q_ref[...].astype(F32), qg_ref[...]) * (HEAD_DIM ** -0.5)).astype(BF16)
    q0 = pl.multiple_of(qi * tq, tq)
    cref = cr_ref[:, pl.ds(q0, LANES)][:, 0:1]

    def update(carry, s, vblk):
        m, l, acc = carry
        m_new = jnp.maximum(m, jnp.max(s, axis=-1, keepdims=True))
        alpha = jnp.exp(m - m_new)
        p = jnp.exp(s - m_new)
        l = alpha * l + jnp.sum(p, axis=-1, keepdims=True)
        acc = alpha * acc + _dot(p.astype(BF16), vblk)
        return m_new, l, acc

    s = _dot_nt(qn, kmn_ref[...]) + (cref + mb_ref[...])
    m = jnp.max(s, axis=-1, keepdims=True)
    p = jnp.exp(s - m)
    carry = (m, jnp.sum(p, axis=-1, keepdims=True), _dot(p.astype(BF16), vmp_ref[...]))

    def body(j, carry):
        off = pl.multiple_of(j * tq, tq)
        s = _dot_nt(qn, kn_ref[pl.ds(off, tq), :]) + (cref - cr_ref[:, pl.ds(off, tq)])
        return update(carry, s, v_ref[pl.ds(off, tq), :])

    carry = lax.fori_loop(0, qi, body, carry)

    s = _dot_nt(qn, kn_ref[pl.ds(q0, tq), :]) + (cref - cr_ref[:, pl.ds(q0, tq)])
    row = lax.broadcasted_iota(jnp.int32, (tq, tq), 0)
    col = lax.broadcasted_iota(jnp.int32, (tq, tq), 1)
    s = jnp.where(row >= col, s, NEG_BIG)
    m, l, acc = update(carry, s, v_ref[pl.ds(q0, tq), :])
    o_ref[...] = (acc / l).astype(o_ref.dtype)


def _attention(uqkv, uqkv_meta, cr, mbias, q_gain, k_gain, batch, n_heads, col_q, tq):
    t = uqkv.shape[0]
    seq = t // batch
    n_meta = uqkv_meta.shape[0]
    nq = seq // tq
    bq = col_q // HEAD_DIM
    bk = bq + n_heads
    bv = bk + n_heads
    return pl.pallas_call(
        _attn_kernel,
        grid=(batch, n_heads, nq),
        in_specs=[pl.BlockSpec((tq, HEAD_DIM), lambda b, h, i: (b * nq + i, bq + h)),
                  pl.BlockSpec((seq, HEAD_DIM), lambda b, h, i: (b, bk + h)),
                  pl.BlockSpec((seq, HEAD_DIM), lambda b, h, i: (b, bv + h)),
                  pl.BlockSpec((n_meta, HEAD_DIM), lambda b, h, i: (0, bk + h)),
                  pl.BlockSpec((n_meta, HEAD_DIM), lambda b, h, i: (0, bv + h)),
                  pl.BlockSpec((None, None, 1, seq), lambda b, h, i: (b, h, 0, 0)),
                  pl.BlockSpec((None, 1, LANES), lambda b, h, i: (h, 0, 0)),
                  pl.BlockSpec((1, HEAD_DIM), lambda b, h, i: (0, 0)),
                  pl.BlockSpec((1, HEAD_DIM), lambda b, h, i: (0, 0))],
        out_specs=pl.BlockSpec((tq, HEAD_DIM), lambda b, h, i: (b * nq + i, h)),
        out_shape=jax.ShapeDtypeStruct((t, n_heads * HEAD_DIM), BF16),
        scratch_shapes=[pltpu.VMEM((seq, HEAD_DIM), BF16),
                        pltpu.VMEM((LANES, HEAD_DIM), BF16),
                        pltpu.VMEM((LANES, HEAD_DIM), BF16)],
        compiler_params=_params("arbitrary", "arbitrary", "arbitrary"),
        name="forgetting_attention",
    )(uqkv, uqkv, uqkv, uqkv_meta, uqkv_meta, cr, mbias, q_gain, k_gain)


def _merge_kernel(yp_ref, ya_ref, wp_ref, wa_ref, gp_ref, ga_ref, o_ref, wpb_ref, wab_ref):
    @pl.when(pl.program_id(1) == 0)
    def _():
        wpb_ref[...] = wp_ref[...].astype(BF16)
        wab_ref[...] = wa_ref[...].astype(BF16)

    a = _dot(yp_ref[...], wpb_ref[...])
    b = _dot(ya_ref[...], wab_ref[...])
    o_ref[...] = (gp_ref[...].astype(F32) * a + ga_ref[...].astype(F32) * b).astype(o_ref.dtype)


def _merge(y_pool, y_attn, w_bp, w_ba, gates, tm, tn):
    t, kp = y_pool.shape
    ka = y_attn.shape[1]
    d = w_bp.shape[1]
    nj = d // tn
    return pl.pallas_call(
        _merge_kernel,
        grid=(nj, t // tm),
        in_specs=[pl.BlockSpec((tm, kp), lambda j, i: (i, 0)),
                  pl.BlockSpec((tm, ka), lambda j, i: (i, 0)),
                  pl.BlockSpec((kp, tn), lambda j, i: (0, j)),
                  pl.BlockSpec((ka, tn), lambda j, i: (0, j)),
                  pl.BlockSpec((tm, tn), lambda j, i: (i, j)),
                  pl.BlockSpec((tm, tn), lambda j, i: (i, nj + j))],
        out_specs=pl.BlockSpec((tm, tn), lambda j, i: (i, j)),
        out_shape=jax.ShapeDtypeStruct((t, d), BF16),
        scratch_shapes=[pltpu.VMEM((kp, tn), BF16), pltpu.VMEM((ka, tn), BF16)],
        compiler_params=_params("arbitrary", "arbitrary"),
        name="branch_merge",
    )(y_pool, y_attn, w_bp, w_ba, gates, gates)


def _outproj_kernel(m_ref, w_ref, x_ref, o_ref, wbf_ref):
    @pl.when(pl.program_id(1) == 0)
    def _():
        wbf_ref[...] = w_ref[...].astype(BF16)

    o_ref[...] = x_ref[...] + _dot(m_ref[...], wbf_ref[...])


def _outproj(merged, w_out, x, tm, tn):
    t, k = merged.shape
    d = w_out.shape[1]
    return pl.pallas_call(
        _outproj_kernel,
        grid=(d // tn, t // tm),
        in_specs=[pl.BlockSpec((tm, k), lambda j, i: (i, 0)),
                  pl.BlockSpec((k, tn), lambda j, i: (0, j)),
                  pl.BlockSpec((tm, tn), lambda j, i: (i, j))],
        out_specs=pl.BlockSpec((tm, tn), lambda j, i: (i, j)),
        out_shape=jax.ShapeDtypeStruct((t, d), F32),
        scratch_shapes=[pltpu.VMEM((k, tn), BF16)],
        compiler_params=_params("arbitrary", "arbitrary"),
        name="out_proj",
    )(merged, w_out, x)


def _router_kernel(h_ref, g_ref, wr_ref, br_ref, xp_ref, idx_ref, wt_ref):
    y = _rms(h_ref[...], g_ref[...])
    y_hi = y.astype(BF16)
    y_lo = (y - y_hi.astype(F32)).astype(BF16)
    w = wr_ref[...]
    w_hi = w.astype(BF16)
    w_lo = (w - w_hi.astype(F32)).astype(BF16)
    logits = _dot_nt(w_hi, y_hi) + (_dot_nt(w_hi, y_lo) + _dot_nt(w_lo, y_hi)) + br_ref[...]

    half = y.shape[1] // 2
    bits = pltpu.bitcast(y_hi.astype(F32), jnp.uint32)
    xp_ref[...] = (bits[:, :half] >> 16) | bits[:, half:]

    n_exp = logits.shape[0]
    eid = lax.broadcasted_iota(jnp.int32, logits.shape, 0)
    vals = logits
    top_v, top_i = [], []
    for _ in range(TOP_K):
        mx = jnp.max(vals, axis=0, keepdims=True)
        sel = jnp.min(jnp.where(vals == mx, eid, n_exp), axis=0, keepdims=True)
        top_v.append(mx)
        top_i.append(sel)
        vals = jnp.where(eid == sel, -jnp.inf, vals)
    ex = [jnp.exp(v - top_v[0]) for v in top_v]
    den = ex[0] + ex[1] + ex[2] + ex[3]
    for k in range(TOP_K):
        idx_ref[k:k + 1, :] = top_i[k]
        wt_ref[k:k + 1, :] = ex[k] / den


def _router(h1, gain, w_router_t, b_router, tm):
    t, d = h1.shape
    n_exp = w_router_t.shape[0]
    return pl.pallas_call(
        _router_kernel,
        grid=(t // tm,),
        in_specs=[pl.BlockSpec((tm, d), lambda i: (i, 0)),
                  pl.BlockSpec((1, d), lambda i: (0, 0)),
                  pl.BlockSpec((n_exp, d), lambda i: (0, 0)),
                  pl.BlockSpec((n_exp, 1), lambda i: (0, 0))],
        out_specs=[pl.BlockSpec((tm, d // 2), lambda i: (i, 0)),
                   pl.BlockSpec((TOP_K, tm), lambda i: (0, i)),
                   pl.BlockSpec((TOP_K, tm), lambda i: (0, i))],
        out_shape=[jax.ShapeDtypeStruct((t, d // 2), jnp.uint32),
                   jax.ShapeDtypeStruct((TOP_K, t), jnp.int32),
                   jax.ShapeDtypeStruct((TOP_K, t), F32)],
        compiler_params=_params("arbitrary"),
        name="router_topk",
    )(h1, gain, w_router_t, b_router)


def _plan_kernel(idx_ref, dest_ref, be_ref, rank_ref, *, n_exp):
    t = idx_ref.shape[1]
    nblk = t // LANES
    eid = lax.broadcasted_iota(jnp.int32, (n_exp, LANES), 0)
    r = lax.broadcasted_iota(jnp.int32, (LANES, LANES), 0)
    c = lax.broadcasted_iota(jnp.int32, (LANES, LANES), 1)
    upper = (r <= c).astype(BF16)

    def member(off):
        m = jnp.zeros((n_exp, LANES), F32)
        for k in range(TOP_K):
            m = m + (idx_ref[k:k + 1, pl.ds(off, LANES)] == eid).astype(F32)
        return m

    def count(blk, carry):
        off = pl.multiple_of(blk * LANES, LANES)
        m = member(off)
        cs = _dot(m.astype(BF16), upper) + carry
        rank_ref[:, pl.ds(off, LANES)] = cs - m
        return carry + jnp.sum(m, axis=1, keepdims=True)

    counts = lax.fori_loop(0, nblk, count, jnp.zeros((n_exp, 1), F32))
    padded = jnp.floor((counts + (EXPERT_BLOCK - 1)) * (1.0 / EXPERT_BLOCK)) * EXPERT_BLOCK
    er = lax.broadcasted_iota(jnp.int32, (n_exp, n_exp), 0)
    ec = lax.broadcasted_iota(jnp.int32, (n_exp, n_exp), 1)
    lower = (ec <= er).astype(BF16)
    pad_end = _dot(lower, jnp.broadcast_to(padded, (n_exp, LANES)).astype(BF16))[:, 0:1]
    pad_start = pad_end - padded

    def place(blk, _):
        off = pl.multiple_of(blk * LANES, LANES)
        pos = pad_start + rank_ref[:, pl.ds(off, LANES)]
        for k in range(TOP_K):
            sel = idx_ref[k:k + 1, pl.ds(off, LANES)] == eid
            d = jnp.sum(jnp.where(sel, pos, 0.0), axis=0, keepdims=True)
            dest_ref[blk, k:k + 1, :] = d.astype(jnp.int32)
        return 0

    lax.fori_loop(0, nblk, place, 0)

    start = (lax.broadcasted_iota(jnp.int32, (n_exp, be_ref.shape[1]), 1) * EXPERT_BLOCK).astype(F32)
    be = jnp.sum((pad_end <= start).astype(F32), axis=0, keepdims=True)
    be_ref[...] = jnp.minimum(be, n_exp - 1).astype(jnp.int32)


def _plan(idx, n_exp, n_blocks_pad):
    t = idx.shape[1]
    assert t <= 256 * EXPERT_BLOCK
    return pl.pallas_call(
        functools.partial(_plan_kernel, n_exp=n_exp),
        grid=(1,),
        in_specs=[pl.BlockSpec(idx.shape, lambda i: (0, 0))],
        out_specs=[pl.BlockSpec((t // LANES, TOP_K, LANES), lambda i: (0, 0, 0)),
                   pl.BlockSpec((1, n_blocks_pad), lambda i: (0, 0))],
        out_shape=[jax.ShapeDtypeStruct((t // LANES, TOP_K, LANES), jnp.int32),
                   jax.ShapeDtypeStruct((1, n_blocks_pad), jnp.int32)],
        scratch_shapes=[pltpu.VMEM((n_exp, t), F32)],
        compiler_params=_params("arbitrary"),
        name="dispatch_plan",
    )(idx)


def _row_copy(src_hbm, src_row, dst_hbm, dst_row, sem):
    return pltpu.make_async_copy(src_hbm.at[pl.ds(src_row, 1), :], dst_hbm.at[pl.ds(dst_row, 1), :], sem)


def _dispatch_kernel(dest_ref, xp_hbm, xs_in_hbm, xs_hbm, sem):
    del xs_in_hbm
    t0 = pl.program_id(0) * LANES

    def issue(r, _):
        for k in range(TOP_K):
            _row_copy(xp_hbm, t0 + r, xs_hbm, dest_ref[k, r], sem).start()
        return 0

    lax.fori_loop(0, LANES, issue, 0)

    def drain(r, _):
        for k in range(TOP_K):
            _row_copy(xp_hbm, 0, xs_hbm, 0, sem).wait()
        return 0

    lax.fori_loop(0, LANES, drain, 0)


def _dispatch(dest, xp, n_rows):
    t, w = xp.shape
    xs0 = jnp.zeros((n_rows, w), xp.dtype)
    return pl.pallas_call(
        _dispatch_kernel,
        grid=(t // LANES,),
        in_specs=[pl.BlockSpec((None, TOP_K, LANES), lambda i: (i, 0, 0), memory_space=pltpu.SMEM),
                  pl.BlockSpec(memory_space=pl.ANY),
                  pl.BlockSpec(memory_space=pl.ANY)],
        out_specs=pl.BlockSpec(memory_space=pl.ANY),
        out_shape=jax.ShapeDtypeStruct((n_rows, w), xp.dtype),
        scratch_shapes=[pltpu.SemaphoreType.DMA],
        input_output_aliases={2: 0},
        compiler_params=_params("arbitrary"),
        name="dispatch_rows",
    )(dest, xp, xs0)


def _expert_changed(be_ref):
    m = pl.program_id(1)
    return (m == 0) | (be_ref[m] != be_ref[jnp.maximum(m - 1, 0)])


def _gateup_kernel(be_ref, xs_ref, wg_ref, wu_ref, bg_ref, bu_ref, h_ref, wgb_ref, wub_ref):
    @pl.when(_expert_changed(be_ref))
    def _():
        wgb_ref[...] = wg_ref[...].astype(BF16)
        wub_ref[...] = wu_ref[...].astype(BF16)

    xp = xs_ref[...]
    half = xp.shape[1]
    xa = pltpu.bitcast(xp << 16, F32).astype(BF16)
    xb = pltpu.bitcast(xp & jnp.uint32(0xFFFF0000), F32).astype(BF16)
    gte = _dot(xa, wgb_ref[0:half, :]) + _dot(xb, wgb_ref[half:, :]) + bg_ref[...]
    up = _dot(xa, wub_ref[0:half, :]) + _dot(xb, wub_ref[half:, :]) + bu_ref[...]
    gte = jnp.minimum(gte, SWIGLU_LIMIT)
    up = jnp.clip(up, -SWIGLU_LIMIT, SWIGLU_LIMIT)
    h_ref[...] = (gte * jax.nn.sigmoid(SWIGLU_ALPHA * gte) * (up + 1.0)).astype(h_ref.dtype)


def _gateup(block_e, xs, w_gate, w_up, b_gate, b_up, tn):
    n_rows, half = xs.shape
    n_exp, d, f = w_gate.shape
    nb = n_rows // EXPERT_BLOCK
    w_spec = pl.BlockSpec((None, d, tn), lambda n, m, be: (be[m], 0, n))
    b_spec = pl.BlockSpec((None, 1, tn), lambda n, m, be: (be[m], 0, n))
    return pl.pallas_call(
        _gateup_kernel,
        grid_spec=pltpu.PrefetchScalarGridSpec(
            num_scalar_prefetch=1,
            grid=(f // tn, nb),
            in_specs=[pl.BlockSpec((EXPERT_BLOCK, half), lambda n, m, be: (m, 0)), w_spec, w_spec, b_spec, b_spec],
            out_specs=pl.BlockSpec((EXPERT_BLOCK, tn), lambda n, m, be: (m, n)),
            scratch_shapes=[pltpu.VMEM((d, tn), BF16), pltpu.VMEM((d, tn), BF16)]),
        out_shape=jax.ShapeDtypeStruct((n_rows, f), BF16),
        compiler_params=_params("arbitrary", "arbitrary"),
        name="expert_gate_up",
    )(block_e, xs, w_gate, w_up, b_gate, b_up)


def _down_kernel(be_ref, h_ref, wd_ref, bd_ref, y_ref, wdb_ref):
    @pl.when(_expert_changed(be_ref))
    def _():
        wdb_ref[...] = wd_ref[...].astype(BF16)

    y_ref[...] = _dot(h_ref[...], wdb_ref[...]) + bd_ref[...]


def _down(block_e, h, w_down, b_down, tn):
    n_rows, f = h.shape
    n_exp, _, d = w_down.shape
    nb = n_rows // EXPERT_BLOCK
    return pl.pallas_call(
        _down_kernel,
        grid_spec=pltpu.PrefetchScalarGridSpec(
            num_scalar_prefetch=1,
            grid=(d // tn, nb),
            in_specs=[pl.BlockSpec((EXPERT_BLOCK, f), lambda n, m, be: (m, 0)),
                      pl.BlockSpec((None, f, tn), lambda n, m, be: (be[m], 0, n)),
                      pl.BlockSpec((None, 1, tn), lambda n, m, be: (be[m], 0, n))],
            out_specs=pl.BlockSpec((EXPERT_BLOCK, tn), lambda n, m, be: (m, n)),
            scratch_shapes=[pltpu.VMEM((f, tn), BF16)]),
        out_shape=jax.ShapeDtypeStruct((n_rows, d), F32),
        compiler_params=_params("arbitrary", "arbitrary"),
        name="expert_down",
    )(block_e, h, w_down, b_down)


def _combine_kernel(dest_ref, wt_ref, h_ref, y_hbm, o_ref, buf_ref, sem):
    def issue(r, _):
        for k in range(TOP_K):
            pltpu.make_async_copy(y_hbm.at[pl.ds(dest_ref[k, r], 1), :], buf_ref.at[k, pl.ds(r, 1), :], sem).start()
        return 0

    lax.fori_loop(0, LANES, issue, 0)

    def drain(r, _):
        for k in range(TOP_K):
            pltpu.make_async_copy(y_hbm.at[pl.ds(0, 1), :], buf_ref.at[k, pl.ds(r, 1), :], sem).wait()
        return 0

    lax.fori_loop(0, LANES, drain, 0)

    wt = wt_ref[...]
    eye = (lax.broadcasted_iota(jnp.int32, (LANES, LANES), 0)
           == lax.broadcasted_iota(jnp.int32, (LANES, LANES), 1))
    acc = h_ref[...]
    for k in range(TOP_K):
        wcol = jnp.sum(jnp.where(eye, wt[k:k + 1, :], 0.0), axis=1, keepdims=True)
        acc = acc + wcol * buf_ref[k]
    o_ref[...] = acc


def _combine(dest, wts, h1, y):
    t, d = h1.shape
    return pl.pallas_call(
        _combine_kernel,
        grid=(t // LANES,),
        in_specs=[pl.BlockSpec((None, TOP_K, LANES), lambda i: (i, 0, 0), memory_space=pltpu.SMEM),
                  pl.BlockSpec((TOP_K, LANES), lambda i: (0, i)),
                  pl.BlockSpec((LANES, d), lambda i: (i, 0)),
                  pl.BlockSpec(memory_space=pl.ANY)],
        out_specs=pl.BlockSpec((LANES, d), lambda i: (i, 0)),
        out_shape=jax.ShapeDtypeStruct((t, d), F32),
        scratch_shapes=[pltpu.VMEM((TOP_K, LANES, d), F32), pltpu.SemaphoreType.DMA],
        compiler_params=_params("arbitrary"),
        name="combine_rows",
    )(dest, wts, h1, y)


def _layer(x, meta, norm_mix, w_in, b_forget, b_branch_gate, q_norm, k_norm, w_pool_group, pool_scale,
           w_branch_pool, w_branch_attn, w_out, norm_ffn, w_router, b_router, w_gate, b_gate, w_up, b_up,
           w_down, b_down):
    batch, seq, d = x.shape
    t = batch * seq
    n_meta = meta.shape[0]
    pool_w = w_pool_group.shape[0] * w_pool_group.shape[1]
    attn_w = w_branch_attn.shape[0]
    n_heads = attn_w // HEAD_DIM
    n_exp = w_router.shape[1]
    col_q = pool_w
    col_zf = pool_w + 3 * attn_w
    col_zg = col_zf + n_heads
    assert n_heads <= LANES and col_zf % LANES == 0

    tm = min(1024, t)
    x2 = x.reshape(t, d)

    hn = _rmsnorm(x2, norm_mix.reshape(1, d), min(256, t))
    hn_meta = _rmsnorm(meta, norm_mix.reshape(1, d), n_meta)

    uqkv, uqkv_meta = _proj(hn, hn_meta, w_in, None, col0=0, ncols=col_zf, tn=512, tm=tm,
                            out_dtype=BF16, name="in_proj_uqkv")
    zf, zf_meta = _proj(hn, hn_meta, w_in, None, col0=col_zf, ncols=LANES, tn=LANES, tm=tm,
                        out_dtype=F32, name="in_proj_forget")
    gates = _proj(hn, None, w_in[:, col_zg:], b_branch_gate.reshape(1, 2 * d), col0=0, ncols=2 * d, tn=512, tm=tm,
                  out_dtype=BF16, name="in_proj_gates")

    bf_pad = jnp.pad(b_forget.reshape(1, n_heads), ((0, 0), (0, LANES - n_heads)))
    zfm_pad = jnp.pad(zf_meta, ((0, LANES - n_meta), (0, 0)))
    cr, mbias = _cumsum(zf, zfm_pad, bf_pad, batch, n_heads, n_meta)

    y_pool = _pool(uqkv, uqkv_meta, w_pool_group, pool_scale.reshape(1, pool_w), batch, min(512, seq))
    y_attn = _attention(uqkv, uqkv_meta, cr, mbias, q_norm.reshape(1, HEAD_DIM), k_norm.reshape(1, HEAD_DIM),
                        batch, n_heads, col_q, min(256, seq))

    merged = _merge(y_pool, y_attn, w_branch_pool, w_branch_attn, gates, tm, 512)
    h1 = _outproj(merged, w_out, x2, tm, 512)

    xp, idx, wts = _router(h1, norm_ffn.reshape(1, d), w_router.T, b_router.reshape(n_exp, 1), min(256, t))
    n_rows_tok = t * TOP_K
    n_blocks = -(-(n_rows_tok + n_exp * (EXPERT_BLOCK - 1)) // EXPERT_BLOCK)
    n_blocks_pad = -(-n_blocks // LANES) * LANES
    dest, block_e = _plan(idx, n_exp, n_blocks_pad)
    xs = _dispatch(dest, xp, n_blocks * EXPERT_BLOCK)
    be = block_e.reshape(n_blocks_pad)
    f = w_gate.shape[2]
    hmid = _gateup(be, xs, w_gate, w_up, b_gate.reshape(n_exp, 1, f), b_up.reshape(n_exp, 1, f), 512)
    y = _down(be, hmid, w_down, b_down.reshape(n_exp, 1, d), 1024)
    out = _combine(dest, wts, h1, y)
    return out.reshape(batch, seq, d)


def kernel(x, meta_tokens, norm_mix, w_in, b_forget, b_branch_gate, q_norm, k_norm, w_pool_group, pool_scale,
           w_branch_pool, w_branch_attn, w_out, norm_ffn, w_router, b_router, w_gate, b_gate, w_up, b_up,
           w_down, b_down):
    depth = norm_mix.shape[0]
    assert depth == 1, "the fused layer pipeline drops the meta rows after the (single) layer"
    return _layer(x, meta_tokens, norm_mix[0], w_in[0], b_forget[0], b_branch_gate[0], q_norm[0], k_norm[0],
                  w_pool_group[0], pool_scale[0], w_branch_pool[0], w_branch_attn[0], w_out[0], norm_ffn[0],
                  w_router[0], b_router[0], w_gate[0], b_gate[0], w_up[0], b_up[0], w_down[0], b_down[0])
```

```python
import functools

import jax
import jax.numpy as jnp
from jax import lax
from jax.experimental import pallas as pl
from jax.experimental.pallas import tpu as pltpu

NORM_EPS = 1e-6
HEAD_DIM = 128
POOL_WINDOWS = (2, 4, 8, 16)
TOP_K = 4
SWIGLU_LIMIT = 7.0
SWIGLU_ALPHA = 1.702
EXPERT_BLOCK = 128
LANES = 128
NEG_BIG = -1e30
VMEM_LIMIT = 56 * 1024 * 1024

F32 = jnp.float32
BF16 = jnp.bfloat16


def _params(*sem, vmem=VMEM_LIMIT):
    return pltpu.CompilerParams(dimension_semantics=sem, vmem_limit_bytes=vmem)


def _dot(a, b):
    return jnp.dot(a, b, preferred_element_type=F32)


def _dot_nt(a, b):
    return lax.dot_general(a, b, (((1,), (1,)), ((), ())), preferred_element_type=F32)


def _rms(x, gain):
    ms = jnp.mean(x * x, axis=-1, keepdims=True)
    return x * lax.rsqrt(ms + NORM_EPS) * gain


def _rmsnorm_kernel(x_ref, g_ref, o_ref):
    o_ref[...] = _rms(x_ref[...], g_ref[...]).astype(o_ref.dtype)


def _rmsnorm(x, gain, tm):
    rows, d = x.shape
    return pl.pallas_call(
        _rmsnorm_kernel,
        grid=(rows // tm,),
        in_specs=[pl.BlockSpec((tm, d), lambda i: (i, 0)), pl.BlockSpec((1, d), lambda i: (0, 0))],
        out_specs=pl.BlockSpec((tm, d), lambda i: (i, 0)),
        out_shape=jax.ShapeDtypeStruct((rows, d), BF16),
        compiler_params=_params("arbitrary"),
        name="rmsnorm",
    )(x, gain)


def _proj_kernel(*refs, has_meta, gate_bias):
    it = iter(refs)
    x_ref = next(it)
    xm_ref = next(it) if has_meta else None
    w_ref = next(it)
    b_ref = next(it) if gate_bias else None
    o_ref = next(it)
    om_ref = next(it) if has_meta else None
    wbf_ref = next(it)

    @pl.when(pl.program_id(1) == 0)
    def _():
        wbf_ref[...] = w_ref[...].astype(BF16)
        if has_meta:
            om_ref[...] = _dot(xm_ref[...], wbf_ref[...]).astype(om_ref.dtype)

    acc = _dot(x_ref[...], wbf_ref[...])
    if gate_bias:
        acc = jax.nn.sigmoid(acc + b_ref[...])
    o_ref[...] = acc.astype(o_ref.dtype)


def _proj(x, xm, w, bias, *, col0, ncols, tn, tm, out_dtype, name):
    rows, k = x.shape
    assert col0 % tn == 0 and ncols % tn == 0 and rows % tm == 0
    jb = col0 // tn
    in_specs = [pl.BlockSpec((tm, k), lambda j, i: (i, 0))]
    args = [x]
    if xm is not None:
        in_specs.append(pl.BlockSpec(xm.shape, lambda j, i: (0, 0)))
        args.append(xm)
    in_specs.append(pl.BlockSpec((k, tn), lambda j, i: (0, jb + j)))
    args.append(w)
    if bias is not None:
        in_specs.append(pl.BlockSpec((1, tn), lambda j, i: (0, j)))
        args.append(bias)
    out_specs = [pl.BlockSpec((tm, tn), lambda j, i: (i, j))]
    out_shape = [jax.ShapeDtypeStruct((rows, ncols), out_dtype)]
    if xm is not None:
        out_specs.append(pl.BlockSpec((xm.shape[0], tn), lambda j, i: (0, j)))
        out_shape.append(jax.ShapeDtypeStruct((xm.shape[0], ncols), out_dtype))
    res = pl.pallas_call(
        functools.partial(_proj_kernel, has_meta=xm is not None, gate_bias=bias is not None),
        grid=(ncols // tn, rows // tm),
        in_specs=in_specs,
        out_specs=out_specs,
        out_shape=out_shape,
        scratch_shapes=[pltpu.VMEM((k, tn), BF16)],
        compiler_params=_params("arbitrary", "arbitrary"),
        name=name,
    )(*args)
    return res if xm is not None else res[0]


def _cumsum_kernel(zf_ref, zfm_ref, bf_ref, cr_ref, mb_ref, *, n_meta):
    seq = zf_ref.shape[0]
    bf = bf_ref[...]
    r = lax.broadcasted_iota(jnp.int32, (LANES, LANES), 0)
    c = lax.broadcasted_iota(jnp.int32, (LANES, LANES), 1)
    tri = (c <= r).astype(F32)
    sfx = (c > r).astype(F32)

    def prefix(mat, x):
        return jnp.dot(mat, x, preferred_element_type=F32, precision=lax.Precision.HIGHEST)

    n_meta_pad = zfm_ref.shape[0]
    lfm = jax.nn.log_sigmoid(zfm_ref[...] + bf)
    rows = lax.broadcasted_iota(jnp.int32, (n_meta_pad, LANES), 0)
    lfm = jnp.where(rows < n_meta, lfm, 0.0)
    mbias = jnp.where(rows < n_meta, prefix(sfx, lfm), NEG_BIG)
    mbias_t = mbias.T
    for h in range(mb_ref.shape[0]):
        mb_ref[h] = mbias_t[h:h + 1, :]

    def body(blk, carry):
        off = pl.multiple_of(blk * LANES, LANES)
        lf = jax.nn.log_sigmoid(zf_ref[pl.ds(off, LANES), :] + bf)
        cs = prefix(tri, lf) + carry
        cs_t = cs.T
        for h in range(cr_ref.shape[0]):
            cr_ref[h, :, pl.ds(off, LANES)] = cs_t[h:h + 1, :]
        return cs[LANES - 1:LANES, :]

    lax.fori_loop(0, seq // LANES, body, jnp.zeros((1, LANES), F32))


def _cumsum(zf, zfm_pad, bf_pad, batch, n_heads, n_meta):
    t = zf.shape[0]
    seq = t // batch
    return pl.pallas_call(
        functools.partial(_cumsum_kernel, n_meta=n_meta),
        grid=(batch,),
        in_specs=[pl.BlockSpec((seq, LANES), lambda b: (b, 0)),
                  pl.BlockSpec(zfm_pad.shape, lambda b: (0, 0)),
                  pl.BlockSpec((1, LANES), lambda b: (0, 0))],
        out_specs=[pl.BlockSpec((None, n_heads, 1, seq), lambda b: (b, 0, 0, 0)),
                   pl.BlockSpec((n_heads, 1, LANES), lambda b: (0, 0, 0))],
        out_shape=[jax.ShapeDtypeStruct((batch, n_heads, 1, seq), F32),
                   jax.ShapeDtypeStruct((n_heads, 1, LANES), F32)],
        compiler_params=_params("arbitrary"),
        name="forget_cumsum",
    )(zf, zfm_pad, bf_pad)


def _pool_kernel(u_ref, um_ref, wp_ref, ps_ref, o_ref, ext_ref, wbf_ref):
    ts = u_ref.shape[0]
    halo = um_ref.shape[0]
    gw = wp_ref.shape[1]
    i = pl.program_id(1)

    @pl.when((pl.program_id(0) == 0) & (i == 0))
    def _():
        wbf_ref[...] = wp_ref[...].astype(BF16)

    @pl.when(i == 0)
    def _():
        ext_ref[0:halo, :] = um_ref[...].astype(F32)

    @pl.when(i > 0)
    def _():
        ext_ref[0:halo, :] = ext_ref[ts:ts + halo, :]

    ext_ref[halo:halo + ts, :] = u_ref[...].astype(F32)
    for g, w in enumerate(POOL_WINDOWS):
        cols = slice(g * gw, (g + 1) * gw)
        uf = ext_ref[halo:halo + ts, cols]
        acc = uf
        for d in range(1, w):
            acc = acc + ext_ref[halo - d:halo - d + ts, cols]
        pooled = acc * (1.0 / w) - uf
        y = _dot(pooled.astype(BF16), wbf_ref[g]) * ps_ref[:, cols]
        o_ref[:, cols] = y.astype(o_ref.dtype)


def _pool(uqkv, uqkv_meta, w_pool, pool_scale, batch, ts):
    t = uqkv.shape[0]
    seq = t // batch
    ng, gw, _ = w_pool.shape
    pw = ng * gw
    halo = uqkv_meta.shape[0]
    assert halo >= max(POOL_WINDOWS) and seq % ts == 0
    nt = seq // ts
    return pl.pallas_call(
        _pool_kernel,
        grid=(batch, nt),
        in_specs=[pl.BlockSpec((ts, pw), lambda b, i: (b * nt + i, 0)),
                  pl.BlockSpec((halo, pw), lambda b, i: (0, 0)),
                  pl.BlockSpec(w_pool.shape, lambda b, i: (0, 0, 0)),
                  pl.BlockSpec((1, pw), lambda b, i: (0, 0))],
        out_specs=pl.BlockSpec((ts, pw), lambda b, i: (b * nt + i, 0)),
        out_shape=jax.ShapeDtypeStruct((t, pw), BF16),
        scratch_shapes=[pltpu.VMEM((halo + ts, pw), F32), pltpu.VMEM(w_pool.shape, BF16)],
        compiler_params=_params("arbitrary", "arbitrary"),
        name="pool_mixer",
    )(uqkv, uqkv_meta, w_pool, pool_scale)


def _attn_kernel(q_ref, k_ref, v_ref, km_ref, vm_ref, cr_ref, mb_ref, qg_ref, kg_ref, o_ref,
                 kn_ref, vt_ref, cb_ref, kmn_ref, vmt_ref, mbc_ref):
    tq = q_ref.shape[0]
    seq = k_ref.shape[0]
    n_meta = km_ref.shape[0]
    qi = pl.program_id(2)

    @pl.when(qi == 0)
    def _():
        kn_ref[...] = _rms(k_ref[...].astype(F32), kg_ref[...]).astype(BF16)

        def chunk(i, _):
            off = pl.multiple_of(i * LANES, LANES)
            vt_ref[:, pl.ds(off, LANES)] = v_ref[pl.ds(off, LANES), :].astype(F32).T.astype(BF16)
            cb_ref[pl.ds(off, LANES), :] = jnp.broadcast_to(cr_ref[:, pl.ds(off, LANES)], (LANES, LANES)).T
            return 0

        lax.fori_loop(0, seq // LANES, chunk, 0)
        pad = jnp.zeros((LANES - n_meta, HEAD_DIM), F32)
        kmn = jnp.concatenate([_rms(km_ref[...].astype(F32), kg_ref[...]), pad], axis=0)
        kmn_ref[...] = kmn.astype(BF16)
        vmt_ref[...] = jnp.concatenate([vm_ref[...].astype(F32), pad], axis=0).T.astype(BF16)
        mbc_ref[...] = jnp.broadcast_to(mb_ref[...], (LANES, LANES)).T

    qn = (_rms(q_ref[...].astype(F32), qg_ref[...]) * (HEAD_DIM ** -0.5)).astype(BF16)
    q0 = pl.multiple_of(qi * tq, tq)
    cref = cb_ref[pl.ds(q0, 1), :]

    def lanes_tq(x):
        return jnp.concatenate([x] * (tq // LANES), axis=1)

    def update(carry, st, vt_blk):
        m, l, acc = carry
        m_new = jnp.maximum(m, jnp.max(st, axis=0, keepdims=True))
        alpha = jnp.exp(m - m_new)
        p = jnp.exp(st - m_new)
        l = alpha * l + jnp.sum(p, axis=0, keepdims=True)
        acc = alpha * acc + _dot(vt_blk, p.astype(BF16))
        return m_new, l, acc

    def scores(off):
        return _dot_nt(kn_ref[pl.ds(off, tq), :], qn) + lanes_tq(cref - cb_ref[pl.ds(off, tq), :])

    def body(j, carry):
        off = pl.multiple_of(j * tq, tq)
        return update(carry, scores(off), vt_ref[:, pl.ds(off, tq)])

    carry = (jnp.full((1, tq), NEG_BIG, F32), jnp.zeros((1, tq), F32), jnp.zeros((HEAD_DIM, tq), F32))
    carry = lax.fori_loop(0, qi, body, carry)

    key = lax.broadcasted_iota(jnp.int32, (tq, tq), 0)
    qry = lax.broadcasted_iota(jnp.int32, (tq, tq), 1)
    st_meta = _dot_nt(kmn_ref[...], qn) + lanes_tq(cref + mbc_ref[...])
    st = jnp.concatenate([st_meta, jnp.where(key <= qry, scores(q0), NEG_BIG)], axis=0)
    vt = jnp.concatenate([vmt_ref[...], vt_ref[:, pl.ds(q0, tq)]], axis=1)
    m, l, acc = update(carry, st, vt)
    o_ref[...] = (acc / l).T.astype(o_ref.dtype)


def _attention(uqkv, uqkv_meta, cr, mbias, q_gain, k_gain, batch, n_heads, col_q, tq):
    t = uqkv.shape[0]
    seq = t // batch
    n_meta = uqkv_meta.shape[0]
    nq = seq // tq
    bq = col_q // HEAD_DIM
    bk = bq + n_heads
    bv = bk + n_heads
    return pl.pallas_call(
        _attn_kernel,
        grid=(batch, n_heads, nq),
        in_specs=[pl.BlockSpec((tq, HEAD_DIM), lambda b, h, i: (b * nq + i, bq + h)),
                  pl.BlockSpec((seq, HEAD_DIM), lambda b, h, i: (b, bk + h)),
                  pl.BlockSpec((seq, HEAD_DIM), lambda b, h, i: (b, bv + h)),
                  pl.BlockSpec((n_meta, HEAD_DIM), lambda b, h, i: (0, bk + h)),
                  pl.BlockSpec((n_meta, HEAD_DIM), lambda b, h, i: (0, bv + h)),
                  pl.BlockSpec((None, None, 1, seq), lambda b, h, i: (b, h, 0, 0)),
                  pl.BlockSpec((None, 1, LANES), lambda b, h, i: (h, 0, 0)),
                  pl.BlockSpec((1, HEAD_DIM), lambda b, h, i: (0, 0)),
                  pl.BlockSpec((1, HEAD_DIM), lambda b, h, i: (0, 0))],
        out_specs=pl.BlockSpec((tq, HEAD_DIM), lambda b, h, i: (b * nq + i, h)),
        out_shape=jax.ShapeDtypeStruct((t, n_heads * HEAD_DIM), BF16),
        scratch_shapes=[pltpu.VMEM((seq, HEAD_DIM), BF16),
                        pltpu.VMEM((HEAD_DIM, seq), BF16),
                        pltpu.VMEM((seq, LANES), F32),
                        pltpu.VMEM((LANES, HEAD_DIM), BF16),
                        pltpu.VMEM((HEAD_DIM, LANES), BF16),
                        pltpu.VMEM((LANES, LANES), F32)],
        compiler_params=_params("arbitrary", "arbitrary", "arbitrary"),
        name="forgetting_attention",
    )(uqkv, uqkv, uqkv, uqkv_meta, uqkv_meta, cr, mbias, q_gain, k_gain)


def _merge_kernel(yp_ref, ya_ref, wp_ref, wa_ref, gp_ref, ga_ref, o_ref, wpb_ref, wab_ref):
    @pl.when(pl.program_id(1) == 0)
    def _():
        wpb_ref[...] = wp_ref[...].astype(BF16)
        wab_ref[...] = wa_ref[...].astype(BF16)

    a = _dot(yp_ref[...], wpb_ref[...])
    b = _dot(ya_ref[...], wab_ref[...])
    o_ref[...] = (gp_ref[...].astype(F32) * a + ga_ref[...].astype(F32) * b).astype(o_ref.dtype)


def _merge(y_pool, y_attn, w_bp, w_ba, gates, tm, tn):
    t, kp = y_pool.shape
    ka = y_attn.shape[1]
    d = w_bp.shape[1]
    nj = d // tn
    return pl.pallas_call(
        _merge_kernel,
        grid=(nj, t // tm),
        in_specs=[pl.BlockSpec((tm, kp), lambda j, i: (i, 0)),
                  pl.BlockSpec((tm, ka), lambda j, i: (i, 0)),
                  pl.BlockSpec((kp, tn), lambda j, i: (0, j)),
                  pl.BlockSpec((ka, tn), lambda j, i: (0, j)),
                  pl.BlockSpec((tm, tn), lambda j, i: (i, j)),
                  pl.BlockSpec((tm, tn), lambda j, i: (i, nj + j))],
        out_specs=pl.BlockSpec((tm, tn), lambda j, i: (i, j)),
        out_shape=jax.ShapeDtypeStruct((t, d), BF16),
        scratch_shapes=[pltpu.VMEM((kp, tn), BF16), pltpu.VMEM((ka, tn), BF16)],
        compiler_params=_params("arbitrary", "arbitrary"),
        name="branch_merge",
    )(y_pool, y_attn, w_bp, w_ba, gates, gates)


def _outproj_kernel(m_ref, w_ref, x_ref, o_ref, wbf_ref):
    @pl.when(pl.program_id(1) == 0)
    def _():
        wbf_ref[...] = w_ref[...].astype(BF16)

    o_ref[...] = x_ref[...] + _dot(m_ref[...], wbf_ref[...])


def _outproj(merged, w_out, x, tm, tn):
    t, k = merged.shape
    d = w_out.shape[1]
    return pl.pallas_call(
        _outproj_kernel,
        grid=(d // tn, t // tm),
        in_specs=[pl.BlockSpec((tm, k), lambda j, i: (i, 0)),
                  pl.BlockSpec((k, tn), lambda j, i: (0, j)),
                  pl.BlockSpec((tm, tn), lambda j, i: (i, j))],
        out_specs=pl.BlockSpec((tm, tn), lambda j, i: (i, j)),
        out_shape=jax.ShapeDtypeStruct((t, d), F32),
        scratch_shapes=[pltpu.VMEM((k, tn), BF16)],
        compiler_params=_params("arbitrary", "arbitrary"),
        name="out_proj",
    )(merged, w_out, x)


def _router_kernel(h_ref, g_ref, wr_ref, br_ref, xp_ref, idx_ref, wt_ref):
    y = _rms(h_ref[...], g_ref[...])
    y_hi = y.astype(BF16)
    y_lo = (y - y_hi.astype(F32)).astype(BF16)
    w = wr_ref[...]
    w_hi = w.astype(BF16)
    w_lo = (w - w_hi.astype(F32)).astype(BF16)
    logits = _dot_nt(w_hi, y_hi) + (_dot_nt(w_hi, y_lo) + _dot_nt(w_lo, y_hi)) + br_ref[...]

    half = y.shape[1] // 2
    bits = pltpu.bitcast(y_hi.astype(F32), jnp.uint32)
    xp_ref[...] = (bits[:, :half] >> 16) | bits[:, half:]

    n_exp = logits.shape[0]
    eid = lax.broadcasted_iota(jnp.int32, logits.shape, 0)
    vals = logits
    top_v, top_i = [], []
    for _ in range(TOP_K):
        mx = jnp.max(vals, axis=0, keepdims=True)
        sel = jnp.min(jnp.where(vals == mx, eid, n_exp), axis=0, keepdims=True)
        top_v.append(mx)
        top_i.append(sel)
        vals = jnp.where(eid == sel, -jnp.inf, vals)
    ex = [jnp.exp(v - top_v[0]) for v in top_v]
    den = ex[0] + ex[1] + ex[2] + ex[3]
    for k in range(TOP_K):
        idx_ref[k:k + 1, :] = top_i[k]
        wt_ref[k:k + 1, :] = ex[k] / den


def _router(h1, gain, w_router_t, b_router, tm):
    t, d = h1.shape
    n_exp = w_router_t.shape[0]
    return pl.pallas_call(
        _router_kernel,
        grid=(t // tm,),
        in_specs=[pl.BlockSpec((tm, d), lambda i: (i, 0)),
                  pl.BlockSpec((1, d), lambda i: (0, 0)),
                  pl.BlockSpec((n_exp, d), lambda i: (0, 0)),
                  pl.BlockSpec((n_exp, 1), lambda i: (0, 0))],
        out_specs=[pl.BlockSpec((tm, d // 2), lambda i: (i, 0)),
                   pl.BlockSpec((TOP_K, tm), lambda i: (0, i)),
                   pl.BlockSpec((TOP_K, tm), lambda i: (0, i))],
        out_shape=[jax.ShapeDtypeStruct((t, d // 2), jnp.uint32),
                   jax.ShapeDtypeStruct((TOP_K, t), jnp.int32),
                   jax.ShapeDtypeStruct((TOP_K, t), F32)],
        compiler_params=_params("arbitrary"),
        name="router_topk",
    )(h1, gain, w_router_t, b_router)


def _plan_kernel(idx_ref, dest_ref, grp_ref, rank_ref, *, n_exp):
    t = idx_ref.shape[1]
    nblk = t // LANES
    eid = lax.broadcasted_iota(jnp.int32, (n_exp, LANES), 0)
    r = lax.broadcasted_iota(jnp.int32, (LANES, LANES), 0)
    c = lax.broadcasted_iota(jnp.int32, (LANES, LANES), 1)
    upper = (r <= c).astype(BF16)

    def member(off):
        m = jnp.zeros((n_exp, LANES), F32)
        for k in range(TOP_K):
            m = m + (idx_ref[k:k + 1, pl.ds(off, LANES)] == eid).astype(F32)
        return m

    def count(blk, carry):
        off = pl.multiple_of(blk * LANES, LANES)
        m = member(off)
        cs = _dot(m.astype(BF16), upper) + carry
        rank_ref[:, pl.ds(off, LANES)] = cs - m
        return carry + jnp.sum(m, axis=1, keepdims=True)

    counts = lax.fori_loop(0, nblk, count, jnp.zeros((n_exp, 1), F32))
    padded = jnp.floor((counts + (EXPERT_BLOCK - 1)) * (1.0 / EXPERT_BLOCK)) * EXPERT_BLOCK
    er = lax.broadcasted_iota(jnp.int32, (n_exp, n_exp), 0)
    ec = lax.broadcasted_iota(jnp.int32, (n_exp, n_exp), 1)
    lower = (ec <= er).astype(BF16)
    pad_end = _dot(lower, jnp.broadcast_to(padded, (n_exp, LANES)).astype(BF16))[:, 0:1]
    pad_start = pad_end - padded

    def place(blk, _):
        off = pl.multiple_of(blk * LANES, LANES)
        pos = pad_start + rank_ref[:, pl.ds(off, LANES)]
        for k in range(TOP_K):
            sel = idx_ref[k:k + 1, pl.ds(off, LANES)] == eid
            d = jnp.sum(jnp.where(sel, pos, 0.0), axis=0, keepdims=True)
            dest_ref[blk, k:k + 1, :] = d.astype(jnp.int32)
        return 0

    lax.fori_loop(0, nblk, place, 0)

    lane = lax.broadcasted_iota(jnp.int32, (n_exp, LANES), 1)
    first = pad_start * (1.0 / EXPERT_BLOCK)
    nblocks = padded * (1.0 / EXPERT_BLOCK)
    grp_ref[...] = jnp.where(lane == 0, first, jnp.where(lane == 1, nblocks, 0.0)).astype(jnp.int32)


def _plan(idx, n_exp):
    t = idx.shape[1]
    assert t <= 256 * EXPERT_BLOCK
    dest, grp = pl.pallas_call(
        functools.partial(_plan_kernel, n_exp=n_exp),
        grid=(1,),
        in_specs=[pl.BlockSpec(idx.shape, lambda i: (0, 0))],
        out_specs=[pl.BlockSpec((t // LANES, TOP_K, LANES), lambda i: (0, 0, 0)),
                   pl.BlockSpec((n_exp, LANES), lambda i: (0, 0))],
        out_shape=[jax.ShapeDtypeStruct((t // LANES, TOP_K, LANES), jnp.int32),
                   jax.ShapeDtypeStruct((n_exp, LANES), jnp.int32)],
        scratch_shapes=[pltpu.VMEM((n_exp, t), F32)],
        compiler_params=_params("arbitrary"),
        name="dispatch_plan",
    )(idx)
    return dest, grp[:, 0], grp[:, 1]


def _dispatch_kernel(gs_ref, gn_ref, dest_ref, xp_ref, xs_hbm, zero_ref, sem, zsem):
    n_exp = gs_ref.shape[0]

    def zero_block(m):
        return pltpu.make_async_copy(zero_ref, xs_hbm.at[pl.ds(m * EXPERT_BLOCK, EXPERT_BLOCK), :], zsem)

    def zero_copy(e):
        return zero_block(gs_ref[e] + gn_ref[e] - 1)

    @pl.when(pl.program_id(0) == 0)
    def _():
        zero_ref[...] = jnp.zeros_like(zero_ref)
        used = gs_ref[n_exp - 1] + gn_ref[n_exp - 1]
        total = xs_hbm.shape[0] // EXPERT_BLOCK

        def start(e, _):
            @pl.when(gn_ref[e] > 0)
            def _():
                zero_copy(e).start()
            return 0

        def wait(e, _):
            @pl.when(gn_ref[e] > 0)
            def _():
                zero_copy(e).wait()
            return 0

        lax.fori_loop(0, n_exp, start, 0)
        lax.fori_loop(used, total, lambda m, _: (zero_block(m).start(), 0)[1], 0)
        lax.fori_loop(0, n_exp, wait, 0)
        lax.fori_loop(used, total, lambda m, _: (zero_block(m).wait(), 0)[1], 0)

    def row_copy(r, k):
        return pltpu.make_async_copy(xp_ref.at[pl.ds(r, 1), :], xs_hbm.at[pl.ds(dest_ref[k, r], 1), :], sem)

    def issue(r, _):
        for k in range(TOP_K):
            row_copy(r, k).start()
        return 0

    def drain(r, _):
        for k in range(TOP_K):
            row_copy(r, k).wait()
        return 0

    lax.fori_loop(0, LANES, issue, 0)
    lax.fori_loop(0, LANES, drain, 0)


def _dispatch(grp_start, grp_blocks, dest, xp, n_rows):
    t, w = xp.shape
    return pl.pallas_call(
        _dispatch_kernel,
        grid_spec=pltpu.PrefetchScalarGridSpec(
            num_scalar_prefetch=2,
            grid=(t // LANES,),
            in_specs=[pl.BlockSpec((None, TOP_K, LANES), lambda i, gs, gn: (i, 0, 0), memory_space=pltpu.SMEM),
                      pl.BlockSpec((LANES, w), lambda i, gs, gn: (i, 0))],
            out_specs=pl.BlockSpec(memory_space=pl.ANY),
            scratch_shapes=[pltpu.VMEM((EXPERT_BLOCK, w), xp.dtype), pltpu.SemaphoreType.DMA,
                            pltpu.SemaphoreType.DMA]),
        out_shape=jax.ShapeDtypeStruct((n_rows, w), xp.dtype),
        compiler_params=_params("arbitrary"),
        name="dispatch_rows",
    )(grp_start, grp_blocks, dest, xp)


def _group_loop(gs_ref, gn_ref, src_hbm, dst_hbm, col0, in_buf, out_buf, sem_in, sem_out, compute):
    e = pl.program_id(1)
    nb = gn_ref[e]
    b0 = gs_ref[e]
    width = out_buf.shape[2]

    def fetch(i, slot):
        rows = pl.ds(pl.multiple_of((b0 + i) * EXPERT_BLOCK, EXPERT_BLOCK), EXPERT_BLOCK)
        return pltpu.make_async_copy(src_hbm.at[rows, :], in_buf.at[slot], sem_in.at[slot])

    def store(i, slot):
        rows = pl.ds(pl.multiple_of((b0 + i) * EXPERT_BLOCK, EXPERT_BLOCK), EXPERT_BLOCK)
        return pltpu.make_async_copy(out_buf.at[slot], dst_hbm.at[rows, pl.ds(col0, width)], sem_out.at[slot])

    @pl.when(e == 0)
    def _():
        last = gs_ref.shape[0] - 1
        used = gs_ref[last] + gn_ref[last]
        total = dst_hbm.shape[0] // EXPERT_BLOCK
        out_buf[0] = jnp.zeros(out_buf.shape[1:], out_buf.dtype)

        def tail(m):
            rows = pl.ds(pl.multiple_of(m * EXPERT_BLOCK, EXPERT_BLOCK), EXPERT_BLOCK)
            return pltpu.make_async_copy(out_buf.at[0], dst_hbm.at[rows, pl.ds(col0, width)], sem_out.at[0])

        lax.fori_loop(used, total, lambda m, _: (tail(m).start(), 0)[1], 0)
        lax.fori_loop(used, total, lambda m, _: (tail(m).wait(), 0)[1], 0)

    @pl.when(nb > 0)
    def _():
        fetch(0, 0).start()

    def step(i, _):
        slot = i & 1
        fetch(i, slot).wait()

        @pl.when(i + 1 < nb)
        def _():
            fetch(i + 1, 1 - slot).start()

        res = compute(in_buf[slot])

        @pl.when(i >= 2)
        def _():
            store(i - 2, slot).wait()

        out_buf[slot] = res
        store(i, slot).start()
        return 0

    lax.fori_loop(0, nb, step, 0)

    @pl.when(nb >= 2)
    def _():
        store(nb - 2, nb & 1).wait()

    @pl.when(nb >= 1)
    def _():
        store(nb - 1, (nb - 1) & 1).wait()


def _gateup_kernel(gs_ref, gn_ref, xs_hbm, wg_ref, wu_ref, bg_ref, bu_ref, h_hbm,
                   wgb_ref, wub_ref, xbuf, hbuf, sem_in, sem_out):
    tn = wgb_ref.shape[1]
    half = xbuf.shape[2]

    @pl.when(gn_ref[pl.program_id(1)] > 0)
    def _():
        wgb_ref[...] = wg_ref[...].astype(BF16)
        wub_ref[...] = wu_ref[...].astype(BF16)

    def compute(xp):
        xa = pltpu.bitcast(xp << 16, F32).astype(BF16)
        xb = pltpu.bitcast(xp & jnp.uint32(0xFFFF0000), F32).astype(BF16)
        gte = _dot(xa, wgb_ref[0:half, :]) + _dot(xb, wgb_ref[half:, :]) + bg_ref[...]
        up = _dot(xa, wub_ref[0:half, :]) + _dot(xb, wub_ref[half:, :]) + bu_ref[...]
        gte = jnp.minimum(gte, SWIGLU_LIMIT)
        up = jnp.clip(up, -SWIGLU_LIMIT, SWIGLU_LIMIT)
        return (gte * jax.nn.sigmoid(SWIGLU_ALPHA * gte) * (up + 1.0)).astype(BF16)

    col0 = pl.multiple_of(pl.program_id(0) * tn, tn)
    _group_loop(gs_ref, gn_ref, xs_hbm, h_hbm, col0, xbuf, hbuf, sem_in, sem_out, compute)


def _gateup(grp_start, grp_blocks, xs, w_gate, w_up, b_gate, b_up, tn):
    n_rows, half = xs.shape
    n_exp, d, f = w_gate.shape
    w_spec = pl.BlockSpec((None, d, tn), lambda n, e, gs, gn: (e, 0, n))
    b_spec = pl.BlockSpec((None, 1, tn), lambda n, e, gs, gn: (e, 0, n))
    return pl.pallas_call(
        _gateup_kernel,
        grid_spec=pltpu.PrefetchScalarGridSpec(
            num_scalar_prefetch=2,
            grid=(f // tn, n_exp),
            in_specs=[pl.BlockSpec(memory_space=pl.ANY), w_spec, w_spec, b_spec, b_spec],
            out_specs=pl.BlockSpec(memory_space=pl.ANY),
            scratch_shapes=[pltpu.VMEM((d, tn), BF16), pltpu.VMEM((d, tn), BF16),
                            pltpu.VMEM((2, EXPERT_BLOCK, half), xs.dtype), pltpu.VMEM((2, EXPERT_BLOCK, tn), BF16),
                            pltpu.SemaphoreType.DMA((2,)), pltpu.SemaphoreType.DMA((2,))]),
        out_shape=jax.ShapeDtypeStruct((n_rows, f), BF16),
        compiler_params=_params("arbitrary", "arbitrary"),
        name="expert_gate_up",
    )(grp_start, grp_blocks, xs, w_gate, w_up, b_gate, b_up)


def _down_kernel(gs_ref, gn_ref, h_hbm, wd_ref, bd_ref, y_hbm, wdb_ref, hbuf, ybuf, sem_in, sem_out):
    tn = wdb_ref.shape[1]

    @pl.when(gn_ref[pl.program_id(1)] > 0)
    def _():
        wdb_ref[...] = wd_ref[...].astype(BF16)

    def compute(h):
        return _dot(h, wdb_ref[...]) + bd_ref[...]

    col0 = pl.multiple_of(pl.program_id(0) * tn, tn)
    _group_loop(gs_ref, gn_ref, h_hbm, y_hbm, col0, hbuf, ybuf, sem_in, sem_out, compute)


def _down(grp_start, grp_blocks, h, w_down, b_down, tn):
    n_rows, f = h.shape
    n_exp, _, d = w_down.shape
    return pl.pallas_call(
        _down_kernel,
        grid_spec=pltpu.PrefetchScalarGridSpec(
            num_scalar_prefetch=2,
            grid=(d // tn, n_exp),
            in_specs=[pl.BlockSpec(memory_space=pl.ANY),
                      pl.BlockSpec((None, f, tn), lambda n, e, gs, gn: (e, 0, n)),
                      pl.BlockSpec((None, 1, tn), lambda n, e, gs, gn: (e, 0, n))],
            out_specs=pl.BlockSpec(memory_space=pl.ANY),
            scratch_shapes=[pltpu.VMEM((f, tn), BF16),
                            pltpu.VMEM((2, EXPERT_BLOCK, f), h.dtype), pltpu.VMEM((2, EXPERT_BLOCK, tn), F32),
                            pltpu.SemaphoreType.DMA((2,)), pltpu.SemaphoreType.DMA((2,))]),
        out_shape=jax.ShapeDtypeStruct((n_rows, d), F32),
        compiler_params=_params("arbitrary", "arbitrary"),
        name="expert_down",
    )(grp_start, grp_blocks, h, w_down, b_down)


def _combine_kernel(dest_ref, wt_ref, h_ref, y_hbm, o_ref, buf_ref, sem):
    def issue(r, _):
        for k in range(TOP_K):
            pltpu.make_async_copy(y_hbm.at[pl.ds(dest_ref[k, r], 1), :], buf_ref.at[k, pl.ds(r, 1), :], sem).start()
        return 0

    lax.fori_loop(0, LANES, issue, 0)

    def drain(r, _):
        for k in range(TOP_K):
            pltpu.make_async_copy(y_hbm.at[pl.ds(0, 1), :], buf_ref.at[k, pl.ds(r, 1), :], sem).wait()
        return 0

    lax.fori_loop(0, LANES, drain, 0)

    wt = wt_ref[...]
    eye = (lax.broadcasted_iota(jnp.int32, (LANES, LANES), 0)
           == lax.broadcasted_iota(jnp.int32, (LANES, LANES), 1))
    acc = h_ref[...]
    for k in range(TOP_K):
        wcol = jnp.sum(jnp.where(eye, wt[k:k + 1, :], 0.0), axis=1, keepdims=True)
        acc = acc + wcol * buf_ref[k]
    o_ref[...] = acc


def _combine(dest, wts, h1, y):
    t, d = h1.shape
    return pl.pallas_call(
        _combine_kernel,
        grid=(t // LANES,),
        in_specs=[pl.BlockSpec((None, TOP_K, LANES), lambda i: (i, 0, 0), memory_space=pltpu.SMEM),
                  pl.BlockSpec((TOP_K, LANES), lambda i: (0, i)),
                  pl.BlockSpec((LANES, d), lambda i: (i, 0)),
                  pl.BlockSpec(memory_space=pl.ANY)],
        out_specs=pl.BlockSpec((LANES, d), lambda i: (i, 0)),
        out_shape=jax.ShapeDtypeStruct((t, d), F32),
        scratch_shapes=[pltpu.VMEM((TOP_K, LANES, d), F32), pltpu.SemaphoreType.DMA],
        compiler_params=_params("arbitrary"),
        name="combine_rows",
    )(dest, wts, h1, y)


def _layer(x, meta, norm_mix, w_in, b_forget, b_branch_gate, q_norm, k_norm, w_pool_group, pool_scale,
           w_branch_pool, w_branch_attn, w_out, norm_ffn, w_router, b_router, w_gate, b_gate, w_up, b_up,
           w_down, b_down):
    batch, seq, d = x.shape
    t = batch * seq
    n_meta = meta.shape[0]
    pool_w = w_pool_group.shape[0] * w_pool_group.shape[1]
    attn_w = w_branch_attn.shape[0]
    n_heads = attn_w // HEAD_DIM
    n_exp = w_router.shape[1]
    col_q = pool_w
    col_zf = pool_w + 3 * attn_w
    col_zg = col_zf + n_heads
    assert n_heads <= LANES and col_zf % LANES == 0

    tm = min(1024, t)
    x2 = x.reshape(t, d)

    hn = _rmsnorm(x2, norm_mix.reshape(1, d), min(256, t))
    hn_meta = _rmsnorm(meta, norm_mix.reshape(1, d), n_meta)

    uqkv, uqkv_meta = _proj(hn, hn_meta, w_in, None, col0=0, ncols=col_zf, tn=512, tm=tm,
                            out_dtype=BF16, name="in_proj_uqkv")
    zf, zf_meta = _proj(hn, hn_meta, w_in, None, col0=col_zf, ncols=LANES, tn=LANES, tm=tm,
                        out_dtype=F32, name="in_proj_forget")
    gates = _proj(hn, None, w_in[:, col_zg:], b_branch_gate.reshape(1, 2 * d), col0=0, ncols=2 * d, tn=512, tm=tm,
                  out_dtype=BF16, name="in_proj_gates")

    bf_pad = jnp.pad(b_forget.reshape(1, n_heads), ((0, 0), (0, LANES - n_heads)))
    zfm_pad = jnp.pad(zf_meta, ((0, LANES - n_meta), (0, 0)))
    cr, mbias = _cumsum(zf, zfm_pad, bf_pad, batch, n_heads, n_meta)

    y_pool = _pool(uqkv, uqkv_meta, w_pool_group, pool_scale.reshape(1, pool_w), batch, min(512, seq))
    y_attn = _attention(uqkv, uqkv_meta, cr, mbias, q_norm.reshape(1, HEAD_DIM), k_norm.reshape(1, HEAD_DIM),
                        batch, n_heads, col_q, min(512, seq))

    merged = _merge(y_pool, y_attn, w_branch_pool, w_branch_attn, gates, tm, 512)
    h1 = _outproj(merged, w_out, x2, tm, 512)

    xp, idx, wts = _router(h1, norm_ffn.reshape(1, d), w_router.T, b_router.reshape(n_exp, 1), min(256, t))
    n_blocks = -(-(t * TOP_K + n_exp * (EXPERT_BLOCK - 1)) // EXPERT_BLOCK)
    dest, grp_start, grp_blocks = _plan(idx, n_exp)
    xs = _dispatch(grp_start, grp_blocks, dest, xp, n_blocks * EXPERT_BLOCK)
    f = w_gate.shape[2]
    hmid = _gateup(grp_start, grp_blocks, xs, w_gate, w_up, b_gate.reshape(n_exp, 1, f), b_up.reshape(n_exp, 1, f), 512)
    y = _down(grp_start, grp_blocks, hmid, w_down, b_down.reshape(n_exp, 1, d), 2048)
    out = _combine(dest, wts, h1, y)
    return out.reshape(batch, seq, d)


def kernel(x, meta_tokens, norm_mix, w_in, b_forget, b_branch_gate, q_norm, k_norm, w_pool_group, pool_scale,
           w_branch_pool, w_branch_attn, w_out, norm_ffn, w_router, b_router, w_gate, b_gate, w_up, b_up,
           w_down, b_down):
    depth = norm_mix.shape[0]
    assert depth == 1, "the fused layer pipeline drops the meta rows after the (single) layer"
    return _layer(x, meta_tokens, norm_mix[0], w_in[0], b_forget[0], b_branch_gate[0], q_norm[0], k_norm[0],
                  w_pool_group[0], pool_scale[0], w_branch_pool[0], w_branch_attn[0], w_out[0], norm_ffn[0],
                  w_router[0], b_router[0], w_gate[0], b_gate[0], w_up[0], b_up[0], w_down[0], b_down[0])
```

```python
import functools

import jax
import jax.numpy as jnp
from jax import lax
from jax.experimental import pallas as pl
from jax.experimental.pallas import tpu as pltpu

NORM_EPS = 1e-6
HEAD_DIM = 128
POOL_WINDOWS = (2, 4, 8, 16)
TOP_K = 4
SWIGLU_LIMIT = 7.0
SWIGLU_ALPHA = 1.702
EXPERT_BLOCK = 128
CHUNK_BLOCKS = 4
LANES = 128
NEG_BIG = -1e30
LOG2E = 1.4426950408889634
VMEM_LIMIT = 56 * 1024 * 1024

F32 = jnp.float32
BF16 = jnp.bfloat16


def _params(*sem, vmem=VMEM_LIMIT):
    return pltpu.CompilerParams(dimension_semantics=sem, vmem_limit_bytes=vmem)


def _dot(a, b):
    return jnp.dot(a, b, preferred_element_type=F32)


def _dot_nt(a, b):
    return lax.dot_general(a, b, (((1,), (1,)), ((), ())), preferred_element_type=F32)


def _rms(x, gain):
    ms = jnp.mean(x * x, axis=-1, keepdims=True)
    return x * lax.rsqrt(ms + NORM_EPS) * gain


def _rmsnorm_kernel(x_ref, g_ref, o_ref):
    o_ref[...] = _rms(x_ref[...], g_ref[...]).astype(o_ref.dtype)


def _rmsnorm(x, gain, tm):
    rows, d = x.shape
    return pl.pallas_call(
        _rmsnorm_kernel,
        grid=(rows // tm,),
        in_specs=[pl.BlockSpec((tm, d), lambda i: (i, 0)), pl.BlockSpec((1, d), lambda i: (0, 0))],
        out_specs=pl.BlockSpec((tm, d), lambda i: (i, 0)),
        out_shape=jax.ShapeDtypeStruct((rows, d), BF16),
        compiler_params=_params("arbitrary"),
        name="rmsnorm",
    )(x, gain)


def _proj_kernel(*refs, has_meta, gate_bias, shift):
    it = iter(refs)
    x_ref = next(it)
    xm_ref = next(it) if has_meta else None
    w_ref = next(it)
    wn_ref = next(it) if shift else None
    b_ref = next(it) if gate_bias else None
    o_ref = next(it)
    om_ref = next(it) if has_meta else None
    wbf_ref = next(it)
    tn, k = w_ref.shape

    @pl.when(pl.program_id(1) == 0)
    def _():
        def chunk(c, _):
            off = pl.multiple_of(c * LANES, LANES)
            if shift:
                wt = jnp.concatenate([w_ref[shift:tn, pl.ds(off, LANES)], wn_ref[:, pl.ds(off, LANES)]], axis=0)
            else:
                wt = w_ref[:, pl.ds(off, LANES)]
            wbf_ref[pl.ds(off, LANES), :] = wt.T.astype(BF16)
            return 0

        lax.fori_loop(0, k // LANES, chunk, 0)
        if has_meta:
            om_ref[...] = _dot(xm_ref[...], wbf_ref[...]).astype(om_ref.dtype)

    acc = _dot(x_ref[...], wbf_ref[...])
    if gate_bias:
        acc = jax.nn.sigmoid(acc + b_ref[...])
    o_ref[...] = acc.astype(o_ref.dtype)


def _proj(x, xm, wt, bias, *, col0, ncols, tn, tm, out_dtype, name, shift=0):
    rows, k = x.shape
    assert col0 % tn == 0 and ncols % tn == 0 and rows % tm == 0
    jb = col0 // tn
    in_specs = [pl.BlockSpec((tm, k), lambda j, i: (i, 0))]
    args = [x]
    if xm is not None:
        in_specs.append(pl.BlockSpec(xm.shape, lambda j, i: (0, 0)))
        args.append(xm)
    in_specs.append(pl.BlockSpec((tn, k), lambda j, i: (jb + j, 0)))
    args.append(wt)
    if shift:
        assert shift % 8 == 0 and tn % shift == 0 and (col0 + ncols + shift) <= wt.shape[0]
        per = tn // shift
        in_specs.append(pl.BlockSpec((shift, k), lambda j, i: ((jb + j + 1) * per, 0)))
        args.append(wt)
    if bias is not None:
        in_specs.append(pl.BlockSpec((1, tn), lambda j, i: (0, j)))
        args.append(bias)
    out_specs = [pl.BlockSpec((tm, tn), lambda j, i: (i, j))]
    out_shape = [jax.ShapeDtypeStruct((rows, ncols), out_dtype)]
    if xm is not None:
        out_specs.append(pl.BlockSpec((xm.shape[0], tn), lambda j, i: (0, j)))
        out_shape.append(jax.ShapeDtypeStruct((xm.shape[0], ncols), out_dtype))
    res = pl.pallas_call(
        functools.partial(_proj_kernel, has_meta=xm is not None, gate_bias=bias is not None, shift=shift),
        grid=(ncols // tn, rows // tm),
        in_specs=in_specs,
        out_specs=out_specs,
        out_shape=out_shape,
        scratch_shapes=[pltpu.VMEM((k, tn), BF16)],
        compiler_params=_params("arbitrary", "arbitrary"),
        name=name,
    )(*args)
    return res if xm is not None else res[0]


def _cumsum_kernel(zf_ref, zfm_ref, bf_ref, cr_ref, mb_ref, *, n_meta):
    seq = zf_ref.shape[0]
    bf = bf_ref[...]
    r = lax.broadcasted_iota(jnp.int32, (LANES, LANES), 0)
    c = lax.broadcasted_iota(jnp.int32, (LANES, LANES), 1)
    tri = (c <= r).astype(F32)
    sfx = (c > r).astype(F32)

    def prefix(mat, x):
        return jnp.dot(mat, x, preferred_element_type=F32, precision=lax.Precision.HIGHEST)

    n_meta_pad = zfm_ref.shape[0]
    lfm = jax.nn.log_sigmoid(zfm_ref[...] + bf)
    rows = lax.broadcasted_iota(jnp.int32, (n_meta_pad, LANES), 0)
    lfm = jnp.where(rows < n_meta, lfm, 0.0)
    mbias = jnp.where(rows < n_meta, prefix(sfx, lfm), NEG_BIG)
    mbias_t = mbias.T
    for h in range(mb_ref.shape[0]):
        mb_ref[h] = mbias_t[h:h + 1, :]

    def body(blk, carry):
        off = pl.multiple_of(blk * LANES, LANES)
        lf = jax.nn.log_sigmoid(zf_ref[pl.ds(off, LANES), :] + bf)
        cs = prefix(tri, lf) + carry
        cs_t = cs.T
        for h in range(cr_ref.shape[0]):
            cr_ref[h, :, pl.ds(off, LANES)] = cs_t[h:h + 1, :]
        return cs[LANES - 1:LANES, :]

    lax.fori_loop(0, seq // LANES, body, jnp.zeros((1, LANES), F32))


def _cumsum(zf, zfm_pad, bf_pad, batch, n_heads, n_meta):
    t = zf.shape[0]
    seq = t // batch
    return pl.pallas_call(
        functools.partial(_cumsum_kernel, n_meta=n_meta),
        grid=(batch,),
        in_specs=[pl.BlockSpec((seq, LANES), lambda b: (b, 0)),
                  pl.BlockSpec(zfm_pad.shape, lambda b: (0, 0)),
                  pl.BlockSpec((1, LANES), lambda b: (0, 0))],
        out_specs=[pl.BlockSpec((None, n_heads, 1, seq), lambda b: (b, 0, 0, 0)),
                   pl.BlockSpec((n_heads, 1, LANES), lambda b: (0, 0, 0))],
        out_shape=[jax.ShapeDtypeStruct((batch, n_heads, 1, seq), F32),
                   jax.ShapeDtypeStruct((n_heads, 1, LANES), F32)],
        compiler_params=_params("arbitrary"),
        name="forget_cumsum",
    )(zf, zfm_pad, bf_pad)


def _pool_kernel(u_ref, um_ref, wp_ref, ps_ref, o_ref, ext_ref, wbf_ref):
    ts = u_ref.shape[0]
    halo = um_ref.shape[0]
    gw = wp_ref.shape[1]
    i = pl.program_id(1)

    @pl.when((pl.program_id(0) == 0) & (i == 0))
    def _():
        wbf_ref[...] = wp_ref[...].astype(BF16)

    @pl.when(i == 0)
    def _():
        ext_ref[0:halo, :] = um_ref[...].astype(F32)

    @pl.when(i > 0)
    def _():
        ext_ref[0:halo, :] = ext_ref[ts:ts + halo, :]

    ext_ref[halo:halo + ts, :] = u_ref[...].astype(F32)
    for g, w in enumerate(POOL_WINDOWS):
        cols = slice(g * gw, (g + 1) * gw)
        uf = ext_ref[halo:halo + ts, cols]
        acc = uf
        for d in range(1, w):
            acc = acc + ext_ref[halo - d:halo - d + ts, cols]
        pooled = acc * (1.0 / w) - uf
        y = _dot(pooled.astype(BF16), wbf_ref[g]) * ps_ref[:, cols]
        o_ref[:, cols] = y.astype(o_ref.dtype)


def _pool(uqkv, uqkv_meta, w_pool, pool_scale, batch, ts):
    t = uqkv.shape[0]
    seq = t // batch
    ng, gw, _ = w_pool.shape
    pw = ng * gw
    halo = uqkv_meta.shape[0]
    assert halo >= max(POOL_WINDOWS) and seq % ts == 0
    nt = seq // ts
    return pl.pallas_call(
        _pool_kernel,
        grid=(batch, nt),
        in_specs=[pl.BlockSpec((ts, pw), lambda b, i: (b * nt + i, 0)),
                  pl.BlockSpec((halo, pw), lambda b, i: (0, 0)),
                  pl.BlockSpec(w_pool.shape, lambda b, i: (0, 0, 0)),
                  pl.BlockSpec((1, pw), lambda b, i: (0, 0))],
        out_specs=pl.BlockSpec((ts, pw), lambda b, i: (b * nt + i, 0)),
        out_shape=jax.ShapeDtypeStruct((t, pw), BF16),
        scratch_shapes=[pltpu.VMEM((halo + ts, pw), F32), pltpu.VMEM(w_pool.shape, BF16)],
        compiler_params=_params("arbitrary", "arbitrary"),
        name="pool_mixer",
    )(uqkv, uqkv_meta, w_pool, pool_scale)


def _attn_kernel(q_ref, k_ref, v_ref, km_ref, vm_ref, cr_ref, mb_ref, qg_ref, kg_ref, o_ref,
                 kn_ref, vt_ref, cb_ref, kmn_ref, vmt_ref, mbc_ref):
    tq = q_ref.shape[0]
    seq = k_ref.shape[0]
    n_meta = km_ref.shape[0]
    qi = pl.program_id(2)

    @pl.when(qi == 0)
    def _():
        kn_ref[...] = _rms(k_ref[...].astype(F32), kg_ref[...]).astype(BF16)

        def chunk(i, _):
            off = pl.multiple_of(i * LANES, LANES)
            vt_ref[:, pl.ds(off, LANES)] = v_ref[pl.ds(off, LANES), :].astype(F32).T.astype(BF16)
            cb_ref[pl.ds(off, LANES), :] = jnp.broadcast_to(cr_ref[:, pl.ds(off, LANES)] * LOG2E, (LANES, LANES)).T
            return 0

        lax.fori_loop(0, seq // LANES, chunk, 0)
        pad = jnp.zeros((LANES - n_meta, HEAD_DIM), F32)
        kmn = jnp.concatenate([_rms(km_ref[...].astype(F32), kg_ref[...]), pad], axis=0)
        kmn_ref[...] = kmn.astype(BF16)
        vmt_ref[...] = jnp.concatenate([vm_ref[...].astype(F32), pad], axis=0).T.astype(BF16)
        mbc_ref[...] = jnp.broadcast_to(mb_ref[...] * LOG2E, (LANES, LANES)).T

    qn = (_rms(q_ref[...].astype(F32), qg_ref[...]) * (HEAD_DIM ** -0.5 * LOG2E)).astype(BF16)
    q0 = pl.multiple_of(qi * tq, tq)
    cref = cb_ref[pl.ds(q0, 1), :]

    def lanes_tq(x):
        return jnp.concatenate([x] * (tq // LANES), axis=1)

    def update(carry, st, vt_blk):
        m, l, acc = carry
        m_new = jnp.maximum(m, jnp.max(st, axis=0, keepdims=True))
        alpha = jnp.exp2(m - m_new)
        p = jnp.exp2(st - m_new)
        l = alpha * l + jnp.sum(p, axis=0, keepdims=True)
        acc = alpha * acc + _dot(vt_blk, p.astype(BF16))
        return m_new, l, acc

    def scores(off):
        return _dot_nt(kn_ref[pl.ds(off, tq), :], qn) + lanes_tq(cref - cb_ref[pl.ds(off, tq), :])

    def body(j, carry):
        off = pl.multiple_of(j * tq, tq)
        return update(carry, scores(off), vt_ref[:, pl.ds(off, tq)])

    carry = (jnp.full((1, tq), NEG_BIG, F32), jnp.zeros((1, tq), F32), jnp.zeros((HEAD_DIM, tq), F32))
    carry = lax.fori_loop(0, qi, body, carry)

    key = lax.broadcasted_iota(jnp.int32, (tq, tq), 0)
    qry = lax.broadcasted_iota(jnp.int32, (tq, tq), 1)
    st_meta = _dot_nt(kmn_ref[...], qn) + lanes_tq(cref + mbc_ref[...])
    st = jnp.concatenate([st_meta, jnp.where(key <= qry, scores(q0), NEG_BIG)], axis=0)
    vt = jnp.concatenate([vmt_ref[...], vt_ref[:, pl.ds(q0, tq)]], axis=1)
    m, l, acc = update(carry, st, vt)
    o_ref[...] = (acc / l).T.astype(o_ref.dtype)


def _attention(uqkv, uqkv_meta, cr, mbias, q_gain, k_gain, batch, n_heads, col_q, tq):
    t = uqkv.shape[0]
    seq = t // batch
    n_meta = uqkv_meta.shape[0]
    nq = seq // tq
    bq = col_q // HEAD_DIM
    bk = bq + n_heads
    bv = bk + n_heads
    return pl.pallas_call(
        _attn_kernel,
        grid=(batch, n_heads, nq),
        in_specs=[pl.BlockSpec((tq, HEAD_DIM), lambda b, h, i: (b * nq + i, bq + h)),
                  pl.BlockSpec((seq, HEAD_DIM), lambda b, h, i: (b, bk + h)),
                  pl.BlockSpec((seq, HEAD_DIM), lambda b, h, i: (b, bv + h)),
                  pl.BlockSpec((n_meta, HEAD_DIM), lambda b, h, i: (0, bk + h)),
                  pl.BlockSpec((n_meta, HEAD_DIM), lambda b, h, i: (0, bv + h)),
                  pl.BlockSpec((None, None, 1, seq), lambda b, h, i: (b, h, 0, 0)),
                  pl.BlockSpec((None, 1, LANES), lambda b, h, i: (h, 0, 0)),
                  pl.BlockSpec((1, HEAD_DIM), lambda b, h, i: (0, 0)),
                  pl.BlockSpec((1, HEAD_DIM), lambda b, h, i: (0, 0))],
        out_specs=pl.BlockSpec((tq, HEAD_DIM), lambda b, h, i: (b * nq + i, h)),
        out_shape=jax.ShapeDtypeStruct((t, n_heads * HEAD_DIM), BF16),
        scratch_shapes=[pltpu.VMEM((seq, HEAD_DIM), BF16),
                        pltpu.VMEM((HEAD_DIM, seq), BF16),
                        pltpu.VMEM((seq, LANES), F32),
                        pltpu.VMEM((LANES, HEAD_DIM), BF16),
                        pltpu.VMEM((HEAD_DIM, LANES), BF16),
                        pltpu.VMEM((LANES, LANES), F32)],
        compiler_params=_params("arbitrary", "arbitrary", "arbitrary"),
        name="forgetting_attention",
    )(uqkv, uqkv, uqkv, uqkv_meta, uqkv_meta, cr, mbias, q_gain, k_gain)


def _merge_kernel(yp_ref, ya_ref, wp_ref, wa_ref, gp_ref, ga_ref, o_ref, wpb_ref, wab_ref):
    @pl.when(pl.program_id(1) == 0)
    def _():
        wpb_ref[...] = wp_ref[...].astype(BF16)
        wab_ref[...] = wa_ref[...].astype(BF16)

    a = _dot(yp_ref[...], wpb_ref[...])
    b = _dot(ya_ref[...], wab_ref[...])
    o_ref[...] = (gp_ref[...].astype(F32) * a + ga_ref[...].astype(F32) * b).astype(o_ref.dtype)


def _merge(y_pool, y_attn, w_bp, w_ba, gates, tm, tn):
    t, kp = y_pool.shape
    ka = y_attn.shape[1]
    d = w_bp.shape[1]
    nj = d // tn
    return pl.pallas_call(
        _merge_kernel,
        grid=(nj, t // tm),
        in_specs=[pl.BlockSpec((tm, kp), lambda j, i: (i, 0)),
                  pl.BlockSpec((tm, ka), lambda j, i: (i, 0)),
                  pl.BlockSpec((kp, tn), lambda j, i: (0, j)),
                  pl.BlockSpec((ka, tn), lambda j, i: (0, j)),
                  pl.BlockSpec((tm, tn), lambda j, i: (i, j)),
                  pl.BlockSpec((tm, tn), lambda j, i: (i, nj + j))],
        out_specs=pl.BlockSpec((tm, tn), lambda j, i: (i, j)),
        out_shape=jax.ShapeDtypeStruct((t, d), BF16),
        scratch_shapes=[pltpu.VMEM((kp, tn), BF16), pltpu.VMEM((ka, tn), BF16)],
        compiler_params=_params("arbitrary", "arbitrary"),
        name="branch_merge",
    )(y_pool, y_attn, w_bp, w_ba, gates, gates)


def _outproj_kernel(m_ref, w_ref, x_ref, o_ref, wbf_ref):
    @pl.when(pl.program_id(1) == 0)
    def _():
        wbf_ref[...] = w_ref[...].astype(BF16)

    o_ref[...] = x_ref[...] + _dot(m_ref[...], wbf_ref[...])


def _outproj(merged, w_out, x, tm, tn):
    t, k = merged.shape
    d = w_out.shape[1]
    return pl.pallas_call(
        _outproj_kernel,
        grid=(d // tn, t // tm),
        in_specs=[pl.BlockSpec((tm, k), lambda j, i: (i, 0)),
                  pl.BlockSpec((k, tn), lambda j, i: (0, j)),
                  pl.BlockSpec((tm, tn), lambda j, i: (i, j))],
        out_specs=pl.BlockSpec((tm, tn), lambda j, i: (i, j)),
        out_shape=jax.ShapeDtypeStruct((t, d), F32),
        scratch_shapes=[pltpu.VMEM((k, tn), BF16)],
        compiler_params=_params("arbitrary", "arbitrary"),
        name="out_proj",
    )(merged, w_out, x)


def _router_kernel(h_ref, g_ref, wr_ref, br_ref, xp_ref, idx_ref, wt_ref):
    y = _rms(h_ref[...], g_ref[...])
    y_hi = y.astype(BF16)
    y_lo = (y - y_hi.astype(F32)).astype(BF16)
    w = wr_ref[...]
    w_hi = w.astype(BF16)
    w_lo = (w - w_hi.astype(F32)).astype(BF16)
    logits = _dot_nt(w_hi, y_hi) + (_dot_nt(w_hi, y_lo) + _dot_nt(w_lo, y_hi)) + br_ref[...]

    half = y.shape[1] // 2
    bits = pltpu.bitcast(y_hi.astype(F32), jnp.uint32)
    xp_ref[...] = (bits[:, :half] >> 16) | bits[:, half:]

    n_exp = logits.shape[0]
    eid = lax.broadcasted_iota(jnp.int32, logits.shape, 0)
    vals = logits
    top_v, top_i = [], []
    for _ in range(TOP_K):
        mx = jnp.max(vals, axis=0, keepdims=True)
        sel = jnp.min(jnp.where(vals == mx, eid, n_exp), axis=0, keepdims=True)
        top_v.append(mx)
        top_i.append(sel)
        vals = jnp.where(eid == sel, -jnp.inf, vals)
    ex = [jnp.exp(v - top_v[0]) for v in top_v]
    den = ex[0] + ex[1] + ex[2] + ex[3]
    for k in range(TOP_K):
        idx_ref[k:k + 1, :] = top_i[k]
        wt_ref[k:k + 1, :] = ex[k] / den


def _router(h1, gain, w_router_t, b_router, tm):
    t, d = h1.shape
    n_exp = w_router_t.shape[0]
    return pl.pallas_call(
        _router_kernel,
        grid=(t // tm,),
        in_specs=[pl.BlockSpec((tm, d), lambda i: (i, 0)),
                  pl.BlockSpec((1, d), lambda i: (0, 0)),
                  pl.BlockSpec((n_exp, d), lambda i: (0, 0)),
                  pl.BlockSpec((n_exp, 1), lambda i: (0, 0))],
        out_specs=[pl.BlockSpec((tm, d // 2), lambda i: (i, 0)),
                   pl.BlockSpec((TOP_K, tm), lambda i: (0, i)),
                   pl.BlockSpec((TOP_K, tm), lambda i: (0, i))],
        out_shape=[jax.ShapeDtypeStruct((t, d // 2), jnp.uint32),
                   jax.ShapeDtypeStruct((TOP_K, t), jnp.int32),
                   jax.ShapeDtypeStruct((TOP_K, t), F32)],
        compiler_params=_params("arbitrary"),
        name="router_topk",
    )(h1, gain, w_router_t, b_router)


def _plan_kernel(idx_ref, dest_ref, grp_ref, rank_ref, *, n_exp):
    t = idx_ref.shape[1]
    nblk = t // LANES
    eid = lax.broadcasted_iota(jnp.int32, (n_exp, LANES), 0)
    r = lax.broadcasted_iota(jnp.int32, (LANES, LANES), 0)
    c = lax.broadcasted_iota(jnp.int32, (LANES, LANES), 1)
    upper = (r <= c).astype(BF16)

    def member(off):
        m = jnp.zeros((n_exp, LANES), F32)
        for k in range(TOP_K):
            m = m + (idx_ref[k:k + 1, pl.ds(off, LANES)] == eid).astype(F32)
        return m

    def count(blk, carry):
        off = pl.multiple_of(blk * LANES, LANES)
        m = member(off)
        cs = _dot(m.astype(BF16), upper) + carry
        rank_ref[:, pl.ds(off, LANES)] = cs - m
        return carry + jnp.sum(m, axis=1, keepdims=True)

    counts = lax.fori_loop(0, nblk, count, jnp.zeros((n_exp, 1), F32))
    padded = jnp.floor((counts + (EXPERT_BLOCK - 1)) * (1.0 / EXPERT_BLOCK)) * EXPERT_BLOCK
    er = lax.broadcasted_iota(jnp.int32, (n_exp, n_exp), 0)
    ec = lax.broadcasted_iota(jnp.int32, (n_exp, n_exp), 1)
    lower = (ec <= er).astype(BF16)
    pad_end = _dot(lower, jnp.broadcast_to(padded, (n_exp, LANES)).astype(BF16))[:, 0:1]
    pad_start = pad_end - padded

    def place(blk, _):
        off = pl.multiple_of(blk * LANES, LANES)
        pos = pad_start + rank_ref[:, pl.ds(off, LANES)]
        for k in range(TOP_K):
            sel = idx_ref[k:k + 1, pl.ds(off, LANES)] == eid
            d = jnp.sum(jnp.where(sel, pos, 0.0), axis=0, keepdims=True)
            dest_ref[blk, k:k + 1, :] = d.astype(jnp.int32)
        return 0

    lax.fori_loop(0, nblk, place, 0)

    lane = lax.broadcasted_iota(jnp.int32, (n_exp, LANES), 1)
    first = pad_start * (1.0 / EXPERT_BLOCK)
    nblocks = padded * (1.0 / EXPERT_BLOCK)
    grp_ref[...] = jnp.where(lane == 0, first, jnp.where(lane == 1, nblocks, 0.0)).astype(jnp.int32)


def _plan(idx, n_exp):
    t = idx.shape[1]
    assert t <= 256 * EXPERT_BLOCK
    dest, grp = pl.pallas_call(
        functools.partial(_plan_kernel, n_exp=n_exp),
        grid=(1,),
        in_specs=[pl.BlockSpec(idx.shape, lambda i: (0, 0))],
        out_specs=[pl.BlockSpec((t // LANES, TOP_K, LANES), lambda i: (0, 0, 0)),
                   pl.BlockSpec((n_exp, LANES), lambda i: (0, 0))],
        out_shape=[jax.ShapeDtypeStruct((t // LANES, TOP_K, LANES), jnp.int32),
                   jax.ShapeDtypeStruct((n_exp, LANES), jnp.int32)],
        scratch_shapes=[pltpu.VMEM((n_exp, t), F32)],
        compiler_params=_params("arbitrary"),
        name="dispatch_plan",
    )(idx)
    return dest, grp[:, 0], grp[:, 1]


def _dispatch_kernel(gs_ref, gn_ref, dest_ref, xp_ref, xs_hbm, zero_ref, sem, zsem):
    n_exp = gs_ref.shape[0]

    def zero_block(m):
        return pltpu.make_async_copy(zero_ref, xs_hbm.at[pl.ds(m * EXPERT_BLOCK, EXPERT_BLOCK), :], zsem)

    def zero_copy(e):
        return zero_block(gs_ref[e] + gn_ref[e] - 1)

    @pl.when(pl.program_id(0) == 0)
    def _():
        zero_ref[...] = jnp.zeros_like(zero_ref)
        used = gs_ref[n_exp - 1] + gn_ref[n_exp - 1]
        total = xs_hbm.shape[0] // EXPERT_BLOCK

        def start(e, _):
            @pl.when(gn_ref[e] > 0)
            def _():
                zero_copy(e).start()
            return 0

        def wait(e, _):
            @pl.when(gn_ref[e] > 0)
            def _():
                zero_copy(e).wait()
            return 0

        lax.fori_loop(0, n_exp, start, 0)
        lax.fori_loop(used, total, lambda m, _: (zero_block(m).start(), 0)[1], 0)
        lax.fori_loop(0, n_exp, wait, 0)
        lax.fori_loop(used, total, lambda m, _: (zero_block(m).wait(), 0)[1], 0)

    def row_copy(r, k):
        return pltpu.make_async_copy(xp_ref.at[pl.ds(r, 1), :], xs_hbm.at[pl.ds(dest_ref[k, r], 1), :], sem)

    def issue(r, _):
        for k in range(TOP_K):
            row_copy(r, k).start()
        return 0

    def drain(r, _):
        for k in range(TOP_K):
            row_copy(r, k).wait()
        return 0

    lax.fori_loop(0, LANES, issue, 0)
    lax.fori_loop(0, LANES, drain, 0)


def _dispatch(grp_start, grp_blocks, dest, xp, n_rows):
    t, w = xp.shape
    return pl.pallas_call(
        _dispatch_kernel,
        grid_spec=pltpu.PrefetchScalarGridSpec(
            num_scalar_prefetch=2,
            grid=(t // LANES,),
            in_specs=[pl.BlockSpec((None, TOP_K, LANES), lambda i, gs, gn: (i, 0, 0), memory_space=pltpu.SMEM),
                      pl.BlockSpec((LANES, w), lambda i, gs, gn: (i, 0))],
            out_specs=pl.BlockSpec(memory_space=pl.ANY),
            scratch_shapes=[pltpu.VMEM((EXPERT_BLOCK, w), xp.dtype), pltpu.SemaphoreType.DMA,
                            pltpu.SemaphoreType.DMA]),
        out_shape=jax.ShapeDtypeStruct((n_rows, w), xp.dtype),
        compiler_params=_params("arbitrary"),
        name="dispatch_rows",
    )(grp_start, grp_blocks, dest, xp)


def _group_loop(gs_ref, gn_ref, src_hbm, dst_hbm, col0, in_buf, out_buf, sem_in, sem_out, compute):
    e = pl.program_id(1)
    nb = gn_ref[e]
    b0 = gs_ref[e]
    width = out_buf.shape[2]
    cb = in_buf.shape[1] // EXPERT_BLOCK
    nch = lax.div(nb + (cb - 1), jnp.int32(cb))

    def rows(i, nblk):
        return pl.ds(pl.multiple_of((b0 + i * cb) * EXPERT_BLOCK, EXPERT_BLOCK), nblk * EXPERT_BLOCK)

    def fetch(i, slot):
        return pltpu.make_async_copy(src_hbm.at[rows(i, cb), :], in_buf.at[slot], sem_in.at[slot])

    def store(i, slot, nblk):
        return pltpu.make_async_copy(out_buf.at[slot, pl.ds(0, nblk * EXPERT_BLOCK), :],
                                     dst_hbm.at[rows(i, nblk), pl.ds(col0, width)], sem_out.at[slot])

    def blocks_in(i):
        return jnp.minimum(nb - i * cb, cb)

    def for_size(count, fn):
        for size in range(1, cb + 1):
            @pl.when(count == size)
            def _():
                fn(size)

    @pl.when(e == 0)
    def _():
        last = gs_ref.shape[0] - 1
        used = gs_ref[last] + gn_ref[last]
        total = dst_hbm.shape[0] // EXPERT_BLOCK
        out_buf[0, 0:EXPERT_BLOCK, :] = jnp.zeros((EXPERT_BLOCK, width), out_buf.dtype)

        def tail(m):
            blk = pl.ds(pl.multiple_of(m * EXPERT_BLOCK, EXPERT_BLOCK), EXPERT_BLOCK)
            return pltpu.make_async_copy(out_buf.at[0, pl.ds(0, EXPERT_BLOCK), :],
                                         dst_hbm.at[blk, pl.ds(col0, width)], sem_out.at[0])

        lax.fori_loop(used, total, lambda m, _: (tail(m).start(), 0)[1], 0)
        lax.fori_loop(used, total, lambda m, _: (tail(m).wait(), 0)[1], 0)

    @pl.when(nch > 0)
    def _():
        fetch(0, 0).start()

    def step(i, _):
        slot = i & 1
        fetch(i, slot).wait()

        @pl.when(i + 1 < nch)
        def _():
            fetch(i + 1, 1 - slot).start()

        @pl.when(i >= 2)
        def _():
            store(i - 2, slot, cb).wait()

        def run(size):
            n = size * EXPERT_BLOCK
            out_buf[slot, 0:n, :] = compute(in_buf[slot, 0:n, :])
            store(i, slot, size).start()

        for_size(blocks_in(i), run)
        return 0

    lax.fori_loop(0, nch, step, 0)

    @pl.when(nch >= 2)
    def _():
        store(nch - 2, nch & 1, cb).wait()

    @pl.when(nch >= 1)
    def _():
        for_size(blocks_in(nch - 1), lambda size: store(nch - 1, (nch - 1) & 1, size).wait())


def _gateup_kernel(gs_ref, gn_ref, xs_hbm, wg_hbm, wu_hbm, bg_ref, bu_ref, h_hbm,
                   wg_st, wu_st, wgb_ref, wub_ref, xbuf, hbuf, wsem, sem_in, sem_out):
    tn = wgb_ref.shape[1]
    half = xbuf.shape[2]
    n, e = pl.program_id(0), pl.program_id(1)
    n_tiles, n_exp = pl.num_programs(0), pl.num_programs(1)

    def weight_copies(nn, ee):
        cols = pl.ds(pl.multiple_of(nn * tn, tn), tn)
        return (pltpu.make_async_copy(wg_hbm.at[ee, :, cols], wg_st, wsem.at[0]),
                pltpu.make_async_copy(wu_hbm.at[ee, :, cols], wu_st, wsem.at[1]))

    @pl.when((n == 0) & (e == 0))
    def _():
        for c in weight_copies(0, 0):
            c.start()

    for c in weight_copies(n, e):
        c.wait()

    @pl.when(gn_ref[e] > 0)
    def _():
        wgb_ref[...] = wg_st[...].astype(BF16)
        wub_ref[...] = wu_st[...].astype(BF16)

    wrap = e == n_exp - 1

    @pl.when(jnp.logical_not(wrap & (n == n_tiles - 1)))
    def _():
        for c in weight_copies(jnp.where(wrap, n + 1, n), jnp.where(wrap, 0, e + 1)):
            c.start()

    def compute(xp):
        xa = pltpu.bitcast(xp << 16, F32).astype(BF16)
        xb = pltpu.bitcast(xp & jnp.uint32(0xFFFF0000), F32).astype(BF16)
        gte = _dot(xa, wgb_ref[0:half, :]) + _dot(xb, wgb_ref[half:, :]) + bg_ref[...]
        up = _dot(xa, wub_ref[0:half, :]) + _dot(xb, wub_ref[half:, :]) + bu_ref[...]
        gte = jnp.minimum(gte, SWIGLU_LIMIT)
        up = jnp.clip(up, -SWIGLU_LIMIT, SWIGLU_LIMIT)
        return (gte * jax.nn.sigmoid(SWIGLU_ALPHA * gte) * (up + 1.0)).astype(BF16)

    col0 = pl.multiple_of(pl.program_id(0) * tn, tn)
    _group_loop(gs_ref, gn_ref, xs_hbm, h_hbm, col0, xbuf, hbuf, sem_in, sem_out, compute)


def _gateup(grp_start, grp_blocks, xs, w_gate, w_up, b_gate, b_up, tn):
    n_rows, half = xs.shape
    n_exp, d, f = w_gate.shape
    chunk = CHUNK_BLOCKS * EXPERT_BLOCK
    any_spec = pl.BlockSpec(memory_space=pl.ANY)
    b_spec = pl.BlockSpec((None, 1, tn), lambda n, e, gs, gn: (e, 0, n))
    return pl.pallas_call(
        _gateup_kernel,
        grid_spec=pltpu.PrefetchScalarGridSpec(
            num_scalar_prefetch=2,
            grid=(f // tn, n_exp),
            in_specs=[any_spec, any_spec, any_spec, b_spec, b_spec],
            out_specs=any_spec,
            scratch_shapes=[pltpu.VMEM((d, tn), F32), pltpu.VMEM((d, tn), F32),
                            pltpu.VMEM((d, tn), BF16), pltpu.VMEM((d, tn), BF16),
                            pltpu.VMEM((2, chunk, half), xs.dtype), pltpu.VMEM((2, chunk, tn), BF16),
                            pltpu.SemaphoreType.DMA((2,)), pltpu.SemaphoreType.DMA((2,)),
                            pltpu.SemaphoreType.DMA((2,))]),
        out_shape=jax.ShapeDtypeStruct((n_rows, f), BF16),
        compiler_params=_params("arbitrary", "arbitrary"),
        name="expert_gate_up",
    )(grp_start, grp_blocks, xs, w_gate, w_up, b_gate, b_up)


def _down_kernel(gs_ref, gn_ref, h_hbm, wd_ref, bd_ref, y_hbm, wdb_ref, hbuf, ybuf, sem_in, sem_out):
    tn = wdb_ref.shape[1]

    @pl.when(gn_ref[pl.program_id(1)] > 0)
    def _():
        wdb_ref[...] = wd_ref[...].astype(BF16)

    def compute(h):
        return _dot(h, wdb_ref[...]) + bd_ref[...]

    col0 = pl.multiple_of(pl.program_id(0) * tn, tn)
    _group_loop(gs_ref, gn_ref, h_hbm, y_hbm, col0, hbuf, ybuf, sem_in, sem_out, compute)


def _down(grp_start, grp_blocks, h, w_down, b_down, tn):
    n_rows, f = h.shape
    n_exp, _, d = w_down.shape
    return pl.pallas_call(
        _down_kernel,
        grid_spec=pltpu.PrefetchScalarGridSpec(
            num_scalar_prefetch=2,
            grid=(d // tn, n_exp),
            in_specs=[pl.BlockSpec(memory_space=pl.ANY),
                      pl.BlockSpec((None, f, tn), lambda n, e, gs, gn: (e, 0, n)),
                      pl.BlockSpec((None, 1, tn), lambda n, e, gs, gn: (e, 0, n))],
            out_specs=pl.BlockSpec(memory_space=pl.ANY),
            scratch_shapes=[pltpu.VMEM((f, tn), BF16),
                            pltpu.VMEM((2, CHUNK_BLOCKS * EXPERT_BLOCK, f), h.dtype),
                            pltpu.VMEM((2, CHUNK_BLOCKS * EXPERT_BLOCK, tn), F32),
                            pltpu.SemaphoreType.DMA((2,)), pltpu.SemaphoreType.DMA((2,))]),
        out_shape=jax.ShapeDtypeStruct((n_rows, d), F32),
        compiler_params=_params("arbitrary", "arbitrary"),
        name="expert_down",
    )(grp_start, grp_blocks, h, w_down, b_down)


def _combine_kernel(dest_ref, wt_ref, h_ref, y_hbm, o_ref, buf_ref, sem):
    def issue(r, _):
        for k in range(TOP_K):
            pltpu.make_async_copy(y_hbm.at[pl.ds(dest_ref[k, r], 1), :], buf_ref.at[k, pl.ds(r, 1), :], sem).start()
        return 0

    lax.fori_loop(0, LANES, issue, 0)

    def drain(r, _):
        for k in range(TOP_K):
            pltpu.make_async_copy(y_hbm.at[pl.ds(0, 1), :], buf_ref.at[k, pl.ds(r, 1), :], sem).wait()
        return 0

    lax.fori_loop(0, LANES, drain, 0)

    wt = wt_ref[...]
    eye = (lax.broadcasted_iota(jnp.int32, (LANES, LANES), 0)
           == lax.broadcasted_iota(jnp.int32, (LANES, LANES), 1))
    acc = h_ref[...]
    for k in range(TOP_K):
        wcol = jnp.sum(jnp.where(eye, wt[k:k + 1, :], 0.0), axis=1, keepdims=True)
        acc = acc + wcol * buf_ref[k]
    o_ref[...] = acc


def _combine(dest, wts, h1, y):
    t, d = h1.shape
    return pl.pallas_call(
        _combine_kernel,
        grid=(t // LANES,),
        in_specs=[pl.BlockSpec((None, TOP_K, LANES), lambda i: (i, 0, 0), memory_space=pltpu.SMEM),
                  pl.BlockSpec((TOP_K, LANES), lambda i: (0, i)),
                  pl.BlockSpec((LANES, d), lambda i: (i, 0)),
                  pl.BlockSpec(memory_space=pl.ANY)],
        out_specs=pl.BlockSpec((LANES, d), lambda i: (i, 0)),
        out_shape=jax.ShapeDtypeStruct((t, d), F32),
        scratch_shapes=[pltpu.VMEM((TOP_K, LANES, d), F32), pltpu.SemaphoreType.DMA],
        compiler_params=_params("arbitrary"),
        name="combine_rows",
    )(dest, wts, h1, y)


def _layer(x, meta, norm_mix, w_in, b_forget, b_branch_gate, q_norm, k_norm, w_pool_group, pool_scale,
           w_branch_pool, w_branch_attn, w_out, norm_ffn, w_router, b_router, w_gate, b_gate, w_up, b_up,
           w_down, b_down):
    batch, seq, d = x.shape
    t = batch * seq
    n_meta = meta.shape[0]
    pool_w = w_pool_group.shape[0] * w_pool_group.shape[1]
    attn_w = w_branch_attn.shape[0]
    n_heads = attn_w // HEAD_DIM
    n_exp = w_router.shape[1]
    col_q = pool_w
    col_zf = pool_w + 3 * attn_w
    col_zg = col_zf + n_heads
    assert n_heads <= LANES and col_zf % LANES == 0

    tm = min(1024, t)
    x2 = x.reshape(t, d)

    hn = _rmsnorm(x2, norm_mix.reshape(1, d), min(256, t))
    hn_meta = _rmsnorm(meta, norm_mix.reshape(1, d), n_meta)

    w_in_t = w_in.T
    uqkv, uqkv_meta = _proj(hn, hn_meta, w_in_t, None, col0=0, ncols=col_zf, tn=512, tm=tm,
                            out_dtype=BF16, name="in_proj_uqkv")
    zf, zf_meta = _proj(hn, hn_meta, w_in_t, None, col0=col_zf, ncols=LANES, tn=LANES, tm=tm,
                        out_dtype=F32, name="in_proj_forget")
    gates = _proj(hn, None, w_in_t, b_branch_gate.reshape(1, 2 * d), col0=col_zf, ncols=2 * d, tn=512, tm=tm,
                  out_dtype=BF16, name="in_proj_gates", shift=col_zg - col_zf)

    bf_pad = jnp.pad(b_forget.reshape(1, n_heads), ((0, 0), (0, LANES - n_heads)))
    zfm_pad = jnp.pad(zf_meta, ((0, LANES - n_meta), (0, 0)))
    cr, mbias = _cumsum(zf, zfm_pad, bf_pad, batch, n_heads, n_meta)

    y_pool = _pool(uqkv, uqkv_meta, w_pool_group, pool_scale.reshape(1, pool_w), batch, min(512, seq))
    y_attn = _attention(uqkv, uqkv_meta, cr, mbias, q_norm.reshape(1, HEAD_DIM), k_norm.reshape(1, HEAD_DIM),
                        batch, n_heads, col_q, min(512, seq))

    merged = _merge(y_pool, y_attn, w_branch_pool, w_branch_attn, gates, tm, 512)
    h1 = _outproj(merged, w_out, x2, tm, 512)

    xp, idx, wts = _router(h1, norm_ffn.reshape(1, d), w_router.T, b_router.reshape(n_exp, 1), min(256, t))
    n_blocks = -(-(t * TOP_K + n_exp * (EXPERT_BLOCK - 1)) // EXPERT_BLOCK) + CHUNK_BLOCKS - 1
    dest, grp_start, grp_blocks = _plan(idx, n_exp)
    xs = _dispatch(grp_start, grp_blocks, dest, xp, n_blocks * EXPERT_BLOCK)
    f = w_gate.shape[2]
    hmid = _gateup(grp_start, grp_blocks, xs, w_gate, w_up, b_gate.reshape(n_exp, 1, f), b_up.reshape(n_exp, 1, f), 512)
    y = _down(grp_start, grp_blocks, hmid, w_down, b_down.reshape(n_exp, 1, d), 2048)
    out = _combine(dest, wts, h1, y)
    return out.reshape(batch, seq, d)


def kernel(x, meta_tokens, norm_mix, w_in, b_forget, b_branch_gate, q_norm, k_norm, w_pool_group, pool_scale,
           w_branch_pool, w_branch_attn, w_out, norm_ffn, w_router, b_router, w_gate, b_gate, w_up, b_up,
           w_down, b_down):
    depth = norm_mix.shape[0]
    assert depth == 1, "the fused layer pipeline drops the meta rows after the (single) layer"
    return _layer(x, meta_tokens, norm_mix[0], w_in[0], b_forget[0], b_branch_gate[0], q_norm[0], k_norm[0],
                  w_pool_group[0], pool_scale[0], w_branch_pool[0], w_branch_attn[0], w_out[0], norm_ffn[0],
                  w_router[0], b_router[0], w_gate[0], b_gate[0], w_up[0], b_up[0], w_down[0], b_down[0])
```

```python
import functools

import jax
import jax.numpy as jnp
from jax import lax
from jax.experimental import pallas as pl
from jax.experimental.pallas import tpu as pltpu

NORM_EPS = 1e-6
HEAD_DIM = 128
POOL_WINDOWS = (2, 4, 8, 16)
TOP_K = 4
SWIGLU_LIMIT = 7.0
SWIGLU_ALPHA = 1.702
EXPERT_BLOCK = 128
CHUNK_BLOCKS = 4
LANES = 128
NEG_BIG = -1e30
LOG2E = 1.4426950408889634
VMEM_LIMIT = 56 * 1024 * 1024

F32 = jnp.float32
BF16 = jnp.bfloat16


def _params(*sem, vmem=VMEM_LIMIT):
    return pltpu.CompilerParams(dimension_semantics=sem, vmem_limit_bytes=vmem)


def _dot(a, b):
    return jnp.dot(a, b, preferred_element_type=F32)


def _dot_nt(a, b):
    return lax.dot_general(a, b, (((1,), (1,)), ((), ())), preferred_element_type=F32)


def _rms(x, gain):
    ms = jnp.mean(x * x, axis=-1, keepdims=True)
    return x * lax.rsqrt(ms + NORM_EPS) * gain


def _rmsnorm_kernel(x_ref, g_ref, o_ref):
    o_ref[...] = _rms(x_ref[...], g_ref[...]).astype(o_ref.dtype)


def _rmsnorm(x, gain, tm):
    rows, d = x.shape
    return pl.pallas_call(
        _rmsnorm_kernel,
        grid=(rows // tm,),
        in_specs=[pl.BlockSpec((tm, d), lambda i: (i, 0)), pl.BlockSpec((1, d), lambda i: (0, 0))],
        out_specs=pl.BlockSpec((tm, d), lambda i: (i, 0)),
        out_shape=jax.ShapeDtypeStruct((rows, d), BF16),
        compiler_params=_params("arbitrary"),
        name="rmsnorm",
    )(x, gain)


def _proj_kernel(*refs, has_meta, gate_bias, shift):
    it = iter(refs)
    x_ref = next(it)
    xm_ref = next(it) if has_meta else None
    w_ref = next(it)
    wn_ref = next(it) if shift else None
    b_ref = next(it) if gate_bias else None
    o_ref = next(it)
    om_ref = next(it) if has_meta else None
    wbf_ref = next(it)
    tn, k = w_ref.shape

    @pl.when(pl.program_id(1) == 0)
    def _():
        def chunk(c, _):
            off = pl.multiple_of(c * LANES, LANES)
            if shift:
                wt = jnp.concatenate([w_ref[shift:tn, pl.ds(off, LANES)], wn_ref[:, pl.ds(off, LANES)]], axis=0)
            else:
                wt = w_ref[:, pl.ds(off, LANES)]
            wbf_ref[pl.ds(off, LANES), :] = wt.T.astype(BF16)
            return 0

        lax.fori_loop(0, k // LANES, chunk, 0)
        if has_meta:
            om_ref[...] = _dot(xm_ref[...], wbf_ref[...]).astype(om_ref.dtype)

    acc = _dot(x_ref[...], wbf_ref[...])
    if gate_bias:
        acc = jax.nn.sigmoid(acc + b_ref[...])
    o_ref[...] = acc.astype(o_ref.dtype)


def _proj(x, xm, wt, bias, *, col0, ncols, tn, tm, out_dtype, name, shift=0):
    rows, k = x.shape
    assert col0 % tn == 0 and ncols % tn == 0 and rows % tm == 0
    jb = col0 // tn
    in_specs = [pl.BlockSpec((tm, k), lambda j, i: (i, 0))]
    args = [x]
    if xm is not None:
        in_specs.append(pl.BlockSpec(xm.shape, lambda j, i: (0, 0)))
        args.append(xm)
    in_specs.append(pl.BlockSpec((tn, k), lambda j, i: (jb + j, 0)))
    args.append(wt)
    if shift:
        assert shift % 8 == 0 and tn % shift == 0 and (col0 + ncols + shift) <= wt.shape[0]
        per = tn // shift
        in_specs.append(pl.BlockSpec((shift, k), lambda j, i: ((jb + j + 1) * per, 0)))
        args.append(wt)
    if bias is not None:
        in_specs.append(pl.BlockSpec((1, tn), lambda j, i: (0, j)))
        args.append(bias)
    out_specs = [pl.BlockSpec((tm, tn), lambda j, i: (i, j))]
    out_shape = [jax.ShapeDtypeStruct((rows, ncols), out_dtype)]
    if xm is not None:
        out_specs.append(pl.BlockSpec((xm.shape[0], tn), lambda j, i: (0, j)))
        out_shape.append(jax.ShapeDtypeStruct((xm.shape[0], ncols), out_dtype))
    res = pl.pallas_call(
        functools.partial(_proj_kernel, has_meta=xm is not None, gate_bias=bias is not None, shift=shift),
        grid=(ncols // tn, rows // tm),
        in_specs=in_specs,
        out_specs=out_specs,
        out_shape=out_shape,
        scratch_shapes=[pltpu.VMEM((k, tn), BF16)],
        compiler_params=_params("arbitrary", "arbitrary"),
        name=name,
    )(*args)
    return res if xm is not None else res[0]


def _cumsum_kernel(zf_ref, zfm_ref, bf_ref, cr_ref, mb_ref, *, n_meta):
    seq = zf_ref.shape[0]
    bf = bf_ref[...]
    r = lax.broadcasted_iota(jnp.int32, (LANES, LANES), 0)
    c = lax.broadcasted_iota(jnp.int32, (LANES, LANES), 1)
    tri = (c <= r).astype(F32)
    sfx = (c > r).astype(F32)

    def prefix(mat, x):
        return jnp.dot(mat, x, preferred_element_type=F32, precision=lax.Precision.HIGHEST)

    n_meta_pad = zfm_ref.shape[0]
    lfm = jax.nn.log_sigmoid(zfm_ref[...] + bf)
    rows = lax.broadcasted_iota(jnp.int32, (n_meta_pad, LANES), 0)
    lfm = jnp.where(rows < n_meta, lfm, 0.0)
    mbias = jnp.where(rows < n_meta, prefix(sfx, lfm), NEG_BIG)
    mbias_t = mbias.T
    for h in range(mb_ref.shape[0]):
        mb_ref[h] = mbias_t[h:h + 1, :]

    def body(blk, carry):
        off = pl.multiple_of(blk * LANES, LANES)
        lf = jax.nn.log_sigmoid(zf_ref[pl.ds(off, LANES), :] + bf)
        cs = prefix(tri, lf) + carry
        cs_t = cs.T
        for h in range(cr_ref.shape[0]):
            cr_ref[h, :, pl.ds(off, LANES)] = cs_t[h:h + 1, :]
        return cs[LANES - 1:LANES, :]

    lax.fori_loop(0, seq // LANES, body, jnp.zeros((1, LANES), F32))


def _cumsum(zf, zfm_pad, bf_pad, batch, n_heads, n_meta):
    t = zf.shape[0]
    seq = t // batch
    return pl.pallas_call(
        functools.partial(_cumsum_kernel, n_meta=n_meta),
        grid=(batch,),
        in_specs=[pl.BlockSpec((seq, LANES), lambda b: (b, 0)),
                  pl.BlockSpec(zfm_pad.shape, lambda b: (0, 0)),
                  pl.BlockSpec((1, LANES), lambda b: (0, 0))],
        out_specs=[pl.BlockSpec((None, n_heads, 1, seq), lambda b: (b, 0, 0, 0)),
                   pl.BlockSpec((n_heads, 1, LANES), lambda b: (0, 0, 0))],
        out_shape=[jax.ShapeDtypeStruct((batch, n_heads, 1, seq), F32),
                   jax.ShapeDtypeStruct((n_heads, 1, LANES), F32)],
        compiler_params=_params("arbitrary"),
        name="forget_cumsum",
    )(zf, zfm_pad, bf_pad)


def _pool_kernel(u_ref, um_ref, wp_ref, ps_ref, o_ref, ext_ref, wbf_ref):
    ts = u_ref.shape[0]
    halo = um_ref.shape[0]
    gw = wp_ref.shape[1]
    i = pl.program_id(1)

    @pl.when((pl.program_id(0) == 0) & (i == 0))
    def _():
        wbf_ref[...] = wp_ref[...].astype(BF16)

    @pl.when(i == 0)
    def _():
        ext_ref[0:halo, :] = um_ref[...].astype(F32)

    @pl.when(i > 0)
    def _():
        ext_ref[0:halo, :] = ext_ref[ts:ts + halo, :]

    ext_ref[halo:halo + ts, :] = u_ref[...].astype(F32)
    for g, w in enumerate(POOL_WINDOWS):
        cols = slice(g * gw, (g + 1) * gw)
        uf = ext_ref[halo:halo + ts, cols]
        acc = uf
        for d in range(1, w):
            acc = acc + ext_ref[halo - d:halo - d + ts, cols]
        pooled = acc * (1.0 / w) - uf
        y = _dot(pooled.astype(BF16), wbf_ref[g]) * ps_ref[:, cols]
        o_ref[:, cols] = y.astype(o_ref.dtype)


def _pool(uqkv, uqkv_meta, w_pool, pool_scale, batch, ts):
    t = uqkv.shape[0]
    seq = t // batch
    ng, gw, _ = w_pool.shape
    pw = ng * gw
    halo = uqkv_meta.shape[0]
    assert halo >= max(POOL_WINDOWS) and seq % ts == 0
    nt = seq // ts
    return pl.pallas_call(
        _pool_kernel,
        grid=(batch, nt),
        in_specs=[pl.BlockSpec((ts, pw), lambda b, i: (b * nt + i, 0)),
                  pl.BlockSpec((halo, pw), lambda b, i: (0, 0)),
                  pl.BlockSpec(w_pool.shape, lambda b, i: (0, 0, 0)),
                  pl.BlockSpec((1, pw), lambda b, i: (0, 0))],
        out_specs=pl.BlockSpec((ts, pw), lambda b, i: (b * nt + i, 0)),
        out_shape=jax.ShapeDtypeStruct((t, pw), BF16),
        scratch_shapes=[pltpu.VMEM((halo + ts, pw), F32), pltpu.VMEM(w_pool.shape, BF16)],
        compiler_params=_params("arbitrary", "arbitrary"),
        name="pool_mixer",
    )(uqkv, uqkv_meta, w_pool, pool_scale)


def _attn_kernel(q_ref, k_ref, v_ref, km_ref, vm_ref, cr_ref, mb_ref, qg_ref, kg_ref, o_ref,
                 kn_ref, vt_ref, cb_ref, kmn_ref, vmt_ref, mbc_ref):
    tq = q_ref.shape[0]
    seq = k_ref.shape[0]
    n_meta = km_ref.shape[0]
    qi = pl.program_id(2)

    @pl.when(qi == 0)
    def _():
        kn_ref[...] = _rms(k_ref[...].astype(F32), kg_ref[...]).astype(BF16)

        def chunk(i, _):
            off = pl.multiple_of(i * LANES, LANES)
            vt_ref[:, pl.ds(off, LANES)] = v_ref[pl.ds(off, LANES), :].astype(F32).T.astype(BF16)
            cb_ref[pl.ds(off, LANES), :] = jnp.broadcast_to(cr_ref[:, pl.ds(off, LANES)] * LOG2E, (LANES, LANES)).T
            return 0

        lax.fori_loop(0, seq // LANES, chunk, 0)
        pad = jnp.zeros((LANES - n_meta, HEAD_DIM), F32)
        kmn = jnp.concatenate([_rms(km_ref[...].astype(F32), kg_ref[...]), pad], axis=0)
        kmn_ref[...] = kmn.astype(BF16)
        vmt_ref[...] = jnp.concatenate([vm_ref[...].astype(F32), pad], axis=0).T.astype(BF16)
        mbc_ref[...] = jnp.broadcast_to(mb_ref[...] * LOG2E, (LANES, LANES)).T

    qn = (_rms(q_ref[...].astype(F32), qg_ref[...]) * (HEAD_DIM ** -0.5 * LOG2E)).astype(BF16)
    q0 = pl.multiple_of(qi * tq, tq)
    cref = cb_ref[pl.ds(q0, 1), :]

    def lanes_tq(x):
        return jnp.concatenate([x] * (tq // LANES), axis=1)

    def update(carry, st, vt_blk):
        m, l, acc = carry
        m_new = jnp.maximum(m, jnp.max(st, axis=0, keepdims=True))
        alpha = jnp.exp2(m - m_new)
        p = jnp.exp2(st - m_new)
        l = alpha * l + jnp.sum(p, axis=0, keepdims=True)
        acc = alpha * acc + _dot(vt_blk, p.astype(BF16))
        return m_new, l, acc

    def scores(off, size):
        return _dot_nt(kn_ref[pl.ds(off, size), :], qn) + lanes_tq(cref - cb_ref[pl.ds(off, size), :])

    def body(j, carry):
        off = pl.multiple_of(j * (2 * tq), 2 * tq)
        return update(carry, scores(off, 2 * tq), vt_ref[:, pl.ds(off, 2 * tq)])

    carry = (jnp.full((1, tq), NEG_BIG, F32), jnp.zeros((1, tq), F32), jnp.zeros((HEAD_DIM, tq), F32))
    carry = lax.fori_loop(0, lax.shift_right_logical(qi, 1), body, carry)

    def finish(with_prev):
        key = lax.broadcasted_iota(jnp.int32, (tq, tq), 0)
        qry = lax.broadcasted_iota(jnp.int32, (tq, tq), 1)
        sts = [_dot_nt(kmn_ref[...], qn) + lanes_tq(cref + mbc_ref[...])]
        vts = [vmt_ref[...]]
        if with_prev:
            prev = pl.multiple_of(q0 - tq, tq)
            sts.append(scores(prev, tq))
            vts.append(vt_ref[:, pl.ds(prev, tq)])
        sts.append(jnp.where(key <= qry, scores(q0, tq), NEG_BIG))
        vts.append(vt_ref[:, pl.ds(q0, tq)])
        m, l, acc = update(carry, jnp.concatenate(sts, axis=0), jnp.concatenate(vts, axis=1))
        o_ref[...] = (acc / l).T.astype(o_ref.dtype)

    odd = (qi & 1) == 1

    @pl.when(odd)
    def _():
        finish(True)

    @pl.when(jnp.logical_not(odd))
    def _():
        finish(False)


def _attention(uqkv, uqkv_meta, cr, mbias, q_gain, k_gain, batch, n_heads, col_q, tq):
    t = uqkv.shape[0]
    seq = t // batch
    n_meta = uqkv_meta.shape[0]
    nq = seq // tq
    bq = col_q // HEAD_DIM
    bk = bq + n_heads
    bv = bk + n_heads
    return pl.pallas_call(
        _attn_kernel,
        grid=(batch, n_heads, nq),
        in_specs=[pl.BlockSpec((tq, HEAD_DIM), lambda b, h, i: (b * nq + i, bq + h)),
                  pl.BlockSpec((seq, HEAD_DIM), lambda b, h, i: (b, bk + h)),
                  pl.BlockSpec((seq, HEAD_DIM), lambda b, h, i: (b, bv + h)),
                  pl.BlockSpec((n_meta, HEAD_DIM), lambda b, h, i: (0, bk + h)),
                  pl.BlockSpec((n_meta, HEAD_DIM), lambda b, h, i: (0, bv + h)),
                  pl.BlockSpec((None, None, 1, seq), lambda b, h, i: (b, h, 0, 0)),
                  pl.BlockSpec((None, 1, LANES), lambda b, h, i: (h, 0, 0)),
                  pl.BlockSpec((1, HEAD_DIM), lambda b, h, i: (0, 0)),
                  pl.BlockSpec((1, HEAD_DIM), lambda b, h, i: (0, 0))],
        out_specs=pl.BlockSpec((tq, HEAD_DIM), lambda b, h, i: (b * nq + i, h)),
        out_shape=jax.ShapeDtypeStruct((t, n_heads * HEAD_DIM), BF16),
        scratch_shapes=[pltpu.VMEM((seq, HEAD_DIM), BF16),
                        pltpu.VMEM((HEAD_DIM, seq), BF16),
                        pltpu.VMEM((seq, LANES), F32),
                        pltpu.VMEM((LANES, HEAD_DIM), BF16),
                        pltpu.VMEM((HEAD_DIM, LANES), BF16),
                        pltpu.VMEM((LANES, LANES), F32)],
        compiler_params=_params("arbitrary", "arbitrary", "arbitrary"),
        name="forgetting_attention",
    )(uqkv, uqkv, uqkv, uqkv_meta, uqkv_meta, cr, mbias, q_gain, k_gain)


def _merge_kernel(yp_ref, ya_ref, wp_ref, wa_ref, gp_ref, ga_ref, o_ref, wpb_ref, wab_ref):
    @pl.when(pl.program_id(1) == 0)
    def _():
        wpb_ref[...] = wp_ref[...].astype(BF16)
        wab_ref[...] = wa_ref[...].astype(BF16)

    a = _dot(yp_ref[...], wpb_ref[...])
    b = _dot(ya_ref[...], wab_ref[...])
    o_ref[...] = (gp_ref[...].astype(F32) * a + ga_ref[...].astype(F32) * b).astype(o_ref.dtype)


def _merge(y_pool, y_attn, w_bp, w_ba, gates, tm, tn):
    t, kp = y_pool.shape
    ka = y_attn.shape[1]
    d = w_bp.shape[1]
    nj = d // tn
    return pl.pallas_call(
        _merge_kernel,
        grid=(nj, t // tm),
        in_specs=[pl.BlockSpec((tm, kp), lambda j, i: (i, 0)),
                  pl.BlockSpec((tm, ka), lambda j, i: (i, 0)),
                  pl.BlockSpec((kp, tn), lambda j, i: (0, j)),
                  pl.BlockSpec((ka, tn), lambda j, i: (0, j)),
                  pl.BlockSpec((tm, tn), lambda j, i: (i, j)),
                  pl.BlockSpec((tm, tn), lambda j, i: (i, nj + j))],
        out_specs=pl.BlockSpec((tm, tn), lambda j, i: (i, j)),
        out_shape=jax.ShapeDtypeStruct((t, d), BF16),
        scratch_shapes=[pltpu.VMEM((kp, tn), BF16), pltpu.VMEM((ka, tn), BF16)],
        compiler_params=_params("arbitrary", "arbitrary"),
        name="branch_merge",
    )(y_pool, y_attn, w_bp, w_ba, gates, gates)


def _outproj_kernel(m_ref, w_ref, x_ref, o_ref, wbf_ref):
    @pl.when(pl.program_id(1) == 0)
    def _():
        wbf_ref[...] = w_ref[...].astype(BF16)

    o_ref[...] = x_ref[...] + _dot(m_ref[...], wbf_ref[...])


def _outproj(merged, w_out, x, tm, tn):
    t, k = merged.shape
    d = w_out.shape[1]
    return pl.pallas_call(
        _outproj_kernel,
        grid=(d // tn, t // tm),
        in_specs=[pl.BlockSpec((tm, k), lambda j, i: (i, 0)),
                  pl.BlockSpec((k, tn), lambda j, i: (0, j)),
                  pl.BlockSpec((tm, tn), lambda j, i: (i, j))],
        out_specs=pl.BlockSpec((tm, tn), lambda j, i: (i, j)),
        out_shape=jax.ShapeDtypeStruct((t, d), F32),
        scratch_shapes=[pltpu.VMEM((k, tn), BF16)],
        compiler_params=_params("arbitrary", "arbitrary"),
        name="out_proj",
    )(merged, w_out, x)


def _router_kernel(h_ref, g_ref, wr_ref, br_ref, xp_ref, idx_ref, wt_ref):
    y = _rms(h_ref[...], g_ref[...])
    y_hi = y.astype(BF16)
    y_lo = (y - y_hi.astype(F32)).astype(BF16)
    w = wr_ref[...]
    w_hi = w.astype(BF16)
    w_lo = (w - w_hi.astype(F32)).astype(BF16)
    logits = _dot_nt(w_hi, y_hi) + (_dot_nt(w_hi, y_lo) + _dot_nt(w_lo, y_hi)) + br_ref[...]

    half = y.shape[1] // 2
    bits = pltpu.bitcast(y_hi.astype(F32), jnp.uint32)
    xp_ref[...] = (bits[:, :half] >> 16) | bits[:, half:]

    n_exp = logits.shape[0]
    eid = lax.broadcasted_iota(jnp.int32, logits.shape, 0)
    vals = logits
    top_v, top_i = [], []
    for _ in range(TOP_K):
        mx = jnp.max(vals, axis=0, keepdims=True)
        sel = jnp.min(jnp.where(vals == mx, eid, n_exp), axis=0, keepdims=True)
        top_v.append(mx)
        top_i.append(sel)
        vals = jnp.where(eid == sel, -jnp.inf, vals)
    ex = [jnp.exp(v - top_v[0]) for v in top_v]
    den = ex[0] + ex[1] + ex[2] + ex[3]
    for k in range(TOP_K):
        idx_ref[k:k + 1, :] = top_i[k]
        wt_ref[k:k + 1, :] = ex[k] / den


def _router(h1, gain, w_router_t, b_router, tm):
    t, d = h1.shape
    n_exp = w_router_t.shape[0]
    return pl.pallas_call(
        _router_kernel,
        grid=(t // tm,),
        in_specs=[pl.BlockSpec((tm, d), lambda i: (i, 0)),
                  pl.BlockSpec((1, d), lambda i: (0, 0)),
                  pl.BlockSpec((n_exp, d), lambda i: (0, 0)),
                  pl.BlockSpec((n_exp, 1), lambda i: (0, 0))],
        out_specs=[pl.BlockSpec((tm, d // 2), lambda i: (i, 0)),
                   pl.BlockSpec((TOP_K, tm), lambda i: (0, i)),
                   pl.BlockSpec((TOP_K, tm), lambda i: (0, i))],
        out_shape=[jax.ShapeDtypeStruct((t, d // 2), jnp.uint32),
                   jax.ShapeDtypeStruct((TOP_K, t), jnp.int32),
                   jax.ShapeDtypeStruct((TOP_K, t), F32)],
        compiler_params=_params("arbitrary"),
        name="router_topk",
    )(h1, gain, w_router_t, b_router)


def _plan_kernel(idx_ref, dest_ref, grp_ref, rank_ref, *, n_exp):
    t = idx_ref.shape[1]
    nblk = t // LANES
    eid = lax.broadcasted_iota(jnp.int32, (n_exp, LANES), 0)
    r = lax.broadcasted_iota(jnp.int32, (LANES, LANES), 0)
    c = lax.broadcasted_iota(jnp.int32, (LANES, LANES), 1)
    upper = (r <= c).astype(BF16)

    def member(off):
        m = jnp.zeros((n_exp, LANES), F32)
        for k in range(TOP_K):
            m = m + (idx_ref[k:k + 1, pl.ds(off, LANES)] == eid).astype(F32)
        return m

    def count(blk, carry):
        off = pl.multiple_of(blk * LANES, LANES)
        m = member(off)
        cs = _dot(m.astype(BF16), upper) + carry
        rank_ref[:, pl.ds(off, LANES)] = cs - m
        return carry + jnp.sum(m, axis=1, keepdims=True)

    counts = lax.fori_loop(0, nblk, count, jnp.zeros((n_exp, 1), F32))
    padded = jnp.floor((counts + (EXPERT_BLOCK - 1)) * (1.0 / EXPERT_BLOCK)) * EXPERT_BLOCK
    er = lax.broadcasted_iota(jnp.int32, (n_exp, n_exp), 0)
    ec = lax.broadcasted_iota(jnp.int32, (n_exp, n_exp), 1)
    lower = (ec <= er).astype(BF16)
    pad_end = _dot(lower, jnp.broadcast_to(padded, (n_exp, LANES)).astype(BF16))[:, 0:1]
    pad_start = pad_end - padded

    def place(blk, _):
        off = pl.multiple_of(blk * LANES, LANES)
        pos = pad_start + rank_ref[:, pl.ds(off, LANES)]
        for k in range(TOP_K):
            sel = idx_ref[k:k + 1, pl.ds(off, LANES)] == eid
            d = jnp.sum(jnp.where(sel, pos, 0.0), axis=0, keepdims=True)
            dest_ref[blk, k:k + 1, :] = d.astype(jnp.int32)
        return 0

    lax.fori_loop(0, nblk, place, 0)

    lane = lax.broadcasted_iota(jnp.int32, (n_exp, LANES), 1)
    first = pad_start * (1.0 / EXPERT_BLOCK)
    nblocks = padded * (1.0 / EXPERT_BLOCK)
    grp_ref[...] = jnp.where(lane == 0, first, jnp.where(lane == 1, nblocks, 0.0)).astype(jnp.int32)


def _plan(idx, n_exp):
    t = idx.shape[1]
    assert t <= 256 * EXPERT_BLOCK
    dest, grp = pl.pallas_call(
        functools.partial(_plan_kernel, n_exp=n_exp),
        grid=(1,),
        in_specs=[pl.BlockSpec(idx.shape, lambda i: (0, 0))],
        out_specs=[pl.BlockSpec((t // LANES, TOP_K, LANES), lambda i: (0, 0, 0)),
                   pl.BlockSpec((n_exp, LANES), lambda i: (0, 0))],
        out_shape=[jax.ShapeDtypeStruct((t // LANES, TOP_K, LANES), jnp.int32),
                   jax.ShapeDtypeStruct((n_exp, LANES), jnp.int32)],
        scratch_shapes=[pltpu.VMEM((n_exp, t), F32)],
        compiler_params=_params("arbitrary"),
        name="dispatch_plan",
    )(idx)
    return dest, grp[:, 0], grp[:, 1]


def _dispatch_kernel(gs_ref, gn_ref, dest_ref, xp_ref, xs_hbm, zero_ref, sem, zsem):
    n_exp = gs_ref.shape[0]

    def zero_block(m):
        return pltpu.make_async_copy(zero_ref, xs_hbm.at[pl.ds(m * EXPERT_BLOCK, EXPERT_BLOCK), :], zsem)

    def zero_copy(e):
        return zero_block(gs_ref[e] + gn_ref[e] - 1)

    @pl.when(pl.program_id(0) == 0)
    def _():
        zero_ref[...] = jnp.zeros_like(zero_ref)
        used = gs_ref[n_exp - 1] + gn_ref[n_exp - 1]
        total = xs_hbm.shape[0] // EXPERT_BLOCK

        def start(e, _):
            @pl.when(gn_ref[e] > 0)
            def _():
                zero_copy(e).start()
            return 0

        def wait(e, _):
            @pl.when(gn_ref[e] > 0)
            def _():
                zero_copy(e).wait()
            return 0

        lax.fori_loop(0, n_exp, start, 0)
        lax.fori_loop(used, total, lambda m, _: (zero_block(m).start(), 0)[1], 0)
        lax.fori_loop(0, n_exp, wait, 0)
        lax.fori_loop(used, total, lambda m, _: (zero_block(m).wait(), 0)[1], 0)

    def row_copy(r, k):
        return pltpu.make_async_copy(xp_ref.at[pl.ds(r, 1), :], xs_hbm.at[pl.ds(dest_ref[k, r], 1), :], sem)

    def issue(r, _):
        for k in range(TOP_K):
            row_copy(r, k).start(priority=k % 2)
        return 0

    def drain(r, _):
        for k in range(TOP_K):
            row_copy(r, k).wait()
        return 0

    lax.fori_loop(0, LANES, issue, 0)
    lax.fori_loop(0, LANES, drain, 0)


def _dispatch(grp_start, grp_blocks, dest, xp, n_rows):
    t, w = xp.shape
    return pl.pallas_call(
        _dispatch_kernel,
        grid_spec=pltpu.PrefetchScalarGridSpec(
            num_scalar_prefetch=2,
            grid=(t // LANES,),
            in_specs=[pl.BlockSpec((None, TOP_K, LANES), lambda i, gs, gn: (i, 0, 0), memory_space=pltpu.SMEM),
                      pl.BlockSpec((LANES, w), lambda i, gs, gn: (i, 0))],
            out_specs=pl.BlockSpec(memory_space=pl.ANY),
            scratch_shapes=[pltpu.VMEM((EXPERT_BLOCK, w), xp.dtype), pltpu.SemaphoreType.DMA,
                            pltpu.SemaphoreType.DMA]),
        out_shape=jax.ShapeDtypeStruct((n_rows, w), xp.dtype),
        compiler_params=_params("arbitrary"),
        name="dispatch_rows",
    )(grp_start, grp_blocks, dest, xp)


def _weight_stage(w_hbms, stages, wsem, tn):
    n, e = pl.program_id(0), pl.program_id(1)
    n_tiles, n_exp = pl.num_programs(0), pl.num_programs(1)

    def copies(nn, ee):
        cols = pl.ds(pl.multiple_of(nn * tn, tn), tn)
        return [pltpu.make_async_copy(w.at[ee, :, cols], st, wsem.at[k])
                for k, (w, st) in enumerate(zip(w_hbms, stages))]

    @pl.when((n == 0) & (e == 0))
    def _():
        for c in copies(0, 0):
            c.start(priority=1)

    def wait_current():
        for c in copies(n, e):
            c.wait()

    def start_next():
        wrap = e == n_exp - 1

        @pl.when(jnp.logical_not(wrap & (n == n_tiles - 1)))
        def _():
            for c in copies(jnp.where(wrap, n + 1, n), jnp.where(wrap, 0, e + 1)):
                c.start(priority=1)

    return wait_current, start_next


def _group_loop(gs_ref, gn_ref, src_hbm, dst_hbm, col0, in_buf, out_buf, sem_in, sem_out, compute, after_first_fetch):
    e = pl.program_id(1)
    nb = gn_ref[e]
    b0 = gs_ref[e]
    width = out_buf.shape[2]
    cb = in_buf.shape[1] // EXPERT_BLOCK
    nch = lax.div(nb + (cb - 1), jnp.int32(cb))

    def rows(i, nblk):
        return pl.ds(pl.multiple_of((b0 + i * cb) * EXPERT_BLOCK, EXPERT_BLOCK), nblk * EXPERT_BLOCK)

    def fetch(i, slot):
        return pltpu.make_async_copy(src_hbm.at[rows(i, cb), :], in_buf.at[slot], sem_in.at[slot])

    def store(i, slot, nblk):
        return pltpu.make_async_copy(out_buf.at[slot, pl.ds(0, nblk * EXPERT_BLOCK), :],
                                     dst_hbm.at[rows(i, nblk), pl.ds(col0, width)], sem_out.at[slot])

    def blocks_in(i):
        return jnp.minimum(nb - i * cb, cb)

    def for_size(count, fn):
        for size in range(1, cb + 1):
            @pl.when(count == size)
            def _():
                fn(size)

    @pl.when(e == 0)
    def _():
        last = gs_ref.shape[0] - 1
        used = gs_ref[last] + gn_ref[last]
        total = dst_hbm.shape[0] // EXPERT_BLOCK
        out_buf[0, 0:EXPERT_BLOCK, :] = jnp.zeros((EXPERT_BLOCK, width), out_buf.dtype)

        def tail(m):
            blk = pl.ds(pl.multiple_of(m * EXPERT_BLOCK, EXPERT_BLOCK), EXPERT_BLOCK)
            return pltpu.make_async_copy(out_buf.at[0, pl.ds(0, EXPERT_BLOCK), :],
                                         dst_hbm.at[blk, pl.ds(col0, width)], sem_out.at[0])

        lax.fori_loop(used, total, lambda m, _: (tail(m).start(), 0)[1], 0)
        lax.fori_loop(used, total, lambda m, _: (tail(m).wait(), 0)[1], 0)

    @pl.when(nch > 0)
    def _():
        fetch(0, 0).start()

    after_first_fetch()

    def step(i, _):
        slot = i & 1
        fetch(i, slot).wait()

        @pl.when(i + 1 < nch)
        def _():
            fetch(i + 1, 1 - slot).start()

        @pl.when(i >= 2)
        def _():
            store(i - 2, slot, cb).wait()

        def run(size):
            n = size * EXPERT_BLOCK
            out_buf[slot, 0:n, :] = compute(in_buf[slot, 0:n, :])
            store(i, slot, size).start()

        for_size(blocks_in(i), run)
        return 0

    lax.fori_loop(0, nch, step, 0)

    @pl.when(nch >= 2)
    def _():
        store(nch - 2, nch & 1, cb).wait()

    @pl.when(nch >= 1)
    def _():
        for_size(blocks_in(nch - 1), lambda size: store(nch - 1, (nch - 1) & 1, size).wait())


def _gateup_kernel(gs_ref, gn_ref, xs_hbm, wg_hbm, wu_hbm, bg_ref, bu_ref, h_hbm,
                   wg_st, wu_st, wgb_ref, wub_ref, xbuf, hbuf, wsem, sem_in, sem_out):
    tn = wgb_ref.shape[1]
    half = xbuf.shape[2]
    wait_weights, start_next_weights = _weight_stage((wg_hbm, wu_hbm), (wg_st, wu_st), wsem, tn)
    wait_weights()

    @pl.when(gn_ref[pl.program_id(1)] > 0)
    def _():
        wgb_ref[...] = wg_st[...].astype(BF16)
        wub_ref[...] = wu_st[...].astype(BF16)

    def compute(xp):
        xa = pltpu.bitcast(xp << 16, F32).astype(BF16)
        xb = pltpu.bitcast(xp & jnp.uint32(0xFFFF0000), F32).astype(BF16)
        gte = _dot(xa, wgb_ref[0:half, :]) + _dot(xb, wgb_ref[half:, :]) + bg_ref[...]
        up = _dot(xa, wub_ref[0:half, :]) + _dot(xb, wub_ref[half:, :]) + bu_ref[...]
        gte = jnp.minimum(gte, SWIGLU_LIMIT)
        up = jnp.clip(up, -SWIGLU_LIMIT, SWIGLU_LIMIT)
        return (gte * jax.nn.sigmoid(SWIGLU_ALPHA * gte) * (up + 1.0)).astype(BF16)

    col0 = pl.multiple_of(pl.program_id(0) * tn, tn)
    _group_loop(gs_ref, gn_ref, xs_hbm, h_hbm, col0, xbuf, hbuf, sem_in, sem_out, compute, start_next_weights)


def _gateup(grp_start, grp_blocks, xs, w_gate, w_up, b_gate, b_up, tn):
    n_rows, half = xs.shape
    n_exp, d, f = w_gate.shape
    chunk = CHUNK_BLOCKS * EXPERT_BLOCK
    any_spec = pl.BlockSpec(memory_space=pl.ANY)
    b_spec = pl.BlockSpec((None, 1, tn), lambda n, e, gs, gn: (e, 0, n))
    return pl.pallas_call(
        _gateup_kernel,
        grid_spec=pltpu.PrefetchScalarGridSpec(
            num_scalar_prefetch=2,
            grid=(f // tn, n_exp),
            in_specs=[any_spec, any_spec, any_spec, b_spec, b_spec],
            out_specs=any_spec,
            scratch_shapes=[pltpu.VMEM((d, tn), F32), pltpu.VMEM((d, tn), F32),
                            pltpu.VMEM((d, tn), BF16), pltpu.VMEM((d, tn), BF16),
                            pltpu.VMEM((2, chunk, half), xs.dtype), pltpu.VMEM((2, chunk, tn), BF16),
                            pltpu.SemaphoreType.DMA((2,)), pltpu.SemaphoreType.DMA((2,)),
                            pltpu.SemaphoreType.DMA((2,))]),
        out_shape=jax.ShapeDtypeStruct((n_rows, f), BF16),
        compiler_params=_params("arbitrary", "arbitrary"),
        name="expert_gate_up",
    )(grp_start, grp_blocks, xs, w_gate, w_up, b_gate, b_up)


def _down_kernel(gs_ref, gn_ref, h_hbm, wd_hbm, bd_ref, y_hbm, wd_st, wdb_ref, hbuf, ybuf, wsem, sem_in, sem_out):
    tn = wdb_ref.shape[1]
    wait_weights, start_next_weights = _weight_stage((wd_hbm,), (wd_st,), wsem, tn)
    wait_weights()

    @pl.when(gn_ref[pl.program_id(1)] > 0)
    def _():
        wdb_ref[...] = wd_st[...].astype(BF16)

    def compute(h):
        return _dot(h, wdb_ref[...]) + bd_ref[...]

    col0 = pl.multiple_of(pl.program_id(0) * tn, tn)
    _group_loop(gs_ref, gn_ref, h_hbm, y_hbm, col0, hbuf, ybuf, sem_in, sem_out, compute, start_next_weights)


def _down(grp_start, grp_blocks, h, w_down, b_down, tn):
    n_rows, f = h.shape
    n_exp, _, d = w_down.shape
    return pl.pallas_call(
        _down_kernel,
        grid_spec=pltpu.PrefetchScalarGridSpec(
            num_scalar_prefetch=2,
            grid=(d // tn, n_exp),
            in_specs=[pl.BlockSpec(memory_space=pl.ANY),
                      pl.BlockSpec(memory_space=pl.ANY),
                      pl.BlockSpec((None, 1, tn), lambda n, e, gs, gn: (e, 0, n))],
            out_specs=pl.BlockSpec(memory_space=pl.ANY),
            scratch_shapes=[pltpu.VMEM((f, tn), F32), pltpu.VMEM((f, tn), BF16),
                            pltpu.VMEM((2, CHUNK_BLOCKS * EXPERT_BLOCK, f), h.dtype),
                            pltpu.VMEM((2, CHUNK_BLOCKS * EXPERT_BLOCK, tn), F32),
                            pltpu.SemaphoreType.DMA((1,)), pltpu.SemaphoreType.DMA((2,)),
                            pltpu.SemaphoreType.DMA((2,))]),
        out_shape=jax.ShapeDtypeStruct((n_rows, d), F32),
        compiler_params=_params("arbitrary", "arbitrary"),
        name="expert_down",
    )(grp_start, grp_blocks, h, w_down, b_down)


def _combine_kernel(dest_ref, wt_ref, h_ref, y_hbm, o_ref, buf_ref, sem):
    def row_copy(r, k):
        return pltpu.make_async_copy(y_hbm.at[pl.ds(dest_ref[k, r], 1), :], buf_ref.at[k, pl.ds(r, 1), :], sem)

    def issue(r, _):
        for k in range(TOP_K):
            row_copy(r, k).start(priority=k % 2)
        return 0

    def drain(r, _):
        for k in range(TOP_K):
            row_copy(r, k).wait()
        return 0

    lax.fori_loop(0, LANES, issue, 0)
    lax.fori_loop(0, LANES, drain, 0)

    wt = wt_ref[...]
    eye = (lax.broadcasted_iota(jnp.int32, (LANES, LANES), 0)
           == lax.broadcasted_iota(jnp.int32, (LANES, LANES), 1))
    acc = h_ref[...]
    for k in range(TOP_K):
        wcol = jnp.sum(jnp.where(eye, wt[k:k + 1, :], 0.0), axis=1, keepdims=True)
        acc = acc + wcol * buf_ref[k]
    o_ref[...] = acc


def _combine(dest, wts, h1, y):
    t, d = h1.shape
    return pl.pallas_call(
        _combine_kernel,
        grid=(t // LANES,),
        in_specs=[pl.BlockSpec((None, TOP_K, LANES), lambda i: (i, 0, 0), memory_space=pltpu.SMEM),
                  pl.BlockSpec((TOP_K, LANES), lambda i: (0, i)),
                  pl.BlockSpec((LANES, d), lambda i: (i, 0)),
                  pl.BlockSpec(memory_space=pl.ANY)],
        out_specs=pl.BlockSpec((LANES, d), lambda i: (i, 0)),
        out_shape=jax.ShapeDtypeStruct((t, d), F32),
        scratch_shapes=[pltpu.VMEM((TOP_K, LANES, d), F32), pltpu.SemaphoreType.DMA],
        compiler_params=_params("arbitrary"),
        name="combine_rows",
    )(dest, wts, h1, y)


def _layer(x, meta, norm_mix, w_in, b_forget, b_branch_gate, q_norm, k_norm, w_pool_group, pool_scale,
           w_branch_pool, w_branch_attn, w_out, norm_ffn, w_router, b_router, w_gate, b_gate, w_up, b_up,
           w_down, b_down):
    batch, seq, d = x.shape
    t = batch * seq
    n_meta = meta.shape[0]
    pool_w = w_pool_group.shape[0] * w_pool_group.shape[1]
    attn_w = w_branch_attn.shape[0]
    n_heads = attn_w // HEAD_DIM
    n_exp = w_router.shape[1]
    col_q = pool_w
    col_zf = pool_w + 3 * attn_w
    col_zg = col_zf + n_heads
    assert n_heads <= LANES and col_zf % LANES == 0

    tm = min(1024, t)
    x2 = x.reshape(t, d)

    hn = _rmsnorm(x2, norm_mix.reshape(1, d), min(256, t))
    hn_meta = _rmsnorm(meta, norm_mix.reshape(1, d), n_meta)

    w_in_t = w_in.T
    uqkv, uqkv_meta = _proj(hn, hn_meta, w_in_t, None, col0=0, ncols=col_zf, tn=512, tm=tm,
                            out_dtype=BF16, name="in_proj_uqkv")
    zf, zf_meta = _proj(hn, hn_meta, w_in_t, None, col0=col_zf, ncols=LANES, tn=LANES, tm=tm,
                        out_dtype=F32, name="in_proj_forget")
    gates = _proj(hn, None, w_in_t, b_branch_gate.reshape(1, 2 * d), col0=col_zf, ncols=2 * d, tn=512, tm=tm,
                  out_dtype=BF16, name="in_proj_gates", shift=col_zg - col_zf)

    bf_pad = jnp.pad(b_forget.reshape(1, n_heads), ((0, 0), (0, LANES - n_heads)))
    zfm_pad = jnp.pad(zf_meta, ((0, LANES - n_meta), (0, 0)))
    cr, mbias = _cumsum(zf, zfm_pad, bf_pad, batch, n_heads, n_meta)

    y_pool = _pool(uqkv, uqkv_meta, w_pool_group, pool_scale.reshape(1, pool_w), batch, min(512, seq))
    y_attn = _attention(uqkv, uqkv_meta, cr, mbias, q_norm.reshape(1, HEAD_DIM), k_norm.reshape(1, HEAD_DIM),
                        batch, n_heads, col_q, min(512, seq))

    merged = _merge(y_pool, y_attn, w_branch_pool, w_branch_attn, gates, tm, 512)
    h1 = _outproj(merged, w_out, x2, tm, 512)

    xp, idx, wts = _router(h1, norm_ffn.reshape(1, d), w_router.T, b_router.reshape(n_exp, 1), min(256, t))
    n_blocks = -(-(t * TOP_K + n_exp * (EXPERT_BLOCK - 1)) // EXPERT_BLOCK) + CHUNK_BLOCKS - 1
    dest, grp_start, grp_blocks = _plan(idx, n_exp)
    xs = _dispatch(grp_start, grp_blocks, dest, xp, n_blocks * EXPERT_BLOCK)
    f = w_gate.shape[2]
    hmid = _gateup(grp_start, grp_blocks, xs, w_gate, w_up, b_gate.reshape(n_exp, 1, f), b_up.reshape(n_exp, 1, f), 512)
    y = _down(grp_start, grp_blocks, hmid, w_down, b_down.reshape(n_exp, 1, d), 2048)
    out = _combine(dest, wts, h1, y)
    return out.reshape(batch, seq, d)


def kernel(x, meta_tokens, norm_mix, w_in, b_forget, b_branch_gate, q_norm, k_norm, w_pool_group, pool_scale,
           w_branch_pool, w_branch_attn, w_out, norm_ffn, w_router, b_router, w_gate, b_gate, w_up, b_up,
           w_down, b_down):
    depth = norm_mix.shape[0]
    assert depth == 1, "the fused layer pipeline drops the meta rows after the (single) layer"
    return _layer(x, meta_tokens, norm_mix[0], w_in[0], b_forget[0], b_branch_gate[0], q_norm[0], k_norm[0],
                  w_pool_group[0], pool_scale[0], w_branch_pool[0], w_branch_attn[0], w_out[0], norm_ffn[0],
                  w_router[0], b_router[0], w_gate[0], b_gate[0], w_up[0], b_up[0], w_down[0], b_down[0])
```

```python
import functools

import jax
import jax.numpy as jnp
from jax import lax
from jax.experimental import pallas as pl
from jax.experimental.pallas import tpu as pltpu

NORM_EPS = 1e-6
HEAD_DIM = 128
POOL_WINDOWS = (2, 4, 8, 16)
TOP_K = 4
SWIGLU_LIMIT = 7.0
SWIGLU_ALPHA = 1.702
EXPERT_BLOCK = 128
CHUNK_BLOCKS = 4
LANES = 128
NEG_BIG = -1e30
LOG2E = 1.4426950408889634
VMEM_LIMIT = 56 * 1024 * 1024

F32 = jnp.float32
BF16 = jnp.bfloat16


def _params(*sem, vmem=VMEM_LIMIT):
    return pltpu.CompilerParams(dimension_semantics=sem, vmem_limit_bytes=vmem)


def _dot(a, b):
    return jnp.dot(a, b, preferred_element_type=F32)


def _dot_nt(a, b):
    return lax.dot_general(a, b, (((1,), (1,)), ((), ())), preferred_element_type=F32)


def _rms(x, gain):
    ms = jnp.mean(x * x, axis=-1, keepdims=True)
    return x * lax.rsqrt(ms + NORM_EPS) * gain


def _rmsnorm_kernel(x_ref, g_ref, o_ref):
    o_ref[...] = _rms(x_ref[...], g_ref[...]).astype(o_ref.dtype)


def _rmsnorm(x, gain, tm):
    rows, d = x.shape
    return pl.pallas_call(
        _rmsnorm_kernel,
        grid=(rows // tm,),
        in_specs=[pl.BlockSpec((tm, d), lambda i: (i, 0)), pl.BlockSpec((1, d), lambda i: (0, 0))],
        out_specs=pl.BlockSpec((tm, d), lambda i: (i, 0)),
        out_shape=jax.ShapeDtypeStruct((rows, d), BF16),
        compiler_params=_params("arbitrary"),
        name="rmsnorm",
    )(x, gain)


def _proj_kernel(*refs, has_meta, gate_bias, shift):
    it = iter(refs)
    x_ref = next(it)
    xm_ref = next(it) if has_meta else None
    w_ref = next(it)
    wn_ref = next(it) if shift else None
    b_ref = next(it) if gate_bias else None
    o_ref = next(it)
    om_ref = next(it) if has_meta else None
    wbf_ref = next(it)
    tn, k = w_ref.shape

    @pl.when(pl.program_id(1) == 0)
    def _():
        def chunk(c, _):
            off = pl.multiple_of(c * LANES, LANES)
            if shift:
                wt = jnp.concatenate([w_ref[shift:tn, pl.ds(off, LANES)], wn_ref[:, pl.ds(off, LANES)]], axis=0)
            else:
                wt = w_ref[:, pl.ds(off, LANES)]
            wbf_ref[pl.ds(off, LANES), :] = wt.T.astype(BF16)
            return 0

        lax.fori_loop(0, k // LANES, chunk, 0)
        if has_meta:
            om_ref[...] = _dot(xm_ref[...], wbf_ref[...]).astype(om_ref.dtype)

    acc = _dot(x_ref[...], wbf_ref[...])
    if gate_bias:
        acc = jax.nn.sigmoid(acc + b_ref[...])
    o_ref[...] = acc.astype(o_ref.dtype)


def _proj(x, xm, wt, bias, *, col0, ncols, tn, tm, out_dtype, name, shift=0):
    rows, k = x.shape
    assert col0 % tn == 0 and ncols % tn == 0 and rows % tm == 0
    jb = col0 // tn
    in_specs = [pl.BlockSpec((tm, k), lambda j, i: (i, 0))]
    args = [x]
    if xm is not None:
        in_specs.append(pl.BlockSpec(xm.shape, lambda j, i: (0, 0)))
        args.append(xm)
    in_specs.append(pl.BlockSpec((tn, k), lambda j, i: (jb + j, 0)))
    args.append(wt)
    if shift:
        assert shift % 8 == 0 and tn % shift == 0 and (col0 + ncols + shift) <= wt.shape[0]
        per = tn // shift
        in_specs.append(pl.BlockSpec((shift, k), lambda j, i: ((jb + j + 1) * per, 0)))
        args.append(wt)
    if bias is not None:
        in_specs.append(pl.BlockSpec((1, tn), lambda j, i: (0, j)))
        args.append(bias)
    out_specs = [pl.BlockSpec((tm, tn), lambda j, i: (i, j))]
    out_shape = [jax.ShapeDtypeStruct((rows, ncols), out_dtype)]
    if xm is not None:
        out_specs.append(pl.BlockSpec((xm.shape[0], tn), lambda j, i: (0, j)))
        out_shape.append(jax.ShapeDtypeStruct((xm.shape[0], ncols), out_dtype))
    res = pl.pallas_call(
        functools.partial(_proj_kernel, has_meta=xm is not None, gate_bias=bias is not None, shift=shift),
        grid=(ncols // tn, rows // tm),
        in_specs=in_specs,
        out_specs=out_specs,
        out_shape=out_shape,
        scratch_shapes=[pltpu.VMEM((k, tn), BF16)],
        compiler_params=_params("arbitrary", "arbitrary"),
        name=name,
    )(*args)
    return res if xm is not None else res[0]


def _cumsum_kernel(zf_ref, zfm_ref, bf_ref, cr_ref, mb_ref, *, n_meta):
    seq = zf_ref.shape[0]
    bf = bf_ref[...]
    r = lax.broadcasted_iota(jnp.int32, (LANES, LANES), 0)
    c = lax.broadcasted_iota(jnp.int32, (LANES, LANES), 1)
    tri = (c <= r).astype(F32)
    sfx = (c > r).astype(F32)

    def prefix(mat, x):
        return jnp.dot(mat, x, preferred_element_type=F32, precision=lax.Precision.HIGHEST)

    n_meta_pad = zfm_ref.shape[0]
    lfm = jax.nn.log_sigmoid(zfm_ref[...] + bf)
    rows = lax.broadcasted_iota(jnp.int32, (n_meta_pad, LANES), 0)
    lfm = jnp.where(rows < n_meta, lfm, 0.0)
    mbias = jnp.where(rows < n_meta, prefix(sfx, lfm), NEG_BIG)
    mbias_t = mbias.T
    for h in range(mb_ref.shape[0]):
        mb_ref[h] = mbias_t[h:h + 1, :]

    def body(blk, carry):
        off = pl.multiple_of(blk * LANES, LANES)
        lf = jax.nn.log_sigmoid(zf_ref[pl.ds(off, LANES), :] + bf)
        cs = prefix(tri, lf) + carry
        cs_t = cs.T
        for h in range(cr_ref.shape[0]):
            cr_ref[h, :, pl.ds(off, LANES)] = cs_t[h:h + 1, :]
        return cs[LANES - 1:LANES, :]

    lax.fori_loop(0, seq // LANES, body, jnp.zeros((1, LANES), F32))


def _cumsum(zf, zfm_pad, bf_pad, batch, n_heads, n_meta):
    t = zf.shape[0]
    seq = t // batch
    return pl.pallas_call(
        functools.partial(_cumsum_kernel, n_meta=n_meta),
        grid=(batch,),
        in_specs=[pl.BlockSpec((seq, LANES), lambda b: (b, 0)),
                  pl.BlockSpec(zfm_pad.shape, lambda b: (0, 0)),
                  pl.BlockSpec((1, LANES), lambda b: (0, 0))],
        out_specs=[pl.BlockSpec((None, n_heads, 1, seq), lambda b: (b, 0, 0, 0)),
                   pl.BlockSpec((n_heads, 1, LANES), lambda b: (0, 0, 0))],
        out_shape=[jax.ShapeDtypeStruct((batch, n_heads, 1, seq), F32),
                   jax.ShapeDtypeStruct((n_heads, 1, LANES), F32)],
        compiler_params=_params("arbitrary"),
        name="forget_cumsum",
    )(zf, zfm_pad, bf_pad)


def _pool_kernel(u_ref, um_ref, wp_ref, ps_ref, o_ref, ext_ref, wbf_ref):
    ts = u_ref.shape[0]
    halo = um_ref.shape[0]
    gw = wp_ref.shape[1]
    i = pl.program_id(1)

    @pl.when((pl.program_id(0) == 0) & (i == 0))
    def _():
        wbf_ref[...] = wp_ref[...].astype(BF16)

    @pl.when(i == 0)
    def _():
        ext_ref[0:halo, :] = um_ref[...].astype(F32)

    @pl.when(i > 0)
    def _():
        ext_ref[0:halo, :] = ext_ref[ts:ts + halo, :]

    ext_ref[halo:halo + ts, :] = u_ref[...].astype(F32)
    for g, w in enumerate(POOL_WINDOWS):
        cols = slice(g * gw, (g + 1) * gw)
        uf = ext_ref[halo:halo + ts, cols]
        acc = uf
        for d in range(1, w):
            acc = acc + ext_ref[halo - d:halo - d + ts, cols]
        pooled = acc * (1.0 / w) - uf
        y = _dot(pooled.astype(BF16), wbf_ref[g]) * ps_ref[:, cols]
        o_ref[:, cols] = y.astype(o_ref.dtype)


def _pool(uqkv, uqkv_meta, w_pool, pool_scale, batch, ts):
    t = uqkv.shape[0]
    seq = t // batch
    ng, gw, _ = w_pool.shape
    pw = ng * gw
    halo = uqkv_meta.shape[0]
    assert halo >= max(POOL_WINDOWS) and seq % ts == 0
    nt = seq // ts
    return pl.pallas_call(
        _pool_kernel,
        grid=(batch, nt),
        in_specs=[pl.BlockSpec((ts, pw), lambda b, i: (b * nt + i, 0)),
                  pl.BlockSpec((halo, pw), lambda b, i: (0, 0)),
                  pl.BlockSpec(w_pool.shape, lambda b, i: (0, 0, 0)),
                  pl.BlockSpec((1, pw), lambda b, i: (0, 0))],
        out_specs=pl.BlockSpec((ts, pw), lambda b, i: (b * nt + i, 0)),
        out_shape=jax.ShapeDtypeStruct((t, pw), BF16),
        scratch_shapes=[pltpu.VMEM((halo + ts, pw), F32), pltpu.VMEM(w_pool.shape, BF16)],
        compiler_params=_params("arbitrary", "arbitrary"),
        name="pool_mixer",
    )(uqkv, uqkv_meta, w_pool, pool_scale)


def _attn_kernel(q_ref, k_ref, v_ref, km_ref, vm_ref, cr_ref, mb_ref, qg_ref, kg_ref, o_ref,
                 kn_ref, vt_ref, cb_ref, kmn_ref, vmt_ref, mbc_ref):
    tq = q_ref.shape[0]
    seq = k_ref.shape[0]
    n_meta = km_ref.shape[0]
    qi = pl.program_id(2)

    @pl.when(qi == 0)
    def _():
        kn_ref[...] = _rms(k_ref[...].astype(F32), kg_ref[...]).astype(BF16)

        def chunk(i, _):
            off = pl.multiple_of(i * LANES, LANES)
            vt_ref[:, pl.ds(off, LANES)] = v_ref[pl.ds(off, LANES), :].astype(F32).T.astype(BF16)
            cb_ref[pl.ds(off, LANES), :] = jnp.broadcast_to(cr_ref[:, pl.ds(off, LANES)] * LOG2E, (LANES, LANES)).T
            return 0

        lax.fori_loop(0, seq // LANES, chunk, 0)
        pad = jnp.zeros((LANES - n_meta, HEAD_DIM), F32)
        kmn = jnp.concatenate([_rms(km_ref[...].astype(F32), kg_ref[...]), pad], axis=0)
        kmn_ref[...] = kmn.astype(BF16)
        vmt_ref[...] = jnp.concatenate([vm_ref[...].astype(F32), pad], axis=0).T.astype(BF16)
        mbc_ref[...] = jnp.broadcast_to(mb_ref[...] * LOG2E, (LANES, LANES)).T

    qn = (_rms(q_ref[...].astype(F32), qg_ref[...]) * (HEAD_DIM ** -0.5 * LOG2E)).astype(BF16)
    q0 = pl.multiple_of(qi * tq, tq)
    cref = cb_ref[pl.ds(q0, 1), :]

    def lanes_tq(x):
        return jnp.concatenate([x] * (tq // LANES), axis=1)

    def update(carry, st, vt_blk):
        m, l, acc = carry
        m_new = jnp.maximum(m, jnp.max(st, axis=0, keepdims=True))
        alpha = jnp.exp2(m - m_new)
        p = jnp.exp2(st - m_new)
        l = alpha * l + jnp.sum(p, axis=0, keepdims=True)
        acc = alpha * acc + _dot(vt_blk, p.astype(BF16))
        return m_new, l, acc

    def scores(off, size):
        return _dot_nt(kn_ref[pl.ds(off, size), :], qn) + lanes_tq(cref - cb_ref[pl.ds(off, size), :])

    def body(j, carry):
        off = pl.multiple_of(j * (2 * tq), 2 * tq)
        return update(carry, scores(off, 2 * tq), vt_ref[:, pl.ds(off, 2 * tq)])

    carry = (jnp.full((1, tq), NEG_BIG, F32), jnp.zeros((1, tq), F32), jnp.zeros((HEAD_DIM, tq), F32))
    carry = lax.fori_loop(0, lax.shift_right_logical(qi, 1), body, carry)

    def finish(with_prev):
        key = lax.broadcasted_iota(jnp.int32, (tq, tq), 0)
        qry = lax.broadcasted_iota(jnp.int32, (tq, tq), 1)
        sts = [_dot_nt(kmn_ref[...], qn) + lanes_tq(cref + mbc_ref[...])]
        vts = [vmt_ref[...]]
        if with_prev:
            prev = pl.multiple_of(q0 - tq, tq)
            sts.append(scores(prev, tq))
            vts.append(vt_ref[:, pl.ds(prev, tq)])
        sts.append(jnp.where(key <= qry, scores(q0, tq), NEG_BIG))
        vts.append(vt_ref[:, pl.ds(q0, tq)])
        m, l, acc = update(carry, jnp.concatenate(sts, axis=0), jnp.concatenate(vts, axis=1))
        o_ref[...] = (acc / l).T.astype(o_ref.dtype)

    odd = (qi & 1) == 1

    @pl.when(odd)
    def _():
        finish(True)

    @pl.when(jnp.logical_not(odd))
    def _():
        finish(False)


def _attention(uqkv, uqkv_meta, cr, mbias, q_gain, k_gain, batch, n_heads, col_q, tq):
    t = uqkv.shape[0]
    seq = t // batch
    n_meta = uqkv_meta.shape[0]
    nq = seq // tq
    bq = col_q // HEAD_DIM
    bk = bq + n_heads
    bv = bk + n_heads
    return pl.pallas_call(
        _attn_kernel,
        grid=(batch, n_heads, nq),
        in_specs=[pl.BlockSpec((tq, HEAD_DIM), lambda b, h, i: (b * nq + i, bq + h)),
                  pl.BlockSpec((seq, HEAD_DIM), lambda b, h, i: (b, bk + h)),
                  pl.BlockSpec((seq, HEAD_DIM), lambda b, h, i: (b, bv + h)),
                  pl.BlockSpec((n_meta, HEAD_DIM), lambda b, h, i: (0, bk + h)),
                  pl.BlockSpec((n_meta, HEAD_DIM), lambda b, h, i: (0, bv + h)),
                  pl.BlockSpec((None, None, 1, seq), lambda b, h, i: (b, h, 0, 0)),
                  pl.BlockSpec((None, 1, LANES), lambda b, h, i: (h, 0, 0)),
                  pl.BlockSpec((1, HEAD_DIM), lambda b, h, i: (0, 0)),
                  pl.BlockSpec((1, HEAD_DIM), lambda b, h, i: (0, 0))],
        out_specs=pl.BlockSpec((tq, HEAD_DIM), lambda b, h, i: (b * nq + i, h)),
        out_shape=jax.ShapeDtypeStruct((t, n_heads * HEAD_DIM), BF16),
        scratch_shapes=[pltpu.VMEM((seq, HEAD_DIM), BF16),
                        pltpu.VMEM((HEAD_DIM, seq), BF16),
                        pltpu.VMEM((seq, LANES), F32),
                        pltpu.VMEM((LANES, HEAD_DIM), BF16),
                        pltpu.VMEM((HEAD_DIM, LANES), BF16),
                        pltpu.VMEM((LANES, LANES), F32)],
        compiler_params=_params("arbitrary", "arbitrary", "arbitrary"),
        name="forgetting_attention",
    )(uqkv, uqkv, uqkv, uqkv_meta, uqkv_meta, cr, mbias, q_gain, k_gain)


def _merge_kernel(yp_ref, ya_ref, wp_ref, wa_ref, gp_ref, ga_ref, o_ref, wpb_ref, wab_ref):
    @pl.when(pl.program_id(1) == 0)
    def _():
        wpb_ref[...] = wp_ref[...].astype(BF16)
        wab_ref[...] = wa_ref[...].astype(BF16)

    a = _dot(yp_ref[...], wpb_ref[...])
    b = _dot(ya_ref[...], wab_ref[...])
    o_ref[...] = (gp_ref[...].astype(F32) * a + ga_ref[...].astype(F32) * b).astype(o_ref.dtype)


def _merge(y_pool, y_attn, w_bp, w_ba, gates, tm, tn):
    t, kp = y_pool.shape
    ka = y_attn.shape[1]
    d = w_bp.shape[1]
    nj = d // tn
    return pl.pallas_call(
        _merge_kernel,
        grid=(nj, t // tm),
        in_specs=[pl.BlockSpec((tm, kp), lambda j, i: (i, 0)),
                  pl.BlockSpec((tm, ka), lambda j, i: (i, 0)),
                  pl.BlockSpec((kp, tn), lambda j, i: (0, j)),
                  pl.BlockSpec((ka, tn), lambda j, i: (0, j)),
                  pl.BlockSpec((tm, tn), lambda j, i: (i, j)),
                  pl.BlockSpec((tm, tn), lambda j, i: (i, nj + j))],
        out_specs=pl.BlockSpec((tm, tn), lambda j, i: (i, j)),
        out_shape=jax.ShapeDtypeStruct((t, d), BF16),
        scratch_shapes=[pltpu.VMEM((kp, tn), BF16), pltpu.VMEM((ka, tn), BF16)],
        compiler_params=_params("arbitrary", "arbitrary"),
        name="branch_merge",
    )(y_pool, y_attn, w_bp, w_ba, gates, gates)


def _outproj_kernel(m_ref, w_ref, x_ref, o_ref, wbf_ref):
    @pl.when(pl.program_id(1) == 0)
    def _():
        wbf_ref[...] = w_ref[...].astype(BF16)

    o_ref[...] = x_ref[...] + _dot(m_ref[...], wbf_ref[...])


def _outproj(merged, w_out, x, tm, tn):
    t, k = merged.shape
    d = w_out.shape[1]
    return pl.pallas_call(
        _outproj_kernel,
        grid=(d // tn, t // tm),
        in_specs=[pl.BlockSpec((tm, k), lambda j, i: (i, 0)),
                  pl.BlockSpec((k, tn), lambda j, i: (0, j)),
                  pl.BlockSpec((tm, tn), lambda j, i: (i, j))],
        out_specs=pl.BlockSpec((tm, tn), lambda j, i: (i, j)),
        out_shape=jax.ShapeDtypeStruct((t, d), F32),
        scratch_shapes=[pltpu.VMEM((k, tn), BF16)],
        compiler_params=_params("arbitrary", "arbitrary"),
        name="out_proj",
    )(merged, w_out, x)


def _router_kernel(h_ref, g_ref, wr_ref, br_ref, xp_ref, idx_ref, wt_ref):
    y = _rms(h_ref[...], g_ref[...])
    y_hi = y.astype(BF16)
    y_lo = (y - y_hi.astype(F32)).astype(BF16)
    w = wr_ref[...]
    w_hi = w.astype(BF16)
    w_lo = (w - w_hi.astype(F32)).astype(BF16)
    logits = _dot_nt(w_hi, y_hi) + (_dot_nt(w_hi, y_lo) + _dot_nt(w_lo, y_hi)) + br_ref[...]

    half = y.shape[1] // 2
    bits = pltpu.bitcast(y_hi.astype(F32), jnp.uint32)
    xp_ref[...] = (bits[:, :half] >> 16) | bits[:, half:]

    n_exp = logits.shape[0]
    eid = lax.broadcasted_iota(jnp.int32, logits.shape, 0)
    vals = logits
    top_v, top_i = [], []
    for _ in range(TOP_K):
        mx = jnp.max(vals, axis=0, keepdims=True)
        sel = jnp.min(jnp.where(vals == mx, eid, n_exp), axis=0, keepdims=True)
        top_v.append(mx)
        top_i.append(sel)
        vals = jnp.where(eid == sel, -jnp.inf, vals)
    ex = [jnp.exp(v - top_v[0]) for v in top_v]
    den = ex[0] + ex[1] + ex[2] + ex[3]
    for k in range(TOP_K):
        idx_ref[k:k + 1, :] = top_i[k]
        wt_ref[k:k + 1, :] = ex[k] / den


def _router(h1, gain, w_router_t, b_router, tm):
    t, d = h1.shape
    n_exp = w_router_t.shape[0]
    return pl.pallas_call(
        _router_kernel,
        grid=(t // tm,),
        in_specs=[pl.BlockSpec((tm, d), lambda i: (i, 0)),
                  pl.BlockSpec((1, d), lambda i: (0, 0)),
                  pl.BlockSpec((n_exp, d), lambda i: (0, 0)),
                  pl.BlockSpec((n_exp, 1), lambda i: (0, 0))],
        out_specs=[pl.BlockSpec((tm, d // 2), lambda i: (i, 0)),
                   pl.BlockSpec((TOP_K, tm), lambda i: (0, i)),
                   pl.BlockSpec((TOP_K, tm), lambda i: (0, i))],
        out_shape=[jax.ShapeDtypeStruct((t, d // 2), jnp.uint32),
                   jax.ShapeDtypeStruct((TOP_K, t), jnp.int32),
                   jax.ShapeDtypeStruct((TOP_K, t), F32)],
        compiler_params=_params("arbitrary"),
        name="router_topk",
    )(h1, gain, w_router_t, b_router)


def _plan_kernel(idx_ref, dest_ref, grp_ref, rank_ref, *, n_exp):
    t = idx_ref.shape[1]
    nblk = t // LANES
    eid = lax.broadcasted_iota(jnp.int32, (n_exp, LANES), 0)
    r = lax.broadcasted_iota(jnp.int32, (LANES, LANES), 0)
    c = lax.broadcasted_iota(jnp.int32, (LANES, LANES), 1)
    upper = (r <= c).astype(BF16)

    def member(off):
        m = jnp.zeros((n_exp, LANES), F32)
        for k in range(TOP_K):
            m = m + (idx_ref[k:k + 1, pl.ds(off, LANES)] == eid).astype(F32)
        return m

    def count(blk, carry):
        off = pl.multiple_of(blk * LANES, LANES)
        m = member(off)
        cs = _dot(m.astype(BF16), upper) + carry
        rank_ref[:, pl.ds(off, LANES)] = cs - m
        return carry + jnp.sum(m, axis=1, keepdims=True)

    counts = lax.fori_loop(0, nblk, count, jnp.zeros((n_exp, 1), F32))
    padded = jnp.floor((counts + (EXPERT_BLOCK - 1)) * (1.0 / EXPERT_BLOCK)) * EXPERT_BLOCK
    er = lax.broadcasted_iota(jnp.int32, (n_exp, n_exp), 0)
    ec = lax.broadcasted_iota(jnp.int32, (n_exp, n_exp), 1)
    lower = (ec <= er).astype(BF16)
    pad_end = _dot(lower, jnp.broadcast_to(padded, (n_exp, LANES)).astype(BF16))[:, 0:1]
    pad_start = pad_end - padded

    def place(blk, _):
        off = pl.multiple_of(blk * LANES, LANES)
        pos = pad_start + rank_ref[:, pl.ds(off, LANES)]
        for k in range(TOP_K):
            sel = idx_ref[k:k + 1, pl.ds(off, LANES)] == eid
            d = jnp.sum(jnp.where(sel, pos, 0.0), axis=0, keepdims=True)
            dest_ref[blk, k:k + 1, :] = d.astype(jnp.int32)
        return 0

    lax.fori_loop(0, nblk, place, 0)

    lane = lax.broadcasted_iota(jnp.int32, (n_exp, LANES), 1)
    first = pad_start * (1.0 / EXPERT_BLOCK)
    nblocks = padded * (1.0 / EXPERT_BLOCK)
    grp_ref[...] = jnp.where(lane == 0, first, jnp.where(lane == 1, nblocks, 0.0)).astype(jnp.int32)


def _plan(idx, n_exp):
    t = idx.shape[1]
    assert t <= 256 * EXPERT_BLOCK
    dest, grp = pl.pallas_call(
        functools.partial(_plan_kernel, n_exp=n_exp),
        grid=(1,),
        in_specs=[pl.BlockSpec(idx.shape, lambda i: (0, 0))],
        out_specs=[pl.BlockSpec((t // LANES, TOP_K, LANES), lambda i: (0, 0, 0)),
                   pl.BlockSpec((n_exp, LANES), lambda i: (0, 0))],
        out_shape=[jax.ShapeDtypeStruct((t // LANES, TOP_K, LANES), jnp.int32),
                   jax.ShapeDtypeStruct((n_exp, LANES), jnp.int32)],
        scratch_shapes=[pltpu.VMEM((n_exp, t), F32)],
        compiler_params=_params("arbitrary"),
        name="dispatch_plan",
    )(idx)
    return dest, grp[:, 0], grp[:, 1]


def _dispatch_kernel(gs_ref, gn_ref, dest_ref, xp_ref, xs_hbm, zero_ref, sem, zsem):
    n_exp = gs_ref.shape[0]

    def zero_block(m):
        return pltpu.make_async_copy(zero_ref, xs_hbm.at[pl.ds(m * EXPERT_BLOCK, EXPERT_BLOCK), :], zsem)

    def zero_copy(e):
        return zero_block(gs_ref[e] + gn_ref[e] - 1)

    @pl.when(pl.program_id(0) == 0)
    def _():
        zero_ref[...] = jnp.zeros_like(zero_ref)
        used = gs_ref[n_exp - 1] + gn_ref[n_exp - 1]
        total = xs_hbm.shape[0] // EXPERT_BLOCK

        def start(e, _):
            @pl.when(gn_ref[e] > 0)
            def _():
                zero_copy(e).start()
            return 0

        def wait(e, _):
            @pl.when(gn_ref[e] > 0)
            def _():
                zero_copy(e).wait()
            return 0

        lax.fori_loop(0, n_exp, start, 0)
        lax.fori_loop(used, total, lambda m, _: (zero_block(m).start(), 0)[1], 0)
        lax.fori_loop(0, n_exp, wait, 0)
        lax.fori_loop(used, total, lambda m, _: (zero_block(m).wait(), 0)[1], 0)

    def row_copy(r, k):
        return pltpu.make_async_copy(xp_ref.at[pl.ds(r, 1), :], xs_hbm.at[pl.ds(dest_ref[k, r], 1), :], sem)

    def issue(r, _):
        for k in range(TOP_K):
            row_copy(r, k).start(priority=k % 2)
        return 0

    def drain(r, _):
        for k in range(TOP_K):
            row_copy(r, k).wait()
        return 0

    lax.fori_loop(0, LANES, issue, 0)
    lax.fori_loop(0, LANES, drain, 0)


def _dispatch(grp_start, grp_blocks, dest, xp, n_rows):
    t, w = xp.shape
    return pl.pallas_call(
        _dispatch_kernel,
        grid_spec=pltpu.PrefetchScalarGridSpec(
            num_scalar_prefetch=2,
            grid=(t // LANES,),
            in_specs=[pl.BlockSpec((None, TOP_K, LANES), lambda i, gs, gn: (i, 0, 0), memory_space=pltpu.SMEM),
                      pl.BlockSpec((LANES, w), lambda i, gs, gn: (i, 0))],
            out_specs=pl.BlockSpec(memory_space=pl.ANY),
            scratch_shapes=[pltpu.VMEM((EXPERT_BLOCK, w), xp.dtype), pltpu.SemaphoreType.DMA,
                            pltpu.SemaphoreType.DMA]),
        out_shape=jax.ShapeDtypeStruct((n_rows, w), xp.dtype),
        compiler_params=_params("arbitrary"),
        name="dispatch_rows",
    )(grp_start, grp_blocks, dest, xp)


def _weight_stage(w_hbms, stages, wsem, tn):
    n, e = pl.program_id(0), pl.program_id(1)
    n_tiles, n_exp = pl.num_programs(0), pl.num_programs(1)

    def copies(nn, ee):
        cols = pl.ds(pl.multiple_of(nn * tn, tn), tn)
        return [pltpu.make_async_copy(w.at[ee, :, cols], st, wsem.at[k])
                for k, (w, st) in enumerate(zip(w_hbms, stages))]

    @pl.when((n == 0) & (e == 0))
    def _():
        for c in copies(0, 0):
            c.start(priority=1)

    def wait_current():
        for c in copies(n, e):
            c.wait()

    def start_next():
        wrap = e == n_exp - 1

        @pl.when(jnp.logical_not(wrap & (n == n_tiles - 1)))
        def _():
            for c in copies(jnp.where(wrap, n + 1, n), jnp.where(wrap, 0, e + 1)):
                c.start(priority=1)

    return wait_current, start_next


def _group_loop(gs_ref, gn_ref, src_hbm, dst_hbm, col0, in_buf, out_buf, sem_in, sem_out, compute, after_first_fetch):
    e = pl.program_id(1)
    nb = gn_ref[e]
    b0 = gs_ref[e]
    width = out_buf.shape[2]
    cb = in_buf.shape[1] // EXPERT_BLOCK
    nch = lax.div(nb + (cb - 1), jnp.int32(cb))

    def rows(i, nblk):
        return pl.ds(pl.multiple_of((b0 + i * cb) * EXPERT_BLOCK, EXPERT_BLOCK), nblk * EXPERT_BLOCK)

    def fetch(i, slot):
        return pltpu.make_async_copy(src_hbm.at[rows(i, cb), :], in_buf.at[slot], sem_in.at[slot])

    def store(i, slot, nblk):
        return pltpu.make_async_copy(out_buf.at[slot, pl.ds(0, nblk * EXPERT_BLOCK), :],
                                     dst_hbm.at[rows(i, nblk), pl.ds(col0, width)], sem_out.at[slot])

    def blocks_in(i):
        return jnp.minimum(nb - i * cb, cb)

    def for_size(count, fn):
        for size in range(1, cb + 1):
            @pl.when(count == size)
            def _():
                fn(size)

    @pl.when(e == 0)
    def _():
        last = gs_ref.shape[0] - 1
        used = gs_ref[last] + gn_ref[last]
        total = dst_hbm.shape[0] // EXPERT_BLOCK
        out_buf[0, 0:EXPERT_BLOCK, :] = jnp.zeros((EXPERT_BLOCK, width), out_buf.dtype)

        def tail(m):
            blk = pl.ds(pl.multiple_of(m * EXPERT_BLOCK, EXPERT_BLOCK), EXPERT_BLOCK)
            return pltpu.make_async_copy(out_buf.at[0, pl.ds(0, EXPERT_BLOCK), :],
                                         dst_hbm.at[blk, pl.ds(col0, width)], sem_out.at[0])

        lax.fori_loop(used, total, lambda m, _: (tail(m).start(), 0)[1], 0)
        lax.fori_loop(used, total, lambda m, _: (tail(m).wait(), 0)[1], 0)

    @pl.when(nch > 0)
    def _():
        fetch(0, 0).start()

    after_first_fetch()

    def step(i, _):
        slot = i & 1
        fetch(i, slot).wait()

        @pl.when(i + 1 < nch)
        def _():
            fetch(i + 1, 1 - slot).start()

        @pl.when(i >= 2)
        def _():
            store(i - 2, slot, cb).wait()

        def run(size):
            n = size * EXPERT_BLOCK
            out_buf[slot, 0:n, :] = compute(in_buf[slot, 0:n, :])
            store(i, slot, size).start()

        for_size(blocks_in(i), run)
        return 0

    lax.fori_loop(0, nch, step, 0)

    @pl.when(nch >= 2)
    def _():
        store(nch - 2, nch & 1, cb).wait()

    @pl.when(nch >= 1)
    def _():
        for_size(blocks_in(nch - 1), lambda size: store(nch - 1, (nch - 1) & 1, size).wait())


def _gateup_kernel(gs_ref, gn_ref, xs_hbm, wg_hbm, wu_hbm, bg_ref, bu_ref, h_hbm,
                   wg_st, wu_st, wgb_ref, wub_ref, xbuf, hbuf, wsem, sem_in, sem_out):
    tn = wgb_ref.shape[1]
    half = xbuf.shape[2]
    wait_weights, start_next_weights = _weight_stage((wg_hbm, wu_hbm), (wg_st, wu_st), wsem, tn)

    def stage_weights():
        wait_weights()

        @pl.when(gn_ref[pl.program_id(1)] > 0)
        def _():
            wgb_ref[...] = wg_st[...].astype(BF16)
            wub_ref[...] = wu_st[...].astype(BF16)

        start_next_weights()

    def compute(xp):
        xa = pltpu.bitcast(xp << 16, F32).astype(BF16)
        xb = pltpu.bitcast(xp & jnp.uint32(0xFFFF0000), F32).astype(BF16)
        gte = _dot(xa, wgb_ref[0:half, :]) + _dot(xb, wgb_ref[half:, :]) + bg_ref[...]
        up = _dot(xa, wub_ref[0:half, :]) + _dot(xb, wub_ref[half:, :]) + bu_ref[...]
        gte = jnp.minimum(gte, SWIGLU_LIMIT)
        up = jnp.clip(up, -SWIGLU_LIMIT, SWIGLU_LIMIT)
        return (gte * jax.nn.sigmoid(SWIGLU_ALPHA * gte) * (up + 1.0)).astype(BF16)

    col0 = pl.multiple_of(pl.program_id(0) * tn, tn)
    _group_loop(gs_ref, gn_ref, xs_hbm, h_hbm, col0, xbuf, hbuf, sem_in, sem_out, compute, stage_weights)


def _gateup(grp_start, grp_blocks, xs, w_gate, w_up, b_gate, b_up, tn):
    n_rows, half = xs.shape
    n_exp, d, f = w_gate.shape
    chunk = CHUNK_BLOCKS * EXPERT_BLOCK
    any_spec = pl.BlockSpec(memory_space=pl.ANY)
    b_spec = pl.BlockSpec((None, 1, tn), lambda n, e, gs, gn: (e, 0, n))
    return pl.pallas_call(
        _gateup_kernel,
        grid_spec=pltpu.PrefetchScalarGridSpec(
            num_scalar_prefetch=2,
            grid=(f // tn, n_exp),
            in_specs=[any_spec, any_spec, any_spec, b_spec, b_spec],
            out_specs=any_spec,
            scratch_shapes=[pltpu.VMEM((d, tn), F32), pltpu.VMEM((d, tn), F32),
                            pltpu.VMEM((d, tn), BF16), pltpu.VMEM((d, tn), BF16),
                            pltpu.VMEM((2, chunk, half), xs.dtype), pltpu.VMEM((2, chunk, tn), BF16),
                            pltpu.SemaphoreType.DMA((2,)), pltpu.SemaphoreType.DMA((2,)),
                            pltpu.SemaphoreType.DMA((2,))]),
        out_shape=jax.ShapeDtypeStruct((n_rows, f), BF16),
        compiler_params=_params("arbitrary", "arbitrary"),
        name="expert_gate_up",
    )(grp_start, grp_blocks, xs, w_gate, w_up, b_gate, b_up)


def _down_kernel(gs_ref, gn_ref, h_hbm, wd_hbm, bd_ref, y_hbm, wd_st, wdb_ref, hbuf, ybuf, wsem, sem_in, sem_out):
    tn = wdb_ref.shape[1]
    wait_weights, start_next_weights = _weight_stage((wd_hbm,), (wd_st,), wsem, tn)

    def stage_weights():
        wait_weights()

        @pl.when(gn_ref[pl.program_id(1)] > 0)
        def _():
            wdb_ref[...] = wd_st[...].astype(BF16)

        start_next_weights()

    def compute(h):
        y = (_dot(h, wdb_ref[...]) + bd_ref[...]).astype(BF16).astype(F32)
        bits = pltpu.bitcast(y, jnp.uint32)
        return (bits[:, :tn // 2] >> 16) | bits[:, tn // 2:]

    col0 = pl.multiple_of(pl.program_id(0) * (tn // 2), tn // 2)
    _group_loop(gs_ref, gn_ref, h_hbm, y_hbm, col0, hbuf, ybuf, sem_in, sem_out, compute, stage_weights)


def _down(grp_start, grp_blocks, h, w_down, b_down, tn):
    n_rows, f = h.shape
    n_exp, _, d = w_down.shape
    return pl.pallas_call(
        _down_kernel,
        grid_spec=pltpu.PrefetchScalarGridSpec(
            num_scalar_prefetch=2,
            grid=(d // tn, n_exp),
            in_specs=[pl.BlockSpec(memory_space=pl.ANY),
                      pl.BlockSpec(memory_space=pl.ANY),
                      pl.BlockSpec((None, 1, tn), lambda n, e, gs, gn: (e, 0, n))],
            out_specs=pl.BlockSpec(memory_space=pl.ANY),
            scratch_shapes=[pltpu.VMEM((f, tn), F32), pltpu.VMEM((f, tn), BF16),
                            pltpu.VMEM((2, CHUNK_BLOCKS * EXPERT_BLOCK, f), h.dtype),
                            pltpu.VMEM((2, CHUNK_BLOCKS * EXPERT_BLOCK, tn // 2), jnp.uint32),
                            pltpu.SemaphoreType.DMA((1,)), pltpu.SemaphoreType.DMA((2,)),
                            pltpu.SemaphoreType.DMA((2,))]),
        out_shape=jax.ShapeDtypeStruct((n_rows, d // 2), jnp.uint32),
        compiler_params=_params("arbitrary", "arbitrary"),
        name="expert_down",
    )(grp_start, grp_blocks, h, w_down, b_down)


def _combine_kernel(dest_ref, dest_next_ref, wt_ref, h_ref, y_hbm, o_ref, buf_ref, sem, *, tile_cols):
    i = pl.program_id(0)
    slot = i & 1

    def row_copy(idx_ref, s, r, k):
        return pltpu.make_async_copy(y_hbm.at[pl.ds(idx_ref[k, r], 1), :], buf_ref.at[s, k, pl.ds(r, 1), :],
                                     sem.at[s])

    def gather(idx_ref, s):
        def issue(r, _):
            for k in range(TOP_K):
                row_copy(idx_ref, s, r, k).start(priority=k % 2)
            return 0

        lax.fori_loop(0, LANES, issue, 0)

    @pl.when(i == 0)
    def _():
        gather(dest_ref, 0)

    @pl.when(i + 1 < pl.num_programs(0))
    def _():
        gather(dest_next_ref, 1 - slot)

    def drain(r, _):
        for k in range(TOP_K):
            row_copy(dest_ref, slot, r, k).wait()
        return 0

    lax.fori_loop(0, LANES, drain, 0)

    wt = wt_ref[...]
    eye = (lax.broadcasted_iota(jnp.int32, (LANES, LANES), 0)
           == lax.broadcasted_iota(jnp.int32, (LANES, LANES), 1))
    wcols = [jnp.sum(jnp.where(eye, wt[k:k + 1, :], 0.0), axis=1, keepdims=True) for k in range(TOP_K)]
    half = tile_cols // 2
    for tile in range(h_ref.shape[1] // tile_cols):
        lo = h_ref[:, tile * tile_cols:tile * tile_cols + half]
        hi = h_ref[:, tile * tile_cols + half:(tile + 1) * tile_cols]
        for k in range(TOP_K):
            u = buf_ref[slot, k, :, tile * half:(tile + 1) * half]
            lo = lo + wcols[k] * pltpu.bitcast(u << 16, F32)
            hi = hi + wcols[k] * pltpu.bitcast(u & jnp.uint32(0xFFFF0000), F32)
        o_ref[:, tile * tile_cols:tile * tile_cols + half] = lo
        o_ref[:, tile * tile_cols + half:(tile + 1) * tile_cols] = hi


def _combine(dest, wts, h1, y, tile_cols):
    t, d = h1.shape
    nt = t // LANES
    idx_spec = lambda f: pl.BlockSpec((None, TOP_K, LANES), f, memory_space=pltpu.SMEM)
    return pl.pallas_call(
        functools.partial(_combine_kernel, tile_cols=tile_cols),
        grid=(nt,),
        in_specs=[idx_spec(lambda i: (i, 0, 0)),
                  idx_spec(lambda i: (jnp.minimum(i + 1, nt - 1), 0, 0)),
                  pl.BlockSpec((TOP_K, LANES), lambda i: (0, i)),
                  pl.BlockSpec((LANES, d), lambda i: (i, 0)),
                  pl.BlockSpec(memory_space=pl.ANY)],
        out_specs=pl.BlockSpec((LANES, d), lambda i: (i, 0)),
        out_shape=jax.ShapeDtypeStruct((t, d), F32),
        scratch_shapes=[pltpu.VMEM((2, TOP_K, LANES, y.shape[1]), y.dtype), pltpu.SemaphoreType.DMA((2,))],
        compiler_params=_params("arbitrary"),
        name="combine_rows",
    )(dest, dest, wts, h1, y)


def _layer(x, meta, norm_mix, w_in, b_forget, b_branch_gate, q_norm, k_norm, w_pool_group, pool_scale,
           w_branch_pool, w_branch_attn, w_out, norm_ffn, w_router, b_router, w_gate, b_gate, w_up, b_up,
           w_down, b_down):
    batch, seq, d = x.shape
    t = batch * seq
    n_meta = meta.shape[0]
    pool_w = w_pool_group.shape[0] * w_pool_group.shape[1]
    attn_w = w_branch_attn.shape[0]
    n_heads = attn_w // HEAD_DIM
    n_exp = w_router.shape[1]
    col_q = pool_w
    col_zf = pool_w + 3 * attn_w
    col_zg = col_zf + n_heads
    assert n_heads <= LANES and col_zf % LANES == 0

    tm = min(1024, t)
    x2 = x.reshape(t, d)

    hn = _rmsnorm(x2, norm_mix.reshape(1, d), min(256, t))
    hn_meta = _rmsnorm(meta, norm_mix.reshape(1, d), n_meta)

    w_in_t = w_in.T
    uqkv, uqkv_meta = _proj(hn, hn_meta, w_in_t, None, col0=0, ncols=col_zf, tn=512, tm=tm,
                            out_dtype=BF16, name="in_proj_uqkv")
    zf, zf_meta = _proj(hn, hn_meta, w_in_t, None, col0=col_zf, ncols=LANES, tn=LANES, tm=tm,
                        out_dtype=F32, name="in_proj_forget")
    gates = _proj(hn, None, w_in_t, b_branch_gate.reshape(1, 2 * d), col0=col_zf, ncols=2 * d, tn=512, tm=tm,
                  out_dtype=BF16, name="in_proj_gates", shift=col_zg - col_zf)

    bf_pad = jnp.pad(b_forget.reshape(1, n_heads), ((0, 0), (0, LANES - n_heads)))
    zfm_pad = jnp.pad(zf_meta, ((0, LANES - n_meta), (0, 0)))
    cr, mbias = _cumsum(zf, zfm_pad, bf_pad, batch, n_heads, n_meta)

    y_pool = _pool(uqkv, uqkv_meta, w_pool_group, pool_scale.reshape(1, pool_w), batch, min(512, seq))
    y_attn = _attention(uqkv, uqkv_meta, cr, mbias, q_norm.reshape(1, HEAD_DIM), k_norm.reshape(1, HEAD_DIM),
                        batch, n_heads, col_q, min(512, seq))

    merged = _merge(y_pool, y_attn, w_branch_pool, w_branch_attn, gates, tm, 512)
    h1 = _outproj(merged, w_out, x2, tm, 512)

    xp, idx, wts = _router(h1, norm_ffn.reshape(1, d), w_router.T, b_router.reshape(n_exp, 1), min(256, t))
    n_blocks = -(-(t * TOP_K + n_exp * (EXPERT_BLOCK - 1)) // EXPERT_BLOCK) + CHUNK_BLOCKS - 1
    dest, grp_start, grp_blocks = _plan(idx, n_exp)
    xs = _dispatch(grp_start, grp_blocks, dest, xp, n_blocks * EXPERT_BLOCK)
    f = w_gate.shape[2]
    hmid = _gateup(grp_start, grp_blocks, xs, w_gate, w_up, b_gate.reshape(n_exp, 1, f), b_up.reshape(n_exp, 1, f), 512)
    down_tn = 2048
    y = _down(grp_start, grp_blocks, hmid, w_down, b_down.reshape(n_exp, 1, d), down_tn)
    out = _combine(dest, wts, h1, y, down_tn)
    return out.reshape(batch, seq, d)


def kernel(x, meta_tokens, norm_mix, w_in, b_forget, b_branch_gate, q_norm, k_norm, w_pool_group, pool_scale,
           w_branch_pool, w_branch_attn, w_out, norm_ffn, w_router, b_router, w_gate, b_gate, w_up, b_up,
           w_down, b_down):
    depth = norm_mix.shape[0]
    assert depth == 1, "the fused layer pipeline drops the meta rows after the (single) layer"
    return _layer(x, meta_tokens, norm_mix[0], w_in[0], b_forget[0], b_branch_gate[0], q_norm[0], k_norm[0],
                  w_pool_group[0], pool_scale[0], w_branch_pool[0], w_branch_attn[0], w_out[0], norm_ffn[0],
                  w_router[0], b_router[0], w_gate[0], b_gate[0], w_up[0], b_up[0], w_down[0], b_down[0])
```

```python
import functools

import jax
import jax.numpy as jnp
from jax import lax
from jax.experimental import pallas as pl
from jax.experimental.pallas import tpu as pltpu

NORM_EPS = 1e-6
HEAD_DIM = 128
POOL_WINDOWS = (2, 4, 8, 16)
TOP_K = 4
SWIGLU_LIMIT = 7.0
SWIGLU_ALPHA = 1.702
EXPERT_BLOCK = 128
CHUNK_BLOCKS = 4
LANES = 128
NEG_BIG = -1e30
LOG2E = 1.4426950408889634
ONES_ROWS = 16
HEAD_GROUP = 2
VMEM_LIMIT = 56 * 1024 * 1024

F32 = jnp.float32
BF16 = jnp.bfloat16


def _params(*sem, vmem=VMEM_LIMIT):
    return pltpu.CompilerParams(dimension_semantics=sem, vmem_limit_bytes=vmem)


def _dot(a, b):
    return jnp.dot(a, b, preferred_element_type=F32)


def _dot_nt(a, b):
    return lax.dot_general(a, b, (((1,), (1,)), ((), ())), preferred_element_type=F32)


def _rms(x, gain):
    ms = jnp.mean(x * x, axis=-1, keepdims=True)
    return x * lax.rsqrt(ms + NORM_EPS) * gain


def _rmsnorm_kernel(x_ref, g_ref, o_ref):
    o_ref[...] = _rms(x_ref[...], g_ref[...]).astype(o_ref.dtype)


def _rmsnorm(x, gain, tm):
    rows, d = x.shape
    return pl.pallas_call(
        _rmsnorm_kernel,
        grid=(rows // tm,),
        in_specs=[pl.BlockSpec((tm, d), lambda i: (i, 0)), pl.BlockSpec((1, d), lambda i: (0, 0))],
        out_specs=pl.BlockSpec((tm, d), lambda i: (i, 0)),
        out_shape=jax.ShapeDtypeStruct((rows, d), BF16),
        compiler_params=_params("arbitrary"),
        name="rmsnorm",
    )(x, gain)


def _proj_kernel(*refs, has_meta, gate_bias, shift):
    it = iter(refs)
    x_ref = next(it)
    xm_ref = next(it) if has_meta else None
    w_ref = next(it)
    wn_ref = next(it) if shift else None
    b_ref = next(it) if gate_bias else None
    o_ref = next(it)
    om_ref = next(it) if has_meta else None
    wbf_ref = next(it)
    tn, k = w_ref.shape

    @pl.when(pl.program_id(1) == 0)
    def _():
        def chunk(c, _):
            off = pl.multiple_of(c * LANES, LANES)
            if shift:
                wt = jnp.concatenate([w_ref[shift:tn, pl.ds(off, LANES)], wn_ref[:, pl.ds(off, LANES)]], axis=0)
            else:
                wt = w_ref[:, pl.ds(off, LANES)]
            wbf_ref[pl.ds(off, LANES), :] = wt.T.astype(BF16)
            return 0

        lax.fori_loop(0, k // LANES, chunk, 0)
        if has_meta:
            om_ref[...] = _dot(xm_ref[...], wbf_ref[...]).astype(om_ref.dtype)

    acc = _dot(x_ref[...], wbf_ref[...])
    if gate_bias:
        acc = jax.nn.sigmoid(acc + b_ref[...])
    o_ref[...] = acc.astype(o_ref.dtype)


def _proj(x, xm, wt, bias, *, col0, ncols, tn, tm, out_dtype, name, shift=0):
    rows, k = x.shape
    assert col0 % tn == 0 and ncols % tn == 0 and rows % tm == 0
    jb = col0 // tn
    in_specs = [pl.BlockSpec((tm, k), lambda j, i: (i, 0))]
    args = [x]
    if xm is not None:
        in_specs.append(pl.BlockSpec(xm.shape, lambda j, i: (0, 0)))
        args.append(xm)
    in_specs.append(pl.BlockSpec((tn, k), lambda j, i: (jb + j, 0)))
    args.append(wt)
    if shift:
        assert shift % 8 == 0 and tn % shift == 0 and (col0 + ncols + shift) <= wt.shape[0]
        per = tn // shift
        in_specs.append(pl.BlockSpec((shift, k), lambda j, i: ((jb + j + 1) * per, 0)))
        args.append(wt)
    if bias is not None:
        in_specs.append(pl.BlockSpec((1, tn), lambda j, i: (0, j)))
        args.append(bias)
    out_specs = [pl.BlockSpec((tm, tn), lambda j, i: (i, j))]
    out_shape = [jax.ShapeDtypeStruct((rows, ncols), out_dtype)]
    if xm is not None:
        out_specs.append(pl.BlockSpec((xm.shape[0], tn), lambda j, i: (0, j)))
        out_shape.append(jax.ShapeDtypeStruct((xm.shape[0], ncols), out_dtype))
    res = pl.pallas_call(
        functools.partial(_proj_kernel, has_meta=xm is not None, gate_bias=bias is not None, shift=shift),
        grid=(ncols // tn, rows // tm),
        in_specs=in_specs,
        out_specs=out_specs,
        out_shape=out_shape,
        scratch_shapes=[pltpu.VMEM((k, tn), BF16)],
        compiler_params=_params("arbitrary", "arbitrary"),
        name=name,
    )(*args)
    return res if xm is not None else res[0]


def _cumsum_kernel(zf_ref, zfm_ref, bf_ref, cr_ref, mb_ref, *, n_meta):
    seq = zf_ref.shape[0]
    bf = bf_ref[...]
    r = lax.broadcasted_iota(jnp.int32, (LANES, LANES), 0)
    c = lax.broadcasted_iota(jnp.int32, (LANES, LANES), 1)
    tri = (c <= r).astype(F32)
    sfx = (c > r).astype(F32)

    def prefix(mat, x):
        return jnp.dot(mat, x, preferred_element_type=F32, precision=lax.Precision.HIGHEST)

    n_meta_pad = zfm_ref.shape[0]
    lfm = jax.nn.log_sigmoid(zfm_ref[...] + bf)
    rows = lax.broadcasted_iota(jnp.int32, (n_meta_pad, LANES), 0)
    lfm = jnp.where(rows < n_meta, lfm, 0.0)
    mbias = jnp.where(rows < n_meta, prefix(sfx, lfm), NEG_BIG)
    mbias_t = mbias.T
    for h in range(mb_ref.shape[0]):
        mb_ref[h] = mbias_t[h:h + 1, :]

    def body(blk, carry):
        off = pl.multiple_of(blk * LANES, LANES)
        lf = jax.nn.log_sigmoid(zf_ref[pl.ds(off, LANES), :] + bf)
        cs = prefix(tri, lf) + carry
        cs_t = cs.T
        for h in range(cr_ref.shape[0]):
            cr_ref[h, :, pl.ds(off, LANES)] = cs_t[h:h + 1, :]
        return cs[LANES - 1:LANES, :]

    lax.fori_loop(0, seq // LANES, body, jnp.zeros((1, LANES), F32))


def _cumsum(zf, zfm_pad, bf_pad, batch, n_heads, n_meta):
    t = zf.shape[0]
    seq = t // batch
    return pl.pallas_call(
        functools.partial(_cumsum_kernel, n_meta=n_meta),
        grid=(batch,),
        in_specs=[pl.BlockSpec((seq, LANES), lambda b: (b, 0)),
                  pl.BlockSpec(zfm_pad.shape, lambda b: (0, 0)),
                  pl.BlockSpec((1, LANES), lambda b: (0, 0))],
        out_specs=[pl.BlockSpec((None, n_heads, 1, seq), lambda b: (b, 0, 0, 0)),
                   pl.BlockSpec((n_heads, 1, LANES), lambda b: (0, 0, 0))],
        out_shape=[jax.ShapeDtypeStruct((batch, n_heads, 1, seq), F32),
                   jax.ShapeDtypeStruct((n_heads, 1, LANES), F32)],
        compiler_params=_params("arbitrary"),
        name="forget_cumsum",
    )(zf, zfm_pad, bf_pad)


def _pool_kernel(u_ref, um_ref, wp_ref, ps_ref, o_ref, ext_ref, wbf_ref):
    ts = u_ref.shape[0]
    halo = um_ref.shape[0]
    gw = wp_ref.shape[1]
    i = pl.program_id(1)

    @pl.when((pl.program_id(0) == 0) & (i == 0))
    def _():
        wbf_ref[...] = wp_ref[...].astype(BF16)

    @pl.when(i == 0)
    def _():
        ext_ref[0:halo, :] = um_ref[...].astype(F32)

    @pl.when(i > 0)
    def _():
        ext_ref[0:halo, :] = ext_ref[ts:ts + halo, :]

    ext_ref[halo:halo + ts, :] = u_ref[...].astype(F32)
    for g, w in enumerate(POOL_WINDOWS):
        cols = slice(g * gw, (g + 1) * gw)
        uf = ext_ref[halo:halo + ts, cols]
        acc = uf
        for d in range(1, w):
            acc = acc + ext_ref[halo - d:halo - d + ts, cols]
        pooled = acc * (1.0 / w) - uf
        y = _dot(pooled.astype(BF16), wbf_ref[g]) * ps_ref[:, cols]
        o_ref[:, cols] = y.astype(o_ref.dtype)


def _pool(uqkv, uqkv_meta, w_pool, pool_scale, batch, ts):
    t = uqkv.shape[0]
    seq = t // batch
    ng, gw, _ = w_pool.shape
    pw = ng * gw
    halo = uqkv_meta.shape[0]
    assert halo >= max(POOL_WINDOWS) and seq % ts == 0
    nt = seq // ts
    return pl.pallas_call(
        _pool_kernel,
        grid=(batch, nt),
        in_specs=[pl.BlockSpec((ts, pw), lambda b, i: (b * nt + i, 0)),
                  pl.BlockSpec((halo, pw), lambda b, i: (0, 0)),
                  pl.BlockSpec(w_pool.shape, lambda b, i: (0, 0, 0)),
                  pl.BlockSpec((1, pw), lambda b, i: (0, 0))],
        out_specs=pl.BlockSpec((ts, pw), lambda b, i: (b * nt + i, 0)),
        out_shape=jax.ShapeDtypeStruct((t, pw), BF16),
        scratch_shapes=[pltpu.VMEM((halo + ts, pw), F32), pltpu.VMEM(w_pool.shape, BF16)],
        compiler_params=_params("arbitrary", "arbitrary"),
        name="pool_mixer",
    )(uqkv, uqkv_meta, w_pool, pool_scale)


def _attn_kernel(q_ref, k_ref, v_ref, km_ref, vm_ref, cr_ref, mb_ref, qg_ref, kg_ref, o_ref,
                 ka_ref, va_ref, kma_ref, vma_ref):
    tq = q_ref.shape[0]
    seq = k_ref.shape[0]
    n_meta = km_ref.shape[0]
    n_group, va_rows = va_ref.shape[0], va_ref.shape[1]
    qi = pl.program_id(2)
    lane = lax.broadcasted_iota(jnp.int32, (LANES, LANES), 1)
    heads = [(g, slice(g * HEAD_DIM, (g + 1) * HEAD_DIM)) for g in range(n_group)]

    def bias_pieces(col):
        hi = col.astype(BF16).astype(F32)
        mid = (col - hi).astype(BF16).astype(F32)
        lo = (col - hi - mid).astype(BF16).astype(F32)
        return jnp.where(lane == 0, hi, jnp.where(lane == 1, mid, jnp.where(lane == 2, lo, 0.0))).astype(BF16)

    ones_row = (lax.broadcasted_iota(jnp.int32, (va_rows - HEAD_DIM, LANES), 0) == 0).astype(F32)

    @pl.when(qi == 0)
    def _():
        def chunk(i, _):
            off = pl.multiple_of(i * LANES, LANES)
            for g, cols in heads:
                vt = v_ref[pl.ds(off, LANES), cols].astype(F32).T
                va_ref[g, :, pl.ds(off, LANES)] = jnp.concatenate([vt, ones_row], axis=0).astype(BF16)
                col = jnp.broadcast_to(cr_ref[g, :, pl.ds(off, LANES)] * LOG2E, (LANES, LANES)).T
                ka_ref[g, pl.ds(off, LANES), HEAD_DIM:] = bias_pieces(col)
            return 0

        lax.fori_loop(0, seq // LANES, chunk, 0)
        pad = jnp.zeros((LANES - n_meta, HEAD_DIM), F32)
        for g, cols in heads:
            ka_ref[g, :, 0:HEAD_DIM] = _rms(k_ref[:, cols].astype(F32), kg_ref[...]).astype(BF16)
            kmn = jnp.concatenate([_rms(km_ref[:, cols].astype(F32), kg_ref[...]), pad], axis=0)
            kma_ref[g, :, 0:HEAD_DIM] = kmn.astype(BF16)
            kma_ref[g, :, HEAD_DIM:] = bias_pieces(-jnp.broadcast_to(mb_ref[g] * LOG2E, (LANES, LANES)).T)
            vmt = jnp.concatenate([vm_ref[:, cols].astype(F32), pad], axis=0).T
            vma_ref[g] = jnp.concatenate([vmt, ones_row], axis=0).astype(BF16)

    minus_ones = jnp.where(lax.broadcasted_iota(jnp.int32, (tq, LANES), 1) < 3, -1.0, 0.0).astype(BF16)
    qa = [jnp.concatenate([(_rms(q_ref[:, cols].astype(F32), qg_ref[...]) * (HEAD_DIM ** -0.5 * LOG2E)).astype(BF16),
                           minus_ones], axis=1) for g, cols in heads]
    q0 = pl.multiple_of(qi * tq, tq)

    def update(carry, st, va_blk):
        m, acc = carry
        m_new = jnp.maximum(m, jnp.max(st, axis=0, keepdims=True))
        p = jnp.exp2(st - m_new)
        acc = jnp.exp2(m - m_new) * acc + _dot(va_blk, p.astype(BF16))
        return m_new, acc

    def scores(g, off, size):
        return _dot_nt(ka_ref[g, pl.ds(off, size), :], qa[g])

    def body(j, carry):
        off = pl.multiple_of(j * (2 * tq), 2 * tq)
        return tuple(update(carry[g], scores(g, off, 2 * tq), va_ref[g, :, pl.ds(off, 2 * tq)]) for g, _ in heads)

    carry = tuple((jnp.full((1, tq), NEG_BIG, F32), jnp.zeros((va_rows, tq), F32)) for _ in heads)
    carry = lax.fori_loop(0, lax.shift_right_logical(qi, 1), body, carry)

    def finish(with_prev):
        key = lax.broadcasted_iota(jnp.int32, (tq, tq), 0)
        qry = lax.broadcasted_iota(jnp.int32, (tq, tq), 1)
        for g, cols in heads:
            sts = [_dot_nt(kma_ref[g], qa[g])]
            vas = [vma_ref[g]]
            if with_prev:
                prev = pl.multiple_of(q0 - tq, tq)
                sts.append(scores(g, prev, tq))
                vas.append(va_ref[g, :, pl.ds(prev, tq)])
            sts.append(jnp.where(key <= qry, scores(g, q0, tq), NEG_BIG))
            vas.append(va_ref[g, :, pl.ds(q0, tq)])
            m, acc = update(carry[g], jnp.concatenate(sts, axis=0), jnp.concatenate(vas, axis=1))
            o_ref[:, cols] = (acc[0:HEAD_DIM, :] / acc[HEAD_DIM:HEAD_DIM + 1, :]).T.astype(o_ref.dtype)

    odd = (qi & 1) == 1

    @pl.when(odd)
    def _():
        finish(True)

    @pl.when(jnp.logical_not(odd))
    def _():
        finish(False)


def _attention(uqkv, uqkv_meta, cr, mbias, q_gain, k_gain, batch, n_heads, col_q, tq):
    t = uqkv.shape[0]
    seq = t // batch
    n_meta = uqkv_meta.shape[0]
    nq = seq // tq
    grp = HEAD_GROUP
    gw = grp * HEAD_DIM
    assert n_heads % grp == 0 and col_q % gw == 0 and (n_heads * HEAD_DIM) % gw == 0
    bq = col_q // gw
    bk = bq + n_heads // grp
    bv = bk + n_heads // grp
    return pl.pallas_call(
        _attn_kernel,
        grid=(batch, n_heads // grp, nq),
        in_specs=[pl.BlockSpec((tq, gw), lambda b, h, i: (b * nq + i, bq + h)),
                  pl.BlockSpec((seq, gw), lambda b, h, i: (b, bk + h)),
                  pl.BlockSpec((seq, gw), lambda b, h, i: (b, bv + h)),
                  pl.BlockSpec((n_meta, gw), lambda b, h, i: (0, bk + h)),
                  pl.BlockSpec((n_meta, gw), lambda b, h, i: (0, bv + h)),
                  pl.BlockSpec((None, grp, 1, seq), lambda b, h, i: (b, h, 0, 0)),
                  pl.BlockSpec((grp, 1, LANES), lambda b, h, i: (h, 0, 0)),
                  pl.BlockSpec((1, HEAD_DIM), lambda b, h, i: (0, 0)),
                  pl.BlockSpec((1, HEAD_DIM), lambda b, h, i: (0, 0))],
        out_specs=pl.BlockSpec((tq, gw), lambda b, h, i: (b * nq + i, h)),
        out_shape=jax.ShapeDtypeStruct((t, n_heads * HEAD_DIM), BF16),
        scratch_shapes=[pltpu.VMEM((grp, seq, HEAD_DIM + LANES), BF16),
                        pltpu.VMEM((grp, HEAD_DIM + ONES_ROWS, seq), BF16),
                        pltpu.VMEM((grp, LANES, HEAD_DIM + LANES), BF16),
                        pltpu.VMEM((grp, HEAD_DIM + ONES_ROWS, LANES), BF16)],
        compiler_params=_params("arbitrary", "arbitrary", "arbitrary"),
        name="forgetting_attention",
    )(uqkv, uqkv, uqkv, uqkv_meta, uqkv_meta, cr, mbias, q_gain, k_gain)


def _merge_kernel(yp_ref, ya_ref, wp_ref, wa_ref, gp_ref, ga_ref, o_ref, wpb_ref, wab_ref):
    @pl.when(pl.program_id(1) == 0)
    def _():
        wpb_ref[...] = wp_ref[...].astype(BF16)
        wab_ref[...] = wa_ref[...].astype(BF16)

    a = _dot(yp_ref[...], wpb_ref[...])
    b = _dot(ya_ref[...], wab_ref[...])
    o_ref[...] = (gp_ref[...].astype(F32) * a + ga_ref[...].astype(F32) * b).astype(o_ref.dtype)


def _merge(y_pool, y_attn, w_bp, w_ba, gates, tm, tn):
    t, kp = y_pool.shape
    ka = y_attn.shape[1]
    d = w_bp.shape[1]
    nj = d // tn
    return pl.pallas_call(
        _merge_kernel,
        grid=(nj, t // tm),
        in_specs=[pl.BlockSpec((tm, kp), lambda j, i: (i, 0)),
                  pl.BlockSpec((tm, ka), lambda j, i: (i, 0)),
                  pl.BlockSpec((kp, tn), lambda j, i: (0, j)),
                  pl.BlockSpec((ka, tn), lambda j, i: (0, j)),
                  pl.BlockSpec((tm, tn), lambda j, i: (i, j)),
                  pl.BlockSpec((tm, tn), lambda j, i: (i, nj + j))],
        out_specs=pl.BlockSpec((tm, tn), lambda j, i: (i, j)),
        out_shape=jax.ShapeDtypeStruct((t, d), BF16),
        scratch_shapes=[pltpu.VMEM((kp, tn), BF16), pltpu.VMEM((ka, tn), BF16)],
        compiler_params=_params("arbitrary", "arbitrary"),
        name="branch_merge",
    )(y_pool, y_attn, w_bp, w_ba, gates, gates)


def _outproj_kernel(m_ref, w_ref, x_ref, o_ref, wbf_ref):
    @pl.when(pl.program_id(1) == 0)
    def _():
        wbf_ref[...] = w_ref[...].astype(BF16)

    o_ref[...] = x_ref[...] + _dot(m_ref[...], wbf_ref[...])


def _outproj(merged, w_out, x, tm, tn):
    t, k = merged.shape
    d = w_out.shape[1]
    return pl.pallas_call(
        _outproj_kernel,
        grid=(d // tn, t // tm),
        in_specs=[pl.BlockSpec((tm, k), lambda j, i: (i, 0)),
                  pl.BlockSpec((k, tn), lambda j, i: (0, j)),
                  pl.BlockSpec((tm, tn), lambda j, i: (i, j))],
        out_specs=pl.BlockSpec((tm, tn), lambda j, i: (i, j)),
        out_shape=jax.ShapeDtypeStruct((t, d), F32),
        scratch_shapes=[pltpu.VMEM((k, tn), BF16)],
        compiler_params=_params("arbitrary", "arbitrary"),
        name="out_proj",
    )(merged, w_out, x)


def _router_kernel(h_ref, g_ref, wr_ref, br_ref, xp_ref, idx_ref, wt_ref):
    y = _rms(h_ref[...], g_ref[...])
    y_hi = y.astype(BF16)
    y_lo = (y - y_hi.astype(F32)).astype(BF16)
    w = wr_ref[...]
    w_hi = w.astype(BF16)
    w_lo = (w - w_hi.astype(F32)).astype(BF16)
    logits = _dot_nt(w_hi, y_hi) + (_dot_nt(w_hi, y_lo) + _dot_nt(w_lo, y_hi)) + br_ref[...]

    half = y.shape[1] // 2
    bits = pltpu.bitcast(y_hi.astype(F32), jnp.uint32)
    xp_ref[...] = (bits[:, :half] >> 16) | bits[:, half:]

    n_exp = logits.shape[0]
    eid = lax.broadcasted_iota(jnp.int32, logits.shape, 0)
    vals = logits
    top_v, top_i = [], []
    for _ in range(TOP_K):
        mx = jnp.max(vals, axis=0, keepdims=True)
        sel = jnp.min(jnp.where(vals == mx, eid, n_exp), axis=0, keepdims=True)
        top_v.append(mx)
        top_i.append(sel)
        vals = jnp.where(eid == sel, -jnp.inf, vals)
    ex = [jnp.exp(v - top_v[0]) for v in top_v]
    den = ex[0] + ex[1] + ex[2] + ex[3]
    for k in range(TOP_K):
        idx_ref[k:k + 1, :] = top_i[k]
        wt_ref[k:k + 1, :] = ex[k] / den


def _router(h1, gain, w_router_t, b_router, tm):
    t, d = h1.shape
    n_exp = w_router_t.shape[0]
    return pl.pallas_call(
        _router_kernel,
        grid=(t // tm,),
        in_specs=[pl.BlockSpec((tm, d), lambda i: (i, 0)),
                  pl.BlockSpec((1, d), lambda i: (0, 0)),
                  pl.BlockSpec((n_exp, d), lambda i: (0, 0)),
                  pl.BlockSpec((n_exp, 1), lambda i: (0, 0))],
        out_specs=[pl.BlockSpec((tm, d // 2), lambda i: (i, 0)),
                   pl.BlockSpec((TOP_K, tm), lambda i: (0, i)),
                   pl.BlockSpec((TOP_K, tm), lambda i: (0, i))],
        out_shape=[jax.ShapeDtypeStruct((t, d // 2), jnp.uint32),
                   jax.ShapeDtypeStruct((TOP_K, t), jnp.int32),
                   jax.ShapeDtypeStruct((TOP_K, t), F32)],
        compiler_params=_params("arbitrary"),
        name="router_topk",
    )(h1, gain, w_router_t, b_router)


def _plan_kernel(idx_ref, dest_ref, grp_ref, rank_ref, *, n_exp):
    t = idx_ref.shape[1]
    nblk = t // LANES
    eid = lax.broadcasted_iota(jnp.int32, (n_exp, LANES), 0)
    r = lax.broadcasted_iota(jnp.int32, (LANES, LANES), 0)
    c = lax.broadcasted_iota(jnp.int32, (LANES, LANES), 1)
    upper = (r <= c).astype(BF16)

    def member(off):
        m = jnp.zeros((n_exp, LANES), F32)
        for k in range(TOP_K):
            m = m + (idx_ref[k:k + 1, pl.ds(off, LANES)] == eid).astype(F32)
        return m

    def count(blk, carry):
        off = pl.multiple_of(blk * LANES, LANES)
        m = member(off)
        cs = _dot(m.astype(BF16), upper) + carry
        rank_ref[:, pl.ds(off, LANES)] = cs - m
        return carry + jnp.sum(m, axis=1, keepdims=True)

    counts = lax.fori_loop(0, nblk, count, jnp.zeros((n_exp, 1), F32))
    padded = jnp.floor((counts + (EXPERT_BLOCK - 1)) * (1.0 / EXPERT_BLOCK)) * EXPERT_BLOCK
    er = lax.broadcasted_iota(jnp.int32, (n_exp, n_exp), 0)
    ec = lax.broadcasted_iota(jnp.int32, (n_exp, n_exp), 1)
    lower = (ec <= er).astype(BF16)
    pad_end = _dot(lower, jnp.broadcast_to(padded, (n_exp, LANES)).astype(BF16))[:, 0:1]
    pad_start = pad_end - padded

    def place(blk, _):
        off = pl.multiple_of(blk * LANES, LANES)
        pos = pad_start + rank_ref[:, pl.ds(off, LANES)]
        for k in range(TOP_K):
            sel = idx_ref[k:k + 1, pl.ds(off, LANES)] == eid
            d = jnp.sum(jnp.where(sel, pos, 0.0), axis=0, keepdims=True)
            dest_ref[blk, k:k + 1, :] = d.astype(jnp.int32)
        return 0

    lax.fori_loop(0, nblk, place, 0)

    lane = lax.broadcasted_iota(jnp.int32, (n_exp, LANES), 1)
    first = pad_start * (1.0 / EXPERT_BLOCK)
    nblocks = padded * (1.0 / EXPERT_BLOCK)
    grp_ref[...] = jnp.where(lane == 0, first, jnp.where(lane == 1, nblocks, 0.0)).astype(jnp.int32)


def _plan(idx, n_exp):
    t = idx.shape[1]
    assert t <= 256 * EXPERT_BLOCK
    dest, grp = pl.pallas_call(
        functools.partial(_plan_kernel, n_exp=n_exp),
        grid=(1,),
        in_specs=[pl.BlockSpec(idx.shape, lambda i: (0, 0))],
        out_specs=[pl.BlockSpec((t // LANES, TOP_K, LANES), lambda i: (0, 0, 0)),
                   pl.BlockSpec((n_exp, LANES), lambda i: (0, 0))],
        out_shape=[jax.ShapeDtypeStruct((t // LANES, TOP_K, LANES), jnp.int32),
                   jax.ShapeDtypeStruct((n_exp, LANES), jnp.int32)],
        scratch_shapes=[pltpu.VMEM((n_exp, t), F32)],
        compiler_params=_params("arbitrary"),
        name="dispatch_plan",
    )(idx)
    return dest, grp[:, 0], grp[:, 1]


def _dispatch_kernel(gs_ref, gn_ref, dest_ref, xp_ref, xs_hbm, zero_ref, sem, zsem):
    n_exp = gs_ref.shape[0]

    def zero_block(m):
        return pltpu.make_async_copy(zero_ref, xs_hbm.at[pl.ds(m * EXPERT_BLOCK, EXPERT_BLOCK), :], zsem)

    def zero_copy(e):
        return zero_block(gs_ref[e] + gn_ref[e] - 1)

    @pl.when(pl.program_id(0) == 0)
    def _():
        zero_ref[...] = jnp.zeros_like(zero_ref)
        used = gs_ref[n_exp - 1] + gn_ref[n_exp - 1]
        total = xs_hbm.shape[0] // EXPERT_BLOCK

        def start(e, _):
            @pl.when(gn_ref[e] > 0)
            def _():
                zero_copy(e).start()
            return 0

        def wait(e, _):
            @pl.when(gn_ref[e] > 0)
            def _():
                zero_copy(e).wait()
            return 0

        lax.fori_loop(0, n_exp, start, 0)
        lax.fori_loop(used, total, lambda m, _: (zero_block(m).start(), 0)[1], 0)
        lax.fori_loop(0, n_exp, wait, 0)
        lax.fori_loop(used, total, lambda m, _: (zero_block(m).wait(), 0)[1], 0)

    def row_copy(r, k):
        return pltpu.make_async_copy(xp_ref.at[pl.ds(r, 1), :], xs_hbm.at[pl.ds(dest_ref[k, r], 1), :], sem)

    def issue(r, _):
        for k in range(TOP_K):
            row_copy(r, k).start(priority=k % 2)
        return 0

    def drain(r, _):
        for k in range(TOP_K):
            row_copy(r, k).wait()
        return 0

    lax.fori_loop(0, LANES, issue, 0, unroll=8)
    lax.fori_loop(0, LANES, drain, 0)


def _dispatch(grp_start, grp_blocks, dest, xp, n_rows):
    t, w = xp.shape
    return pl.pallas_call(
        _dispatch_kernel,
        grid_spec=pltpu.PrefetchScalarGridSpec(
            num_scalar_prefetch=2,
            grid=(t // LANES,),
            in_specs=[pl.BlockSpec((None, TOP_K, LANES), lambda i, gs, gn: (i, 0, 0), memory_space=pltpu.SMEM),
                      pl.BlockSpec((LANES, w), lambda i, gs, gn: (i, 0))],
            out_specs=pl.BlockSpec(memory_space=pl.ANY),
            scratch_shapes=[pltpu.VMEM((EXPERT_BLOCK, w), xp.dtype), pltpu.SemaphoreType.DMA,
                            pltpu.SemaphoreType.DMA]),
        out_shape=jax.ShapeDtypeStruct((n_rows, w), xp.dtype),
        compiler_params=_params("arbitrary"),
        name="dispatch_rows",
    )(grp_start, grp_blocks, dest, xp)


def _weight_stage(w_hbms, stages, wsem, tn):
    n, e = pl.program_id(0), pl.program_id(1)
    n_tiles, n_exp = pl.num_programs(0), pl.num_programs(1)

    def copies(nn, ee):
        cols = pl.ds(pl.multiple_of(nn * tn, tn), tn)
        return [pltpu.make_async_copy(w.at[ee, :, cols], st, wsem.at[k])
                for k, (w, st) in enumerate(zip(w_hbms, stages))]

    @pl.when((n == 0) & (e == 0))
    def _():
        for c in copies(0, 0):
            c.start(priority=1)

    def wait_current():
        for c in copies(n, e):
            c.wait()

    def start_next():
        wrap = e == n_exp - 1

        @pl.when(jnp.logical_not(wrap & (n == n_tiles - 1)))
        def _():
            for c in copies(jnp.where(wrap, n + 1, n), jnp.where(wrap, 0, e + 1)):
                c.start(priority=1)

    return wait_current, start_next


def _group_loop(gs_ref, gn_ref, src_hbm, dst_hbm, col0, in_buf, out_buf, sem_in, sem_out, compute, after_first_fetch):
    e = pl.program_id(1)
    nb = gn_ref[e]
    b0 = gs_ref[e]
    width = out_buf.shape[2]
    cb = in_buf.shape[1] // EXPERT_BLOCK
    nch = lax.div(nb + (cb - 1), jnp.int32(cb))

    def rows(i, nblk):
        return pl.ds(pl.multiple_of((b0 + i * cb) * EXPERT_BLOCK, EXPERT_BLOCK), nblk * EXPERT_BLOCK)

    def fetch(i, slot):
        return pltpu.make_async_copy(src_hbm.at[rows(i, cb), :], in_buf.at[slot], sem_in.at[slot])

    def store(i, slot, nblk):
        return pltpu.make_async_copy(out_buf.at[slot, pl.ds(0, nblk * EXPERT_BLOCK), :],
                                     dst_hbm.at[rows(i, nblk), pl.ds(col0, width)], sem_out.at[slot])

    def blocks_in(i):
        return jnp.minimum(nb - i * cb, cb)

    def for_size(count, fn):
        for size in range(1, cb + 1):
            @pl.when(count == size)
            def _():
                fn(size)

    @pl.when(e == 0)
    def _():
        last = gs_ref.shape[0] - 1
        used = gs_ref[last] + gn_ref[last]
        total = dst_hbm.shape[0] // EXPERT_BLOCK
        out_buf[0, 0:EXPERT_BLOCK, :] = jnp.zeros((EXPERT_BLOCK, width), out_buf.dtype)

        def tail(m):
            blk = pl.ds(pl.multiple_of(m * EXPERT_BLOCK, EXPERT_BLOCK), EXPERT_BLOCK)
            return pltpu.make_async_copy(out_buf.at[0, pl.ds(0, EXPERT_BLOCK), :],
                                         dst_hbm.at[blk, pl.ds(col0, width)], sem_out.at[0])

        lax.fori_loop(used, total, lambda m, _: (tail(m).start(), 0)[1], 0)
        lax.fori_loop(used, total, lambda m, _: (tail(m).wait(), 0)[1], 0)

    @pl.when(nch > 0)
    def _():
        fetch(0, 0).start()

    after_first_fetch()

    def step(i, _):
        slot = i & 1
        fetch(i, slot).wait()

        @pl.when(i + 1 < nch)
        def _():
            fetch(i + 1, 1 - slot).start()

        @pl.when(i >= 2)
        def _():
            store(i - 2, slot, cb).wait()

        def run(size):
            n = size * EXPERT_BLOCK
            out_buf[slot, 0:n, :] = compute(in_buf[slot, 0:n, :])
            store(i, slot, size).start()

        for_size(blocks_in(i), run)
        return 0

    lax.fori_loop(0, nch, step, 0)

    @pl.when(nch >= 2)
    def _():
        store(nch - 2, nch & 1, cb).wait()

    @pl.when(nch >= 1)
    def _():
        for_size(blocks_in(nch - 1), lambda size: store(nch - 1, (nch - 1) & 1, size).wait())


def _gateup_kernel(gs_ref, gn_ref, xs_hbm, wg_hbm, wu_hbm, bg_ref, bu_ref, h_hbm,
                   wg_st, wu_st, wgb_ref, wub_ref, xbuf, hbuf, wsem, sem_in, sem_out):
    tn = wgb_ref.shape[1]
    half = xbuf.shape[2]
    wait_weights, start_next_weights = _weight_stage((wg_hbm, wu_hbm), (wg_st, wu_st), wsem, tn)

    def stage_weights():
        wait_weights()

        @pl.when(gn_ref[pl.program_id(1)] > 0)
        def _():
            wgb_ref[...] = wg_st[...].astype(BF16)
            wub_ref[...] = wu_st[...].astype(BF16)

        start_next_weights()

    def compute(xp):
        xa = pltpu.bitcast(xp << 16, F32).astype(BF16)
        xb = pltpu.bitcast(xp & jnp.uint32(0xFFFF0000), F32).astype(BF16)
        gte = _dot(xa, wgb_ref[0:half, :]) + _dot(xb, wgb_ref[half:, :]) + bg_ref[...]
        up = _dot(xa, wub_ref[0:half, :]) + _dot(xb, wub_ref[half:, :]) + bu_ref[...]
        gte = jnp.minimum(gte, SWIGLU_LIMIT)
        up = jnp.clip(up, -SWIGLU_LIMIT, SWIGLU_LIMIT)
        return (gte * jax.nn.sigmoid(SWIGLU_ALPHA * gte) * (up + 1.0)).astype(BF16)

    col0 = pl.multiple_of(pl.program_id(0) * tn, tn)
    _group_loop(gs_ref, gn_ref, xs_hbm, h_hbm, col0, xbuf, hbuf, sem_in, sem_out, compute, stage_weights)


def _gateup(grp_start, grp_blocks, xs, w_gate, w_up, b_gate, b_up, tn):
    n_rows, half = xs.shape
    n_exp, d, f = w_gate.shape
    chunk = CHUNK_BLOCKS * EXPERT_BLOCK
    any_spec = pl.BlockSpec(memory_space=pl.ANY)
    b_spec = pl.BlockSpec((None, 1, tn), lambda n, e, gs, gn: (e, 0, n))
    return pl.pallas_call(
        _gateup_kernel,
        grid_spec=pltpu.PrefetchScalarGridSpec(
            num_scalar_prefetch=2,
            grid=(f // tn, n_exp),
            in_specs=[any_spec, any_spec, any_spec, b_spec, b_spec],
            out_specs=any_spec,
            scratch_shapes=[pltpu.VMEM((d, tn), F32), pltpu.VMEM((d, tn), F32),
                            pltpu.VMEM((d, tn), BF16), pltpu.VMEM((d, tn), BF16),
                            pltpu.VMEM((2, chunk, half), xs.dtype), pltpu.VMEM((2, chunk, tn), BF16),
                            pltpu.SemaphoreType.DMA((2,)), pltpu.SemaphoreType.DMA((2,)),
                            pltpu.SemaphoreType.DMA((2,))]),
        out_shape=jax.ShapeDtypeStruct((n_rows, f), BF16),
        compiler_params=_params("arbitrary", "arbitrary"),
        name="expert_gate_up",
    )(grp_start, grp_blocks, xs, w_gate, w_up, b_gate, b_up)


def _down_kernel(gs_ref, gn_ref, h_hbm, wd_hbm, bd_ref, y_hbm, wd_st, wdb_ref, hbuf, ybuf, wsem, sem_in, sem_out):
    tn = wdb_ref.shape[1]
    wait_weights, start_next_weights = _weight_stage((wd_hbm,), (wd_st,), wsem, tn)

    def stage_weights():
        wait_weights()

        @pl.when(gn_ref[pl.program_id(1)] > 0)
        def _():
            wdb_ref[...] = wd_st[...].astype(BF16)

        start_next_weights()

    def compute(h):
        y = (_dot(h, wdb_ref[...]) + bd_ref[...]).astype(BF16).astype(F32)
        bits = pltpu.bitcast(y, jnp.uint32)
        return (bits[:, :tn // 2] >> 16) | bits[:, tn // 2:]

    col0 = pl.multiple_of(pl.program_id(0) * (tn // 2), tn // 2)
    _group_loop(gs_ref, gn_ref, h_hbm, y_hbm, col0, hbuf, ybuf, sem_in, sem_out, compute, stage_weights)


def _down(grp_start, grp_blocks, h, w_down, b_down, tn):
    n_rows, f = h.shape
    n_exp, _, d = w_down.shape
    return pl.pallas_call(
        _down_kernel,
        grid_spec=pltpu.PrefetchScalarGridSpec(
            num_scalar_prefetch=2,
            grid=(d // tn, n_exp),
            in_specs=[pl.BlockSpec(memory_space=pl.ANY),
                      pl.BlockSpec(memory_space=pl.ANY),
                      pl.BlockSpec((None, 1, tn), lambda n, e, gs, gn: (e, 0, n))],
            out_specs=pl.BlockSpec(memory_space=pl.ANY),
            scratch_shapes=[pltpu.VMEM((f, tn), F32), pltpu.VMEM((f, tn), BF16),
                            pltpu.VMEM((2, CHUNK_BLOCKS * EXPERT_BLOCK, f), h.dtype),
                            pltpu.VMEM((2, CHUNK_BLOCKS * EXPERT_BLOCK, tn // 2), jnp.uint32),
                            pltpu.SemaphoreType.DMA((1,)), pltpu.SemaphoreType.DMA((2,)),
                            pltpu.SemaphoreType.DMA((2,))]),
        out_shape=jax.ShapeDtypeStruct((n_rows, d // 2), jnp.uint32),
        compiler_params=_params("arbitrary", "arbitrary"),
        name="expert_down",
    )(grp_start, grp_blocks, h, w_down, b_down)


def _combine_kernel(dest_ref, dest_next_ref, wt_ref, h_ref, y_hbm, o_ref, buf_a, buf_b, wb_ref, sem, *, tile_cols):
    i = pl.program_id(0)
    n_cols = y_hbm.shape[1]
    half = tile_cols // 2
    n_chunks = n_cols // LANES
    rows_per_chunk = LANES // n_chunks

    def row_copy(idx_ref, t, buf, s, r, k):
        return pltpu.make_async_copy(y_hbm.at[pl.ds(idx_ref[t, k, r], 1), :], buf.at[k, pl.ds(r, 1), :], sem.at[s])

    def drain(idx_ref, t, buf, s):
        def wait(r, _):
            for k in range(TOP_K):
                row_copy(idx_ref, t, buf, s, r, k).wait()
            return 0

        lax.fori_loop(0, LANES, wait, 0)

    def combine_tile(t, buf, issue_next):
        rows = slice(t * LANES, (t + 1) * LANES)
        wt = wt_ref[:, rows]
        for k in range(TOP_K):
            wb_ref[k] = jnp.broadcast_to(wt[k:k + 1, :], (LANES, LANES)).T
        for c in range(n_chunks):
            col = (c * LANES // half) * tile_cols + (c * LANES) % half
            lo = h_ref[rows, col:col + LANES]
            hi = h_ref[rows, col + half:col + half + LANES]
            for k in range(TOP_K):
                u = buf[k, :, c * LANES:(c + 1) * LANES]
                lo = lo + wb_ref[k] * pltpu.bitcast(u << 16, F32)
                hi = hi + wb_ref[k] * pltpu.bitcast(u & jnp.uint32(0xFFFF0000), F32)
            o_ref[rows, col:col + LANES] = lo
            o_ref[rows, col + half:col + half + LANES] = hi
            for r in range(c * rows_per_chunk, (c + 1) * rows_per_chunk):
                for k in range(TOP_K):
                    issue_next(r, k)

    @pl.when(i == 0)
    def _():
        def issue(r, _):
            for k in range(TOP_K):
                row_copy(dest_ref, 0, buf_a, 0, r, k).start(priority=k % 2)
            return 0

        lax.fori_loop(0, LANES, issue, 0)

    drain(dest_ref, 0, buf_a, 0)
    combine_tile(0, buf_a, lambda r, k: row_copy(dest_ref, 1, buf_b, 1, r, k).start(priority=k % 2))
    drain(dest_ref, 1, buf_b, 1)
    combine_tile(1, buf_b, lambda r, k: row_copy(dest_next_ref, 0, buf_a, 0, r, k).start(priority=k % 2))

    @pl.when(i == pl.num_programs(0) - 1)
    def _():
        drain(dest_next_ref, 0, buf_a, 0)


def _combine(dest, wts, h1, y, tile_cols):
    t, d = h1.shape
    nt = t // LANES
    assert nt % 2 == 0 and LANES % (y.shape[1] // LANES) == 0
    smem = pltpu.SMEM
    return pl.pallas_call(
        functools.partial(_combine_kernel, tile_cols=tile_cols),
        grid=(nt // 2,),
        in_specs=[pl.BlockSpec((2, TOP_K, LANES), lambda i: (i, 0, 0), memory_space=smem),
                  pl.BlockSpec((1, TOP_K, LANES), lambda i: (jnp.minimum(2 * i + 2, nt - 1), 0, 0), memory_space=smem),
                  pl.BlockSpec((TOP_K, 2 * LANES), lambda i: (0, i)),
                  pl.BlockSpec((2 * LANES, d), lambda i: (i, 0)),
                  pl.BlockSpec(memory_space=pl.ANY)],
        out_specs=pl.BlockSpec((2 * LANES, d), lambda i: (i, 0)),
        out_shape=jax.ShapeDtypeStruct((t, d), F32),
        scratch_shapes=[pltpu.VMEM((TOP_K, LANES, y.shape[1]), y.dtype),
                        pltpu.VMEM((TOP_K, LANES, y.shape[1]), y.dtype),
                        pltpu.VMEM((TOP_K, LANES, LANES), F32),
                        pltpu.SemaphoreType.DMA((2,))],
        compiler_params=_params("arbitrary"),
        name="combine_rows",
    )(dest, dest, wts, h1, y)


def _layer(x, meta, norm_mix, w_in, b_forget, b_branch_gate, q_norm, k_norm, w_pool_group, pool_scale,
           w_branch_pool, w_branch_attn, w_out, norm_ffn, w_router, b_router, w_gate, b_gate, w_up, b_up,
           w_down, b_down):
    batch, seq, d = x.shape
    t = batch * seq
    n_meta = meta.shape[0]
    pool_w = w_pool_group.shape[0] * w_pool_group.shape[1]
    attn_w = w_branch_attn.shape[0]
    n_heads = attn_w // HEAD_DIM
    n_exp = w_router.shape[1]
    col_q = pool_w
    col_zf = pool_w + 3 * attn_w
    col_zg = col_zf + n_heads
    assert n_heads <= LANES and col_zf % LANES == 0

    tm = min(1024, t)
    x2 = x.reshape(t, d)

    hn = _rmsnorm(x2, norm_mix.reshape(1, d), min(256, t))
    hn_meta = _rmsnorm(meta, norm_mix.reshape(1, d), n_meta)

    w_in_t = w_in.T
    uqkv, uqkv_meta = _proj(hn, hn_meta, w_in_t, None, col0=0, ncols=col_zf, tn=512, tm=tm,
                            out_dtype=BF16, name="in_proj_uqkv")
    zf, zf_meta = _proj(hn, hn_meta, w_in_t, None, col0=col_zf, ncols=LANES, tn=LANES, tm=tm,
                        out_dtype=F32, name="in_proj_forget")
    gates = _proj(hn, None, w_in_t, b_branch_gate.reshape(1, 2 * d), col0=col_zf, ncols=2 * d, tn=512, tm=tm,
                  out_dtype=BF16, name="in_proj_gates", shift=col_zg - col_zf)

    bf_pad = jnp.pad(b_forget.reshape(1, n_heads), ((0, 0), (0, LANES - n_heads)))
    zfm_pad = jnp.pad(zf_meta, ((0, LANES - n_meta), (0, 0)))
    cr, mbias = _cumsum(zf, zfm_pad, bf_pad, batch, n_heads, n_meta)

    y_pool = _pool(uqkv, uqkv_meta, w_pool_group, pool_scale.reshape(1, pool_w), batch, min(512, seq))
    y_attn = _attention(uqkv, uqkv_meta, cr, mbias, q_norm.reshape(1, HEAD_DIM), k_norm.reshape(1, HEAD_DIM),
                        batch, n_heads, col_q, min(512, seq))

    merged = _merge(y_pool, y_attn, w_branch_pool, w_branch_attn, gates, tm, 512)
    h1 = _outproj(merged, w_out, x2, tm, 512)

    xp, idx, wts = _router(h1, norm_ffn.reshape(1, d), w_router.T, b_router.reshape(n_exp, 1), min(256, t))
    n_blocks = -(-(t * TOP_K + n_exp * (EXPERT_BLOCK - 1)) // EXPERT_BLOCK) + CHUNK_BLOCKS - 1
    dest, grp_start, grp_blocks = _plan(idx, n_exp)
    xs = _dispatch(grp_start, grp_blocks, dest, xp, n_blocks * EXPERT_BLOCK)
    f = w_gate.shape[2]
    hmid = _gateup(grp_start, grp_blocks, xs, w_gate, w_up, b_gate.reshape(n_exp, 1, f), b_up.reshape(n_exp, 1, f), 512)
    down_tn = 2048
    y = _down(grp_start, grp_blocks, hmid, w_down, b_down.reshape(n_exp, 1, d), down_tn)
    out = _combine(dest, wts, h1, y, down_tn)
    return out.reshape(batch, seq, d)


def kernel(x, meta_tokens, norm_mix, w_in, b_forget, b_branch_gate, q_norm, k_norm, w_pool_group, pool_scale,
           w_branch_pool, w_branch_attn, w_out, norm_ffn, w_router, b_router, w_gate, b_gate, w_up, b_up,
           w_down, b_down):
    depth = norm_mix.shape[0]
    assert depth == 1, "the fused layer pipeline drops the meta rows after the (single) layer"
    return _layer(x, meta_tokens, norm_mix[0], w_in[0], b_forget[0], b_branch_gate[0], q_norm[0], k_norm[0],
                  w_pool_group[0], pool_scale[0], w_branch_pool[0], w_branch_attn[0], w_out[0], norm_ffn[0],
                  w_router[0], b_router[0], w_gate[0], b_gate[0], w_up[0], b_up[0], w_down[0], b_down[0])
```

```python
import functools

import jax
import jax.numpy as jnp
from jax import lax
from jax.experimental import pallas as pl
from jax.experimental.pallas import tpu as pltpu

NORM_EPS = 1e-6
HEAD_DIM = 128
POOL_WINDOWS = (2, 4, 8, 16)
TOP_K = 4
SWIGLU_LIMIT = 7.0
SWIGLU_ALPHA = 1.702
EXPERT_BLOCK = 128
CHUNK_BLOCKS = 4
DOWN_CHUNK_BLOCKS = 8
DOWN_STRIP = 256
LANES = 128
NEG_BIG = -1e30
LOG2E = 1.4426950408889634
ONES_ROWS = 16
HEAD_GROUP = 2
VMEM_LIMIT = 56 * 1024 * 1024

F32 = jnp.float32
BF16 = jnp.bfloat16


def _params(*sem, vmem=VMEM_LIMIT):
    return pltpu.CompilerParams(dimension_semantics=sem, vmem_limit_bytes=vmem)


def _dot(a, b):
    return jnp.dot(a, b, preferred_element_type=F32)


def _dot_nt(a, b):
    return lax.dot_general(a, b, (((1,), (1,)), ((), ())), preferred_element_type=F32)


def _rms(x, gain):
    ms = jnp.mean(x * x, axis=-1, keepdims=True)
    return x * lax.rsqrt(ms + NORM_EPS) * gain


def _rmsnorm_kernel(x_ref, g_ref, o_ref):
    o_ref[...] = _rms(x_ref[...], g_ref[...]).astype(o_ref.dtype)


def _rmsnorm(x, gain, tm):
    rows, d = x.shape
    return pl.pallas_call(
        _rmsnorm_kernel,
        grid=(rows // tm,),
        in_specs=[pl.BlockSpec((tm, d), lambda i: (i, 0)), pl.BlockSpec((1, d), lambda i: (0, 0))],
        out_specs=pl.BlockSpec((tm, d), lambda i: (i, 0)),
        out_shape=jax.ShapeDtypeStruct((rows, d), BF16),
        compiler_params=_params("arbitrary"),
        name="rmsnorm",
    )(x, gain)


def _proj_kernel(*refs, has_meta, gate_bias, shift):
    it = iter(refs)
    x_ref = next(it)
    xm_ref = next(it) if has_meta else None
    w_ref = next(it)
    wn_ref = next(it) if shift else None
    b_ref = next(it) if gate_bias else None
    o_ref = next(it)
    om_ref = next(it) if has_meta else None
    wbf_ref = next(it)
    tn, k = w_ref.shape

    @pl.when(pl.program_id(1) == 0)
    def _():
        def chunk(c, _):
            off = pl.multiple_of(c * LANES, LANES)
            if shift:
                wt = jnp.concatenate([w_ref[shift:tn, pl.ds(off, LANES)], wn_ref[:, pl.ds(off, LANES)]], axis=0)
            else:
                wt = w_ref[:, pl.ds(off, LANES)]
            wbf_ref[pl.ds(off, LANES), :] = wt.T.astype(BF16)
            return 0

        lax.fori_loop(0, k // LANES, chunk, 0, unroll=4)
        if has_meta:
            om_ref[...] = _dot(xm_ref[...], wbf_ref[...]).astype(om_ref.dtype)

    acc = _dot(x_ref[...], wbf_ref[...])
    if gate_bias:
        acc = jax.nn.sigmoid(acc + b_ref[...])
    o_ref[...] = acc.astype(o_ref.dtype)


def _proj(x, xm, wt, bias, *, col0, ncols, tn, tm, out_dtype, name, shift=0):
    rows, k = x.shape
    assert col0 % tn == 0 and ncols % tn == 0 and rows % tm == 0
    jb = col0 // tn
    in_specs = [pl.BlockSpec((tm, k), lambda j, i: (i, 0))]
    args = [x]
    if xm is not None:
        in_specs.append(pl.BlockSpec(xm.shape, lambda j, i: (0, 0)))
        args.append(xm)
    in_specs.append(pl.BlockSpec((tn, k), lambda j, i: (jb + j, 0)))
    args.append(wt)
    if shift:
        assert shift % 8 == 0 and tn % shift == 0 and (col0 + ncols + shift) <= wt.shape[0]
        per = tn // shift
        in_specs.append(pl.BlockSpec((shift, k), lambda j, i: ((jb + j + 1) * per, 0)))
        args.append(wt)
    if bias is not None:
        in_specs.append(pl.BlockSpec((1, tn), lambda j, i: (0, j)))
        args.append(bias)
    out_specs = [pl.BlockSpec((tm, tn), lambda j, i: (i, j))]
    out_shape = [jax.ShapeDtypeStruct((rows, ncols), out_dtype)]
    if xm is not None:
        out_specs.append(pl.BlockSpec((xm.shape[0], tn), lambda j, i: (0, j)))
        out_shape.append(jax.ShapeDtypeStruct((xm.shape[0], ncols), out_dtype))
    res = pl.pallas_call(
        functools.partial(_proj_kernel, has_meta=xm is not None, gate_bias=bias is not None, shift=shift),
        grid=(ncols // tn, rows // tm),
        in_specs=in_specs,
        out_specs=out_specs,
        out_shape=out_shape,
        scratch_shapes=[pltpu.VMEM((k, tn), BF16)],
        compiler_params=_params("arbitrary", "arbitrary"),
        name=name,
    )(*args)
    return res if xm is not None else res[0]


def _cumsum_kernel(zf_ref, zfm_ref, bf_ref, cr_ref, mb_ref, *, n_meta):
    seq = zf_ref.shape[0]
    bf = bf_ref[...]
    r = lax.broadcasted_iota(jnp.int32, (LANES, LANES), 0)
    c = lax.broadcasted_iota(jnp.int32, (LANES, LANES), 1)
    tri = (c <= r).astype(F32)
    sfx = (c > r).astype(F32)

    def prefix(mat, x):
        return jnp.dot(mat, x, preferred_element_type=F32, precision=lax.Precision.HIGHEST)

    n_meta_pad = zfm_ref.shape[0]
    lfm = jax.nn.log_sigmoid(zfm_ref[...] + bf)
    rows = lax.broadcasted_iota(jnp.int32, (n_meta_pad, LANES), 0)
    lfm = jnp.where(rows < n_meta, lfm, 0.0)
    mbias = jnp.where(rows < n_meta, prefix(sfx, lfm), NEG_BIG)
    mbias_t = mbias.T
    for h in range(mb_ref.shape[0]):
        mb_ref[h] = mbias_t[h:h + 1, :]

    def body(blk, carry):
        off = pl.multiple_of(blk * LANES, LANES)
        lf = jax.nn.log_sigmoid(zf_ref[pl.ds(off, LANES), :] + bf)
        cs = prefix(tri, lf) + carry
        cs_t = cs.T
        for h in range(cr_ref.shape[0]):
            cr_ref[h, :, pl.ds(off, LANES)] = cs_t[h:h + 1, :]
        return cs[LANES - 1:LANES, :]

    lax.fori_loop(0, seq // LANES, body, jnp.zeros((1, LANES), F32))


def _cumsum(zf, zfm_pad, bf_pad, batch, n_heads, n_meta):
    t = zf.shape[0]
    seq = t // batch
    return pl.pallas_call(
        functools.partial(_cumsum_kernel, n_meta=n_meta),
        grid=(batch,),
        in_specs=[pl.BlockSpec((seq, LANES), lambda b: (b, 0)),
                  pl.BlockSpec(zfm_pad.shape, lambda b: (0, 0)),
                  pl.BlockSpec((1, LANES), lambda b: (0, 0))],
        out_specs=[pl.BlockSpec((None, n_heads, 1, seq), lambda b: (b, 0, 0, 0)),
                   pl.BlockSpec((n_heads, 1, LANES), lambda b: (0, 0, 0))],
        out_shape=[jax.ShapeDtypeStruct((batch, n_heads, 1, seq), F32),
                   jax.ShapeDtypeStruct((n_heads, 1, LANES), F32)],
        compiler_params=_params("arbitrary"),
        name="forget_cumsum",
    )(zf, zfm_pad, bf_pad)


def _pool_kernel(u_ref, um_ref, wp_ref, ps_ref, o_ref, ext_ref, wbf_ref):
    ts = u_ref.shape[0]
    halo = um_ref.shape[0]
    gw = wp_ref.shape[1]
    i = pl.program_id(1)

    @pl.when((pl.program_id(0) == 0) & (i == 0))
    def _():
        wbf_ref[...] = wp_ref[...].astype(BF16)

    @pl.when(i == 0)
    def _():
        ext_ref[0:halo, :] = um_ref[...].astype(F32)

    @pl.when(i > 0)
    def _():
        ext_ref[0:halo, :] = ext_ref[ts:ts + halo, :]

    ext_ref[halo:halo + ts, :] = u_ref[...].astype(F32)
    for g, w in enumerate(POOL_WINDOWS):
        cols = slice(g * gw, (g + 1) * gw)
        uf = ext_ref[halo:halo + ts, cols]
        acc = uf
        for d in range(1, w):
            acc = acc + ext_ref[halo - d:halo - d + ts, cols]
        pooled = acc * (1.0 / w) - uf
        y = _dot(pooled.astype(BF16), wbf_ref[g]) * ps_ref[:, cols]
        o_ref[:, cols] = y.astype(o_ref.dtype)


def _pool(uqkv, uqkv_meta, w_pool, pool_scale, batch, ts):
    t = uqkv.shape[0]
    seq = t // batch
    ng, gw, _ = w_pool.shape
    pw = ng * gw
    halo = uqkv_meta.shape[0]
    assert halo >= max(POOL_WINDOWS) and seq % ts == 0
    nt = seq // ts
    return pl.pallas_call(
        _pool_kernel,
        grid=(batch, nt),
        in_specs=[pl.BlockSpec((ts, pw), lambda b, i: (b * nt + i, 0)),
                  pl.BlockSpec((halo, pw), lambda b, i: (0, 0)),
                  pl.BlockSpec(w_pool.shape, lambda b, i: (0, 0, 0)),
                  pl.BlockSpec((1, pw), lambda b, i: (0, 0))],
        out_specs=pl.BlockSpec((ts, pw), lambda b, i: (b * nt + i, 0)),
        out_shape=jax.ShapeDtypeStruct((t, pw), BF16),
        scratch_shapes=[pltpu.VMEM((halo + ts, pw), F32), pltpu.VMEM(w_pool.shape, BF16)],
        compiler_params=_params("arbitrary", "arbitrary"),
        name="pool_mixer",
    )(uqkv, uqkv_meta, w_pool, pool_scale)


def _attn_kernel(q_ref, k_ref, v_ref, km_ref, vm_ref, cr_ref, mb_ref, qg_ref, kg_ref, o_ref,
                 ka_ref, va_ref, kma_ref, vma_ref):
    tq = q_ref.shape[0]
    seq = k_ref.shape[0]
    n_meta = km_ref.shape[0]
    n_group, va_rows = va_ref.shape[0], va_ref.shape[1]
    qi = pl.program_id(2)
    lane = lax.broadcasted_iota(jnp.int32, (LANES, LANES), 1)
    heads = [(g, slice(g * HEAD_DIM, (g + 1) * HEAD_DIM)) for g in range(n_group)]

    def bias_pieces(col):
        hi = col.astype(BF16).astype(F32)
        mid = (col - hi).astype(BF16).astype(F32)
        lo = (col - hi - mid).astype(BF16).astype(F32)
        return jnp.where(lane == 0, hi, jnp.where(lane == 1, mid, jnp.where(lane == 2, lo, 0.0))).astype(BF16)

    ones_row = (lax.broadcasted_iota(jnp.int32, (va_rows - HEAD_DIM, LANES), 0) == 0).astype(F32)

    @pl.when(qi == 0)
    def _():
        def chunk(i, _):
            off = pl.multiple_of(i * LANES, LANES)
            for g, cols in heads:
                vt = v_ref[pl.ds(off, LANES), cols].astype(F32).T
                va_ref[g, :, pl.ds(off, LANES)] = jnp.concatenate([vt, ones_row], axis=0).astype(BF16)
                col = jnp.broadcast_to(cr_ref[g, :, pl.ds(off, LANES)] * LOG2E, (LANES, LANES)).T
                ka_ref[g, pl.ds(off, LANES), HEAD_DIM:] = bias_pieces(col)
            return 0

        lax.fori_loop(0, seq // LANES, chunk, 0, unroll=2)
        pad = jnp.zeros((LANES - n_meta, HEAD_DIM), F32)
        for g, cols in heads:
            ka_ref[g, :, 0:HEAD_DIM] = _rms(k_ref[:, cols].astype(F32), kg_ref[...]).astype(BF16)
            kmn = jnp.concatenate([_rms(km_ref[:, cols].astype(F32), kg_ref[...]), pad], axis=0)
            kma_ref[g, :, 0:HEAD_DIM] = kmn.astype(BF16)
            kma_ref[g, :, HEAD_DIM:] = bias_pieces(-jnp.broadcast_to(mb_ref[g] * LOG2E, (LANES, LANES)).T)
            vmt = jnp.concatenate([vm_ref[:, cols].astype(F32), pad], axis=0).T
            vma_ref[g] = jnp.concatenate([vmt, ones_row], axis=0).astype(BF16)

    minus_ones = jnp.where(lax.broadcasted_iota(jnp.int32, (tq, LANES), 1) < 3, -1.0, 0.0).astype(BF16)
    qa = [jnp.concatenate([(_rms(q_ref[:, cols].astype(F32), qg_ref[...]) * (HEAD_DIM ** -0.5 * LOG2E)).astype(BF16),
                           minus_ones], axis=1) for g, cols in heads]
    q0 = pl.multiple_of(qi * tq, tq)

    def update(carry, st, va_blk):
        m, acc = carry
        m_new = jnp.maximum(m, jnp.max(st, axis=0, keepdims=True))
        p = jnp.exp2(st - m_new)
        acc = jnp.exp2(m - m_new) * acc + _dot(va_blk, p.astype(BF16))
        return m_new, acc

    def scores(g, off, size):
        return _dot_nt(ka_ref[g, pl.ds(off, size), :], qa[g])

    def body(j, carry):
        off = pl.multiple_of(j * (2 * tq), 2 * tq)
        return tuple(update(carry[g], scores(g, off, 2 * tq), va_ref[g, :, pl.ds(off, 2 * tq)]) for g, _ in heads)

    carry = tuple((jnp.full((1, tq), NEG_BIG, F32), jnp.zeros((va_rows, tq), F32)) for _ in heads)
    carry = lax.fori_loop(0, lax.shift_right_logical(qi, 1), body, carry)

    def finish(with_prev):
        key = lax.broadcasted_iota(jnp.int32, (tq, tq), 0)
        qry = lax.broadcasted_iota(jnp.int32, (tq, tq), 1)
        for g, cols in heads:
            sts = [_dot_nt(kma_ref[g], qa[g])]
            vas = [vma_ref[g]]
            if with_prev:
                prev = pl.multiple_of(q0 - tq, tq)
                sts.append(scores(g, prev, tq))
                vas.append(va_ref[g, :, pl.ds(prev, tq)])
            sts.append(jnp.where(key <= qry, scores(g, q0, tq), NEG_BIG))
            vas.append(va_ref[g, :, pl.ds(q0, tq)])
            m, acc = update(carry[g], jnp.concatenate(sts, axis=0), jnp.concatenate(vas, axis=1))
            o_ref[:, cols] = (acc[0:HEAD_DIM, :] / acc[HEAD_DIM:HEAD_DIM + 1, :]).T.astype(o_ref.dtype)

    odd = (qi & 1) == 1

    @pl.when(odd)
    def _():
        finish(True)

    @pl.when(jnp.logical_not(odd))
    def _():
        finish(False)


def _attention(uqkv, uqkv_meta, cr, mbias, q_gain, k_gain, batch, n_heads, col_q, tq):
    t = uqkv.shape[0]
    seq = t // batch
    n_meta = uqkv_meta.shape[0]
    nq = seq // tq
    grp = HEAD_GROUP
    gw = grp * HEAD_DIM
    assert n_heads % grp == 0 and col_q % gw == 0 and (n_heads * HEAD_DIM) % gw == 0
    bq = col_q // gw
    bk = bq + n_heads // grp
    bv = bk + n_heads // grp
    return pl.pallas_call(
        _attn_kernel,
        grid=(batch, n_heads // grp, nq),
        in_specs=[pl.BlockSpec((tq, gw), lambda b, h, i: (b * nq + i, bq + h)),
                  pl.BlockSpec((seq, gw), lambda b, h, i: (b, bk + h)),
                  pl.BlockSpec((seq, gw), lambda b, h, i: (b, bv + h)),
                  pl.BlockSpec((n_meta, gw), lambda b, h, i: (0, bk + h)),
                  pl.BlockSpec((n_meta, gw), lambda b, h, i: (0, bv + h)),
                  pl.BlockSpec((None, grp, 1, seq), lambda b, h, i: (b, h, 0, 0)),
                  pl.BlockSpec((grp, 1, LANES), lambda b, h, i: (h, 0, 0)),
                  pl.BlockSpec((1, HEAD_DIM), lambda b, h, i: (0, 0)),
                  pl.BlockSpec((1, HEAD_DIM), lambda b, h, i: (0, 0))],
        out_specs=pl.BlockSpec((tq, gw), lambda b, h, i: (b * nq + i, h)),
        out_shape=jax.ShapeDtypeStruct((t, n_heads * HEAD_DIM), BF16),
        scratch_shapes=[pltpu.VMEM((grp, seq, HEAD_DIM + LANES), BF16),
                        pltpu.VMEM((grp, HEAD_DIM + ONES_ROWS, seq), BF16),
                        pltpu.VMEM((grp, LANES, HEAD_DIM + LANES), BF16),
                        pltpu.VMEM((grp, HEAD_DIM + ONES_ROWS, LANES), BF16)],
        compiler_params=_params("arbitrary", "arbitrary", "arbitrary"),
        name="forgetting_attention",
    )(uqkv, uqkv, uqkv, uqkv_meta, uqkv_meta, cr, mbias, q_gain, k_gain)


def _merge_kernel(yp_ref, ya_ref, wp_ref, wa_ref, gp_ref, ga_ref, o_ref, wpb_ref, wab_ref):
    @pl.when(pl.program_id(1) == 0)
    def _():
        wpb_ref[...] = wp_ref[...].astype(BF16)
        wab_ref[...] = wa_ref[...].astype(BF16)

    a = _dot(yp_ref[...], wpb_ref[...])
    b = _dot(ya_ref[...], wab_ref[...])
    o_ref[...] = (gp_ref[...].astype(F32) * a + ga_ref[...].astype(F32) * b).astype(o_ref.dtype)


def _merge(y_pool, y_attn, w_bp, w_ba, gates, tm, tn):
    t, kp = y_pool.shape
    ka = y_attn.shape[1]
    d = w_bp.shape[1]
    nj = d // tn
    return pl.pallas_call(
        _merge_kernel,
        grid=(nj, t // tm),
        in_specs=[pl.BlockSpec((tm, kp), lambda j, i: (i, 0)),
                  pl.BlockSpec((tm, ka), lambda j, i: (i, 0)),
                  pl.BlockSpec((kp, tn), lambda j, i: (0, j)),
                  pl.BlockSpec((ka, tn), lambda j, i: (0, j)),
                  pl.BlockSpec((tm, tn), lambda j, i: (i, j)),
                  pl.BlockSpec((tm, tn), lambda j, i: (i, nj + j))],
        out_specs=pl.BlockSpec((tm, tn), lambda j, i: (i, j)),
        out_shape=jax.ShapeDtypeStruct((t, d), BF16),
        scratch_shapes=[pltpu.VMEM((kp, tn), BF16), pltpu.VMEM((ka, tn), BF16)],
        compiler_params=_params("arbitrary", "arbitrary"),
        name="branch_merge",
    )(y_pool, y_attn, w_bp, w_ba, gates, gates)


def _outproj_kernel(m_ref, w_ref, x_ref, o_ref, wbf_ref):
    @pl.when(pl.program_id(1) == 0)
    def _():
        wbf_ref[...] = w_ref[...].astype(BF16)

    o_ref[...] = x_ref[...] + _dot(m_ref[...], wbf_ref[...])


def _outproj(merged, w_out, x, tm, tn):
    t, k = merged.shape
    d = w_out.shape[1]
    return pl.pallas_call(
        _outproj_kernel,
        grid=(d // tn, t // tm),
        in_specs=[pl.BlockSpec((tm, k), lambda j, i: (i, 0)),
                  pl.BlockSpec((k, tn), lambda j, i: (0, j)),
                  pl.BlockSpec((tm, tn), lambda j, i: (i, j))],
        out_specs=pl.BlockSpec((tm, tn), lambda j, i: (i, j)),
        out_shape=jax.ShapeDtypeStruct((t, d), F32),
        scratch_shapes=[pltpu.VMEM((k, tn), BF16)],
        compiler_params=_params("arbitrary", "arbitrary"),
        name="out_proj",
    )(merged, w_out, x)


def _router_kernel(h_ref, g_ref, wr_ref, br_ref, xp_ref, idx_ref, wt_ref):
    y = _rms(h_ref[...], g_ref[...])
    y_hi = y.astype(BF16)
    y_lo = (y - y_hi.astype(F32)).astype(BF16)
    w = wr_ref[...]
    w_hi = w.astype(BF16)
    w_lo = (w - w_hi.astype(F32)).astype(BF16)
    logits = _dot_nt(w_hi, y_hi) + (_dot_nt(w_hi, y_lo) + _dot_nt(w_lo, y_hi)) + br_ref[...]

    half = y.shape[1] // 2
    bits = pltpu.bitcast(y_hi.astype(F32), jnp.uint32)
    xp_ref[...] = (bits[:, :half] >> 16) | bits[:, half:]

    n_exp = logits.shape[0]
    eid = lax.broadcasted_iota(jnp.int32, logits.shape, 0)
    vals = logits
    top_v, top_i = [], []
    for _ in range(TOP_K):
        mx = jnp.max(vals, axis=0, keepdims=True)
        sel = jnp.min(jnp.where(vals == mx, eid, n_exp), axis=0, keepdims=True)
        top_v.append(mx)
        top_i.append(sel)
        vals = jnp.where(eid == sel, -jnp.inf, vals)
    ex = [jnp.exp(v - top_v[0]) for v in top_v]
    den = ex[0] + ex[1] + ex[2] + ex[3]
    for k in range(TOP_K):
        idx_ref[k:k + 1, :] = top_i[k]
        wt_ref[k:k + 1, :] = ex[k] / den


def _router(h1, gain, w_router_t, b_router, tm):
    t, d = h1.shape
    n_exp = w_router_t.shape[0]
    return pl.pallas_call(
        _router_kernel,
        grid=(t // tm,),
        in_specs=[pl.BlockSpec((tm, d), lambda i: (i, 0)),
                  pl.BlockSpec((1, d), lambda i: (0, 0)),
                  pl.BlockSpec((n_exp, d), lambda i: (0, 0)),
                  pl.BlockSpec((n_exp, 1), lambda i: (0, 0))],
        out_specs=[pl.BlockSpec((tm, d // 2), lambda i: (i, 0)),
                   pl.BlockSpec((TOP_K, tm), lambda i: (0, i)),
                   pl.BlockSpec((TOP_K, tm), lambda i: (0, i))],
        out_shape=[jax.ShapeDtypeStruct((t, d // 2), jnp.uint32),
                   jax.ShapeDtypeStruct((TOP_K, t), jnp.int32),
                   jax.ShapeDtypeStruct((TOP_K, t), F32)],
        compiler_params=_params("arbitrary"),
        name="router_topk",
    )(h1, gain, w_router_t, b_router)


def _plan_kernel(idx_ref, dest_ref, grp_ref, rank_ref, *, n_exp):
    t = idx_ref.shape[1]
    nblk = t // LANES
    eid = lax.broadcasted_iota(jnp.int32, (n_exp, LANES), 0)
    r = lax.broadcasted_iota(jnp.int32, (LANES, LANES), 0)
    c = lax.broadcasted_iota(jnp.int32, (LANES, LANES), 1)
    upper = (r <= c).astype(BF16)

    def member(off):
        m = jnp.zeros((n_exp, LANES), F32)
        for k in range(TOP_K):
            m = m + (idx_ref[k:k + 1, pl.ds(off, LANES)] == eid).astype(F32)
        return m

    def count(blk, carry):
        off = pl.multiple_of(blk * LANES, LANES)
        m = member(off)
        cs = _dot(m.astype(BF16), upper) + carry
        rank_ref[:, pl.ds(off, LANES)] = cs - m
        return carry + jnp.sum(m, axis=1, keepdims=True)

    counts = lax.fori_loop(0, nblk, count, jnp.zeros((n_exp, 1), F32))
    padded = jnp.floor((counts + (EXPERT_BLOCK - 1)) * (1.0 / EXPERT_BLOCK)) * EXPERT_BLOCK
    er = lax.broadcasted_iota(jnp.int32, (n_exp, n_exp), 0)
    ec = lax.broadcasted_iota(jnp.int32, (n_exp, n_exp), 1)
    lower = (ec <= er).astype(BF16)
    pad_end = _dot(lower, jnp.broadcast_to(padded, (n_exp, LANES)).astype(BF16))[:, 0:1]
    pad_start = pad_end - padded

    def place(blk, _):
        off = pl.multiple_of(blk * LANES, LANES)
        pos = pad_start + rank_ref[:, pl.ds(off, LANES)]
        for k in range(TOP_K):
            sel = idx_ref[k:k + 1, pl.ds(off, LANES)] == eid
            d = jnp.sum(jnp.where(sel, pos, 0.0), axis=0, keepdims=True)
            dest_ref[blk, k:k + 1, :] = d.astype(jnp.int32)
        return 0

    lax.fori_loop(0, nblk, place, 0)

    lane = lax.broadcasted_iota(jnp.int32, (n_exp, LANES), 1)
    first = pad_start * (1.0 / EXPERT_BLOCK)
    nblocks = padded * (1.0 / EXPERT_BLOCK)
    grp_ref[...] = jnp.where(lane == 0, first, jnp.where(lane == 1, nblocks, 0.0)).astype(jnp.int32)


def _plan(idx, n_exp):
    t = idx.shape[1]
    assert t <= 256 * EXPERT_BLOCK
    dest, grp = pl.pallas_call(
        functools.partial(_plan_kernel, n_exp=n_exp),
        grid=(1,),
        in_specs=[pl.BlockSpec(idx.shape, lambda i: (0, 0))],
        out_specs=[pl.BlockSpec((t // LANES, TOP_K, LANES), lambda i: (0, 0, 0)),
                   pl.BlockSpec((n_exp, LANES), lambda i: (0, 0))],
        out_shape=[jax.ShapeDtypeStruct((t // LANES, TOP_K, LANES), jnp.int32),
                   jax.ShapeDtypeStruct((n_exp, LANES), jnp.int32)],
        scratch_shapes=[pltpu.VMEM((n_exp, t), F32)],
        compiler_params=_params("arbitrary"),
        name="dispatch_plan",
    )(idx)
    return dest, grp[:, 0], grp[:, 1]


def _dispatch_kernel(gs_ref, gn_ref, dest_ref, xp_ref, xs_hbm, zero_ref, sem, zsem):
    n_exp = gs_ref.shape[0]

    def zero_block(m):
        return pltpu.make_async_copy(zero_ref, xs_hbm.at[pl.ds(m * EXPERT_BLOCK, EXPERT_BLOCK), :], zsem)

    def zero_copy(e):
        return zero_block(gs_ref[e] + gn_ref[e] - 1)

    @pl.when(pl.program_id(0) == 0)
    def _():
        zero_ref[...] = jnp.zeros_like(zero_ref)
        used = gs_ref[n_exp - 1] + gn_ref[n_exp - 1]
        total = xs_hbm.shape[0] // EXPERT_BLOCK

        def start(e, _):
            @pl.when(gn_ref[e] > 0)
            def _():
                zero_copy(e).start()
            return 0

        def wait(e, _):
            @pl.when(gn_ref[e] > 0)
            def _():
                zero_copy(e).wait()
            return 0

        lax.fori_loop(0, n_exp, start, 0)
        lax.fori_loop(used, total, lambda m, _: (zero_block(m).start(), 0)[1], 0)
        lax.fori_loop(0, n_exp, wait, 0)
        lax.fori_loop(used, total, lambda m, _: (zero_block(m).wait(), 0)[1], 0)

    def row_copy(r, k):
        return pltpu.make_async_copy(xp_ref.at[pl.ds(r, 1), :], xs_hbm.at[pl.ds(dest_ref[k, r], 1), :], sem)

    def issue(r, _):
        for k in range(TOP_K):
            row_copy(r, k).start(priority=k % 2)
        return 0

    def drain(r, _):
        for k in range(TOP_K):
            row_copy(r, k).wait()
        return 0

    lax.fori_loop(0, LANES, issue, 0, unroll=8)
    lax.fori_loop(0, LANES, drain, 0)


def _dispatch(grp_start, grp_blocks, dest, xp, n_rows):
    t, w = xp.shape
    return pl.pallas_call(
        _dispatch_kernel,
        grid_spec=pltpu.PrefetchScalarGridSpec(
            num_scalar_prefetch=2,
            grid=(t // LANES,),
            in_specs=[pl.BlockSpec((None, TOP_K, LANES), lambda i, gs, gn: (i, 0, 0), memory_space=pltpu.SMEM),
                      pl.BlockSpec((LANES, w), lambda i, gs, gn: (i, 0))],
            out_specs=pl.BlockSpec(memory_space=pl.ANY),
            scratch_shapes=[pltpu.VMEM((EXPERT_BLOCK, w), xp.dtype), pltpu.SemaphoreType.DMA,
                            pltpu.SemaphoreType.DMA]),
        out_shape=jax.ShapeDtypeStruct((n_rows, w), xp.dtype),
        compiler_params=_params("arbitrary"),
        name="dispatch_rows",
    )(grp_start, grp_blocks, dest, xp)


def _weight_stage(w_hbms, stages, wsem, tn):
    n, e = pl.program_id(0), pl.program_id(1)
    n_tiles, n_exp = pl.num_programs(0), pl.num_programs(1)

    def copies(nn, ee):
        cols = pl.ds(pl.multiple_of(nn * tn, tn), tn)
        return [pltpu.make_async_copy(w.at[ee, :, cols], st, wsem.at[k])
                for k, (w, st) in enumerate(zip(w_hbms, stages))]

    @pl.when((n == 0) & (e == 0))
    def _():
        for c in copies(0, 0):
            c.start(priority=1)

    def wait_current():
        for c in copies(n, e):
            c.wait()

    def start_next():
        wrap = e == n_exp - 1

        @pl.when(jnp.logical_not(wrap & (n == n_tiles - 1)))
        def _():
            for c in copies(jnp.where(wrap, n + 1, n), jnp.where(wrap, 0, e + 1)):
                c.start(priority=1)

    return wait_current, start_next


def _group_loop(gs_ref, gn_ref, src_hbm, dst_hbm, col0, in_buf, out_buf, sem_in, sem_out, compute, after_first_fetch):
    e = pl.program_id(1)
    nb = gn_ref[e]
    b0 = gs_ref[e]
    width = out_buf.shape[2]
    cb = in_buf.shape[1] // EXPERT_BLOCK
    sizes = [cb >> s for s in range(cb.bit_length())]
    assert sizes[-1] == 1 and sum(sizes) == 2 * cb - 1, "blocks per chunk must be a power of two"
    nch = lax.shift_right_logical(nb, cb.bit_length() - 1) + sum((nb >> s) & 1 for s in range(cb.bit_length() - 1))

    def rows(start, nblk):
        return pl.ds(pl.multiple_of((b0 + start) * EXPERT_BLOCK, EXPERT_BLOCK), nblk * EXPERT_BLOCK)

    def fetch(start, slot):
        return pltpu.make_async_copy(src_hbm.at[rows(start, cb), :], in_buf.at[slot], sem_in.at[slot])

    def store(start, slot, nblk):
        return pltpu.make_async_copy(out_buf.at[slot, pl.ds(0, nblk * EXPERT_BLOCK), :],
                                     dst_hbm.at[rows(start, nblk), pl.ds(col0, width)], sem_out.at[slot])

    def piece(remaining):
        z = jnp.int32(1)
        for size in reversed(sizes[:-1]):
            z = jnp.where(remaining >= size, size, z)
        return z

    def for_size(count, fn):
        for size in sizes:
            @pl.when(count == size)
            def _():
                fn(size)

    @pl.when(e == 0)
    def _():
        last = gs_ref.shape[0] - 1
        used = gs_ref[last] + gn_ref[last]
        total = dst_hbm.shape[0] // EXPERT_BLOCK
        out_buf[0, 0:EXPERT_BLOCK, :] = jnp.zeros((EXPERT_BLOCK, width), out_buf.dtype)

        def tail(m):
            blk = pl.ds(pl.multiple_of(m * EXPERT_BLOCK, EXPERT_BLOCK), EXPERT_BLOCK)
            return pltpu.make_async_copy(out_buf.at[0, pl.ds(0, EXPERT_BLOCK), :],
                                         dst_hbm.at[blk, pl.ds(col0, width)], sem_out.at[0])

        lax.fori_loop(used, total, lambda m, _: (tail(m).start(), 0)[1], 0)
        lax.fori_loop(used, total, lambda m, _: (tail(m).wait(), 0)[1], 0)

    @pl.when(nch > 0)
    def _():
        fetch(0, 0).start()

    after_first_fetch()

    def wait_store(slot, size):
        for_size(size, lambda s: store(0, slot, s).wait())

    def step(i, carry):
        start, prev1, prev2 = carry
        size = piece(nb - start)
        slot = i & 1
        fetch(start, slot).wait()

        @pl.when(start + size < nb)
        def _():
            fetch(start + size, 1 - slot).start()

        @pl.when(i >= 2)
        def _():
            wait_store(slot, prev2)

        def run(s):
            n = s * EXPERT_BLOCK
            out_buf[slot, 0:n, :] = compute(in_buf[slot, 0:n, :])
            store(start, slot, s).start()

        for_size(size, run)
        return start + size, size, prev1

    _, last1, last2 = lax.fori_loop(0, nch, step, (jnp.int32(0), jnp.int32(0), jnp.int32(0)))

    @pl.when(nch >= 2)
    def _():
        wait_store(nch & 1, last2)

    @pl.when(nch >= 1)
    def _():
        wait_store((nch - 1) & 1, last1)


def _gateup_kernel(gs_ref, gn_ref, xs_hbm, wg_hbm, wu_hbm, bg_ref, bu_ref, h_hbm,
                   wg_st, wu_st, wgb_ref, wub_ref, xbuf, hbuf, wsem, sem_in, sem_out):
    tn = wgb_ref.shape[1]
    half = xbuf.shape[2]
    wait_weights, start_next_weights = _weight_stage((wg_hbm, wu_hbm), (wg_st, wu_st), wsem, tn)

    def stage_weights():
        wait_weights()

        @pl.when(gn_ref[pl.program_id(1)] > 0)
        def _():
            wgb_ref[...] = wg_st[...].astype(BF16)
            wub_ref[...] = wu_st[...].astype(BF16)

        start_next_weights()

    def compute(xp):
        xa = pltpu.bitcast(xp << 16, F32).astype(BF16)
        xb = pltpu.bitcast(xp & jnp.uint32(0xFFFF0000), F32).astype(BF16)
        gte = _dot(xa, wgb_ref[0:half, :]) + _dot(xb, wgb_ref[half:, :]) + bg_ref[...]
        up = _dot(xa, wub_ref[0:half, :]) + _dot(xb, wub_ref[half:, :]) + bu_ref[...]
        gte = jnp.minimum(gte, SWIGLU_LIMIT)
        up = jnp.clip(up, -SWIGLU_LIMIT, SWIGLU_LIMIT)
        return (gte * jax.nn.sigmoid(SWIGLU_ALPHA * gte) * (up + 1.0)).astype(BF16)

    col0 = pl.multiple_of(pl.program_id(0) * tn, tn)
    _group_loop(gs_ref, gn_ref, xs_hbm, h_hbm, col0, xbuf, hbuf, sem_in, sem_out, compute, stage_weights)


def _gateup(grp_start, grp_blocks, xs, w_gate, w_up, b_gate, b_up, tn):
    n_rows, half = xs.shape
    n_exp, d, f = w_gate.shape
    chunk = CHUNK_BLOCKS * EXPERT_BLOCK
    any_spec = pl.BlockSpec(memory_space=pl.ANY)
    b_spec = pl.BlockSpec((None, 1, tn), lambda n, e, gs, gn: (e, 0, n))
    return pl.pallas_call(
        _gateup_kernel,
        grid_spec=pltpu.PrefetchScalarGridSpec(
            num_scalar_prefetch=2,
            grid=(f // tn, n_exp),
            in_specs=[any_spec, any_spec, any_spec, b_spec, b_spec],
            out_specs=any_spec,
            scratch_shapes=[pltpu.VMEM((d, tn), F32), pltpu.VMEM((d, tn), F32),
                            pltpu.VMEM((d, tn), BF16), pltpu.VMEM((d, tn), BF16),
                            pltpu.VMEM((2, chunk, half), xs.dtype), pltpu.VMEM((2, chunk, tn), BF16),
                            pltpu.SemaphoreType.DMA((2,)), pltpu.SemaphoreType.DMA((2,)),
                            pltpu.SemaphoreType.DMA((2,))]),
        out_shape=jax.ShapeDtypeStruct((n_rows, f), BF16),
        compiler_params=_params("arbitrary", "arbitrary"),
        name="expert_gate_up",
    )(grp_start, grp_blocks, xs, w_gate, w_up, b_gate, b_up)


def _down_kernel(gs_ref, gn_ref, h_hbm, wd_hbm, bd_ref, y_hbm, wd_st, wdb_ref, hbuf, ybuf, wsem, sem_in, sem_out):
    tn = wdb_ref.shape[1]
    wait_weights, start_next_weights = _weight_stage((wd_hbm,), (wd_st,), wsem, tn)

    def stage_weights():
        wait_weights()

        @pl.when(gn_ref[pl.program_id(1)] > 0)
        def _():
            wdb_ref[...] = wd_st[...].astype(BF16)

        start_next_weights()

    def compute(h):
        def strip(c0):
            y = _dot(h, wdb_ref[:, c0:c0 + DOWN_STRIP]) + bd_ref[:, c0:c0 + DOWN_STRIP]
            return pltpu.bitcast(y.astype(BF16).astype(F32), jnp.uint32)

        return jnp.concatenate([(strip(c0) >> 16) | strip(tn // 2 + c0) for c0 in range(0, tn // 2, DOWN_STRIP)],
                               axis=1)

    col0 = pl.multiple_of(pl.program_id(0) * (tn // 2), tn // 2)
    _group_loop(gs_ref, gn_ref, h_hbm, y_hbm, col0, hbuf, ybuf, sem_in, sem_out, compute, stage_weights)


def _down(grp_start, grp_blocks, h, w_down, b_down, tn):
    n_rows, f = h.shape
    n_exp, _, d = w_down.shape
    return pl.pallas_call(
        _down_kernel,
        grid_spec=pltpu.PrefetchScalarGridSpec(
            num_scalar_prefetch=2,
            grid=(d // tn, n_exp),
            in_specs=[pl.BlockSpec(memory_space=pl.ANY),
                      pl.BlockSpec(memory_space=pl.ANY),
                      pl.BlockSpec((None, 1, tn), lambda n, e, gs, gn: (e, 0, n))],
            out_specs=pl.BlockSpec(memory_space=pl.ANY),
            scratch_shapes=[pltpu.VMEM((f, tn), F32), pltpu.VMEM((f, tn), BF16),
                            pltpu.VMEM((2, DOWN_CHUNK_BLOCKS * EXPERT_BLOCK, f), h.dtype),
                            pltpu.VMEM((2, DOWN_CHUNK_BLOCKS * EXPERT_BLOCK, tn // 2), jnp.uint32),
                            pltpu.SemaphoreType.DMA((1,)), pltpu.SemaphoreType.DMA((2,)),
                            pltpu.SemaphoreType.DMA((2,))]),
        out_shape=jax.ShapeDtypeStruct((n_rows, d // 2), jnp.uint32),
        compiler_params=_params("arbitrary", "arbitrary"),
        name="expert_down",
    )(grp_start, grp_blocks, h, w_down, b_down)


def _combine_kernel(dest_ref, dest_next_ref, wt_ref, h_ref, y_hbm, o_ref, buf_a, buf_b, wb_ref, sem, *, tile_cols):
    i = pl.program_id(0)
    n_cols = y_hbm.shape[1]
    half = tile_cols // 2
    n_chunks = n_cols // LANES
    rows_per_chunk = LANES // n_chunks

    def row_copy(idx_ref, t, buf, s, r, k):
        return pltpu.make_async_copy(y_hbm.at[pl.ds(idx_ref[t, k, r], 1), :], buf.at[k, pl.ds(r, 1), :], sem.at[s])

    def drain(idx_ref, t, buf, s):
        def wait(r, _):
            for k in range(TOP_K):
                row_copy(idx_ref, t, buf, s, r, k).wait()
            return 0

        lax.fori_loop(0, LANES, wait, 0)

    def combine_tile(t, buf, issue_next):
        rows = slice(t * LANES, (t + 1) * LANES)
        wt = wt_ref[:, rows]
        for k in range(TOP_K):
            wb_ref[k] = jnp.broadcast_to(wt[k:k + 1, :], (LANES, LANES)).T
        for c in range(n_chunks):
            col = (c * LANES // half) * tile_cols + (c * LANES) % half
            lo = h_ref[rows, col:col + LANES]
            hi = h_ref[rows, col + half:col + half + LANES]
            for k in range(TOP_K):
                u = buf[k, :, c * LANES:(c + 1) * LANES]
                lo = lo + wb_ref[k] * pltpu.bitcast(u << 16, F32)
                hi = hi + wb_ref[k] * pltpu.bitcast(u & jnp.uint32(0xFFFF0000), F32)
            o_ref[rows, col:col + LANES] = lo
            o_ref[rows, col + half:col + half + LANES] = hi
            for r in range(c * rows_per_chunk, (c + 1) * rows_per_chunk):
                for k in range(TOP_K):
                    issue_next(r, k)

    @pl.when(i == 0)
    def _():
        def issue(r, _):
            for k in range(TOP_K):
                row_copy(dest_ref, 0, buf_a, 0, r, k).start(priority=k % 2)
            return 0

        lax.fori_loop(0, LANES, issue, 0)

    drain(dest_ref, 0, buf_a, 0)
    combine_tile(0, buf_a, lambda r, k: row_copy(dest_ref, 1, buf_b, 1, r, k).start(priority=k % 2))
    drain(dest_ref, 1, buf_b, 1)
    combine_tile(1, buf_b, lambda r, k: row_copy(dest_next_ref, 0, buf_a, 0, r, k).start(priority=k % 2))

    @pl.when(i == pl.num_programs(0) - 1)
    def _():
        drain(dest_next_ref, 0, buf_a, 0)


def _combine(dest, wts, h1, y, tile_cols):
    t, d = h1.shape
    nt = t // LANES
    assert nt % 2 == 0 and LANES % (y.shape[1] // LANES) == 0
    smem = pltpu.SMEM
    return pl.pallas_call(
        functools.partial(_combine_kernel, tile_cols=tile_cols),
        grid=(nt // 2,),
        in_specs=[pl.BlockSpec((2, TOP_K, LANES), lambda i: (i, 0, 0), memory_space=smem),
                  pl.BlockSpec((1, TOP_K, LANES), lambda i: (jnp.minimum(2 * i + 2, nt - 1), 0, 0), memory_space=smem),
                  pl.BlockSpec((TOP_K, 2 * LANES), lambda i: (0, i)),
                  pl.BlockSpec((2 * LANES, d), lambda i: (i, 0)),
                  pl.BlockSpec(memory_space=pl.ANY)],
        out_specs=pl.BlockSpec((2 * LANES, d), lambda i: (i, 0)),
        out_shape=jax.ShapeDtypeStruct((t, d), F32),
        scratch_shapes=[pltpu.VMEM((TOP_K, LANES, y.shape[1]), y.dtype),
                        pltpu.VMEM((TOP_K, LANES, y.shape[1]), y.dtype),
                        pltpu.VMEM((TOP_K, LANES, LANES), F32),
                        pltpu.SemaphoreType.DMA((2,))],
        compiler_params=_params("arbitrary"),
        name="combine_rows",
    )(dest, dest, wts, h1, y)


def _layer(x, meta, norm_mix, w_in, b_forget, b_branch_gate, q_norm, k_norm, w_pool_group, pool_scale,
           w_branch_pool, w_branch_attn, w_out, norm_ffn, w_router, b_router, w_gate, b_gate, w_up, b_up,
           w_down, b_down):
    batch, seq, d = x.shape
    t = batch * seq
    n_meta = meta.shape[0]
    pool_w = w_pool_group.shape[0] * w_pool_group.shape[1]
    attn_w = w_branch_attn.shape[0]
    n_heads = attn_w // HEAD_DIM
    n_exp = w_router.shape[1]
    col_q = pool_w
    col_zf = pool_w + 3 * attn_w
    col_zg = col_zf + n_heads
    assert n_heads <= LANES and col_zf % LANES == 0

    tm = min(1024, t)
    x2 = x.reshape(t, d)

    hn = _rmsnorm(x2, norm_mix.reshape(1, d), min(256, t))
    hn_meta = _rmsnorm(meta, norm_mix.reshape(1, d), n_meta)

    w_in_t = w_in.T
    uqkv, uqkv_meta = _proj(hn, hn_meta, w_in_t, None, col0=0, ncols=col_zf, tn=512, tm=tm,
                            out_dtype=BF16, name="in_proj_uqkv")
    zf, zf_meta = _proj(hn, hn_meta, w_in_t, None, col0=col_zf, ncols=LANES, tn=LANES, tm=tm,
                        out_dtype=F32, name="in_proj_forget")
    gates = _proj(hn, None, w_in_t, b_branch_gate.reshape(1, 2 * d), col0=col_zf, ncols=2 * d, tn=512, tm=tm,
                  out_dtype=BF16, name="in_proj_gates", shift=col_zg - col_zf)

    bf_pad = jnp.pad(b_forget.reshape(1, n_heads), ((0, 0), (0, LANES - n_heads)))
    zfm_pad = jnp.pad(zf_meta, ((0, LANES - n_meta), (0, 0)))
    cr, mbias = _cumsum(zf, zfm_pad, bf_pad, batch, n_heads, n_meta)

    y_pool = _pool(uqkv, uqkv_meta, w_pool_group, pool_scale.reshape(1, pool_w), batch, min(512, seq))
    y_attn = _attention(uqkv, uqkv_meta, cr, mbias, q_norm.reshape(1, HEAD_DIM), k_norm.reshape(1, HEAD_DIM),
                        batch, n_heads, col_q, min(512, seq))

    merged = _merge(y_pool, y_attn, w_branch_pool, w_branch_attn, gates, tm, 512)
    h1 = _outproj(merged, w_out, x2, tm, 512)

    xp, idx, wts = _router(h1, norm_ffn.reshape(1, d), w_router.T, b_router.reshape(n_exp, 1), min(256, t))
    n_blocks = -(-(t * TOP_K + n_exp * (EXPERT_BLOCK - 1)) // EXPERT_BLOCK) + max(CHUNK_BLOCKS, DOWN_CHUNK_BLOCKS) - 1
    dest, grp_start, grp_blocks = _plan(idx, n_exp)
    xs = _dispatch(grp_start, grp_blocks, dest, xp, n_blocks * EXPERT_BLOCK)
    f = w_gate.shape[2]
    hmid = _gateup(grp_start, grp_blocks, xs, w_gate, w_up, b_gate.reshape(n_exp, 1, f), b_up.reshape(n_exp, 1, f), 512)
    down_tn = 2048
    y = _down(grp_start, grp_blocks, hmid, w_down, b_down.reshape(n_exp, 1, d), down_tn)
    out = _combine(dest, wts, h1, y, down_tn)
    return out.reshape(batch, seq, d)


def kernel(x, meta_tokens, norm_mix, w_in, b_forget, b_branch_gate, q_norm, k_norm, w_pool_group, pool_scale,
           w_branch_pool, w_branch_attn, w_out, norm_ffn, w_router, b_router, w_gate, b_gate, w_up, b_up,
           w_down, b_down):
    depth = norm_mix.shape[0]
    assert depth == 1, "the fused layer pipeline drops the meta rows after the (single) layer"
    return _layer(x, meta_tokens, norm_mix[0], w_in[0], b_forget[0], b_branch_gate[0], q_norm[0], k_norm[0],
                  w_pool_group[0], pool_scale[0], w_branch_pool[0], w_branch_attn[0], w_out[0], norm_ffn[0],
                  w_router[0], b_router[0], w_gate[0], b_gate[0], w_up[0], b_up[0], w_down[0], b_down[0])
```

```python
import functools

import jax
import jax.numpy as jnp
from jax import lax
from jax.experimental import pallas as pl
from jax.experimental.pallas import tpu as pltpu

NORM_EPS = 1e-6
HEAD_DIM = 128
POOL_WINDOWS = (2, 4, 8, 16)
TOP_K = 4
SWIGLU_LIMIT = 7.0
SWIGLU_ALPHA = 1.702
EXPERT_BLOCK = 128
CHUNK_BLOCKS = 4
DOWN_CHUNK_BLOCKS = 4
LANES = 128
NEG_BIG = -1e30
LOG2E = 1.4426950408889634
ONES_ROWS = 16
HEAD_GROUP = 2
VMEM_LIMIT = 56 * 1024 * 1024

F32 = jnp.float32
BF16 = jnp.bfloat16


def _params(*sem, vmem=VMEM_LIMIT):
    return pltpu.CompilerParams(dimension_semantics=sem, vmem_limit_bytes=vmem)


def _dot(a, b):
    return jnp.dot(a, b, preferred_element_type=F32)


def _dot_nt(a, b):
    return lax.dot_general(a, b, (((1,), (1,)), ((), ())), preferred_element_type=F32)


def _rms(x, gain):
    ms = jnp.mean(x * x, axis=-1, keepdims=True)
    return x * lax.rsqrt(ms + NORM_EPS) * gain


def _rmsnorm_kernel(x_ref, g_ref, o_ref):
    o_ref[...] = _rms(x_ref[...], g_ref[...]).astype(o_ref.dtype)


def _rmsnorm(x, gain, tm):
    rows, d = x.shape
    return pl.pallas_call(
        _rmsnorm_kernel,
        grid=(rows // tm,),
        in_specs=[pl.BlockSpec((tm, d), lambda i: (i, 0)), pl.BlockSpec((1, d), lambda i: (0, 0))],
        out_specs=pl.BlockSpec((tm, d), lambda i: (i, 0)),
        out_shape=jax.ShapeDtypeStruct((rows, d), BF16),
        compiler_params=_params("arbitrary"),
        name="rmsnorm",
    )(x, gain)


def _proj_kernel(*refs, has_meta, gate_bias, shift):
    it = iter(refs)
    x_ref = next(it)
    xm_ref = next(it) if has_meta else None
    w_ref = next(it)
    wn_ref = next(it) if shift else None
    b_ref = next(it) if gate_bias else None
    o_ref = next(it)
    om_ref = next(it) if has_meta else None
    wbf_ref = next(it)
    tn, k = w_ref.shape

    @pl.when(pl.program_id(1) == 0)
    def _():
        def chunk(c, _):
            off = pl.multiple_of(c * LANES, LANES)
            if shift:
                wt = jnp.concatenate([w_ref[shift:tn, pl.ds(off, LANES)], wn_ref[:, pl.ds(off, LANES)]], axis=0)
            else:
                wt = w_ref[:, pl.ds(off, LANES)]
            wbf_ref[pl.ds(off, LANES), :] = wt.T.astype(BF16)
            return 0

        lax.fori_loop(0, k // LANES, chunk, 0, unroll=4)
        if has_meta:
            om_ref[...] = _dot(xm_ref[...], wbf_ref[...]).astype(om_ref.dtype)

    acc = _dot(x_ref[...], wbf_ref[...])
    if gate_bias:
        acc = jax.nn.sigmoid(acc + b_ref[...])
    o_ref[...] = acc.astype(o_ref.dtype)


def _proj(x, xm, wt, bias, *, col0, ncols, tn, tm, out_dtype, name, shift=0):
    rows, k = x.shape
    assert col0 % tn == 0 and ncols % tn == 0 and rows % tm == 0
    jb = col0 // tn
    in_specs = [pl.BlockSpec((tm, k), lambda j, i: (i, 0))]
    args = [x]
    if xm is not None:
        in_specs.append(pl.BlockSpec(xm.shape, lambda j, i: (0, 0)))
        args.append(xm)
    in_specs.append(pl.BlockSpec((tn, k), lambda j, i: (jb + j, 0)))
    args.append(wt)
    if shift:
        assert shift % 8 == 0 and tn % shift == 0 and (col0 + ncols + shift) <= wt.shape[0]
        per = tn // shift
        in_specs.append(pl.BlockSpec((shift, k), lambda j, i: ((jb + j + 1) * per, 0)))
        args.append(wt)
    if bias is not None:
        in_specs.append(pl.BlockSpec((1, tn), lambda j, i: (0, j)))
        args.append(bias)
    out_specs = [pl.BlockSpec((tm, tn), lambda j, i: (i, j))]
    out_shape = [jax.ShapeDtypeStruct((rows, ncols), out_dtype)]
    if xm is not None:
        out_specs.append(pl.BlockSpec((xm.shape[0], tn), lambda j, i: (0, j)))
        out_shape.append(jax.ShapeDtypeStruct((xm.shape[0], ncols), out_dtype))
    res = pl.pallas_call(
        functools.partial(_proj_kernel, has_meta=xm is not None, gate_bias=bias is not None, shift=shift),
        grid=(ncols // tn, rows // tm),
        in_specs=in_specs,
        out_specs=out_specs,
        out_shape=out_shape,
        scratch_shapes=[pltpu.VMEM((k, tn), BF16)],
        compiler_params=_params("arbitrary", "arbitrary"),
        name=name,
    )(*args)
    return res if xm is not None else res[0]


def _cumsum_kernel(zf_ref, zfm_ref, bf_ref, cr_ref, mb_ref, *, n_meta):
    seq = zf_ref.shape[0]
    bf = bf_ref[...]
    r = lax.broadcasted_iota(jnp.int32, (LANES, LANES), 0)
    c = lax.broadcasted_iota(jnp.int32, (LANES, LANES), 1)
    tri = (c <= r).astype(F32)
    sfx = (c > r).astype(F32)

    def prefix(mat, x):
        return jnp.dot(mat, x, preferred_element_type=F32, precision=lax.Precision.HIGHEST)

    n_meta_pad = zfm_ref.shape[0]
    lfm = jax.nn.log_sigmoid(zfm_ref[...] + bf)
    rows = lax.broadcasted_iota(jnp.int32, (n_meta_pad, LANES), 0)
    lfm = jnp.where(rows < n_meta, lfm, 0.0)
    mbias = jnp.where(rows < n_meta, prefix(sfx, lfm), NEG_BIG)
    mbias_t = mbias.T
    for h in range(mb_ref.shape[0]):
        mb_ref[h] = mbias_t[h:h + 1, :]

    def body(blk, carry):
        off = pl.multiple_of(blk * LANES, LANES)
        lf = jax.nn.log_sigmoid(zf_ref[pl.ds(off, LANES), :] + bf)
        cs = prefix(tri, lf) + carry
        cs_t = cs.T
        for h in range(cr_ref.shape[0]):
            cr_ref[h, :, pl.ds(off, LANES)] = cs_t[h:h + 1, :]
        return cs[LANES - 1:LANES, :]

    lax.fori_loop(0, seq // LANES, body, jnp.zeros((1, LANES), F32))


def _cumsum(zf, zfm_pad, bf_pad, batch, n_heads, n_meta):
    t = zf.shape[0]
    seq = t // batch
    return pl.pallas_call(
        functools.partial(_cumsum_kernel, n_meta=n_meta),
        grid=(batch,),
        in_specs=[pl.BlockSpec((seq, LANES), lambda b: (b, 0)),
                  pl.BlockSpec(zfm_pad.shape, lambda b: (0, 0)),
                  pl.BlockSpec((1, LANES), lambda b: (0, 0))],
        out_specs=[pl.BlockSpec((None, n_heads, 1, seq), lambda b: (b, 0, 0, 0)),
                   pl.BlockSpec((n_heads, 1, LANES), lambda b: (0, 0, 0))],
        out_shape=[jax.ShapeDtypeStruct((batch, n_heads, 1, seq), F32),
                   jax.ShapeDtypeStruct((n_heads, 1, LANES), F32)],
        compiler_params=_params("arbitrary"),
        name="forget_cumsum",
    )(zf, zfm_pad, bf_pad)


def _pool_kernel(u_ref, um_ref, wp_ref, ps_ref, o_ref, ext_ref, wbf_ref):
    ts = u_ref.shape[0]
    halo = um_ref.shape[0]
    gw = wp_ref.shape[1]
    i = pl.program_id(1)

    @pl.when((pl.program_id(0) == 0) & (i == 0))
    def _():
        wbf_ref[...] = wp_ref[...].astype(BF16)

    @pl.when(i == 0)
    def _():
        ext_ref[0:halo, :] = um_ref[...].astype(F32)

    @pl.when(i > 0)
    def _():
        ext_ref[0:halo, :] = ext_ref[ts:ts + halo, :]

    ext_ref[halo:halo + ts, :] = u_ref[...].astype(F32)
    for g, w in enumerate(POOL_WINDOWS):
        cols = slice(g * gw, (g + 1) * gw)
        uf = ext_ref[halo:halo + ts, cols]
        acc = uf
        for d in range(1, w):
            acc = acc + ext_ref[halo - d:halo - d + ts, cols]
        pooled = acc * (1.0 / w) - uf
        y = _dot(pooled.astype(BF16), wbf_ref[g]) * ps_ref[:, cols]
        o_ref[:, cols] = y.astype(o_ref.dtype)


def _pool(uqkv, uqkv_meta, w_pool, pool_scale, batch, ts):
    t = uqkv.shape[0]
    seq = t // batch
    ng, gw, _ = w_pool.shape
    pw = ng * gw
    halo = uqkv_meta.shape[0]
    assert halo >= max(POOL_WINDOWS) and seq % ts == 0
    nt = seq // ts
    return pl.pallas_call(
        _pool_kernel,
        grid=(batch, nt),
        in_specs=[pl.BlockSpec((ts, pw), lambda b, i: (b * nt + i, 0)),
                  pl.BlockSpec((halo, pw), lambda b, i: (0, 0)),
                  pl.BlockSpec(w_pool.shape, lambda b, i: (0, 0, 0)),
                  pl.BlockSpec((1, pw), lambda b, i: (0, 0))],
        out_specs=pl.BlockSpec((ts, pw), lambda b, i: (b * nt + i, 0)),
        out_shape=jax.ShapeDtypeStruct((t, pw), BF16),
        scratch_shapes=[pltpu.VMEM((halo + ts, pw), F32), pltpu.VMEM(w_pool.shape, BF16)],
        compiler_params=_params("arbitrary", "arbitrary"),
        name="pool_mixer",
    )(uqkv, uqkv_meta, w_pool, pool_scale)


def _attn_kernel(q_ref, k_ref, v_ref, km_ref, vm_ref, cr_ref, mb_ref, qg_ref, kg_ref, o_ref,
                 ka_ref, va_ref, kma_ref, vma_ref):
    tq = q_ref.shape[0]
    seq = k_ref.shape[0]
    n_meta = km_ref.shape[0]
    n_group, va_rows = va_ref.shape[0], va_ref.shape[1]
    qi = pl.program_id(2)
    lane = lax.broadcasted_iota(jnp.int32, (LANES, LANES), 1)
    heads = [(g, slice(g * HEAD_DIM, (g + 1) * HEAD_DIM)) for g in range(n_group)]

    def bias_pieces(col):
        hi = col.astype(BF16).astype(F32)
        mid = (col - hi).astype(BF16).astype(F32)
        lo = (col - hi - mid).astype(BF16).astype(F32)
        return jnp.where(lane == 0, hi, jnp.where(lane == 1, mid, jnp.where(lane == 2, lo, 0.0))).astype(BF16)

    ones_row = (lax.broadcasted_iota(jnp.int32, (va_rows - HEAD_DIM, LANES), 0) == 0).astype(F32)

    @pl.when(qi == 0)
    def _():
        def chunk(i, _):
            off = pl.multiple_of(i * LANES, LANES)
            for g, cols in heads:
                vt = v_ref[pl.ds(off, LANES), cols].astype(F32).T
                va_ref[g, :, pl.ds(off, LANES)] = jnp.concatenate([vt, ones_row], axis=0).astype(BF16)
                col = jnp.broadcast_to(cr_ref[g, :, pl.ds(off, LANES)] * LOG2E, (LANES, LANES)).T
                ka_ref[g, pl.ds(off, LANES), HEAD_DIM:] = bias_pieces(col)
            return 0

        lax.fori_loop(0, seq // LANES, chunk, 0, unroll=2)
        pad = jnp.zeros((LANES - n_meta, HEAD_DIM), F32)
        for g, cols in heads:
            ka_ref[g, :, 0:HEAD_DIM] = _rms(k_ref[:, cols].astype(F32), kg_ref[...]).astype(BF16)
            kmn = jnp.concatenate([_rms(km_ref[:, cols].astype(F32), kg_ref[...]), pad], axis=0)
            kma_ref[g, :, 0:HEAD_DIM] = kmn.astype(BF16)
            kma_ref[g, :, HEAD_DIM:] = bias_pieces(-jnp.broadcast_to(mb_ref[g] * LOG2E, (LANES, LANES)).T)
            vmt = jnp.concatenate([vm_ref[:, cols].astype(F32), pad], axis=0).T
            vma_ref[g] = jnp.concatenate([vmt, ones_row], axis=0).astype(BF16)

    minus_ones = jnp.where(lax.broadcasted_iota(jnp.int32, (tq, LANES), 1) < 3, -1.0, 0.0).astype(BF16)
    qa = [jnp.concatenate([(_rms(q_ref[:, cols].astype(F32), qg_ref[...]) * (HEAD_DIM ** -0.5 * LOG2E)).astype(BF16),
                           minus_ones], axis=1) for g, cols in heads]
    q0 = pl.multiple_of(qi * tq, tq)

    def update(carry, st, va_blk):
        m, acc = carry
        m_new = jnp.maximum(m, jnp.max(st, axis=0, keepdims=True))
        p = jnp.exp2(st - m_new)
        acc = jnp.exp2(m - m_new) * acc + _dot(va_blk, p.astype(BF16))
        return m_new, acc

    def scores(g, off, size):
        return _dot_nt(ka_ref[g, pl.ds(off, size), :], qa[g])

    def body(j, carry):
        off = pl.multiple_of(j * (2 * tq), 2 * tq)
        return tuple(update(carry[g], scores(g, off, 2 * tq), va_ref[g, :, pl.ds(off, 2 * tq)]) for g, _ in heads)

    carry = tuple((jnp.full((1, tq), NEG_BIG, F32), jnp.zeros((va_rows, tq), F32)) for _ in heads)
    carry = lax.fori_loop(0, lax.shift_right_logical(qi, 1), body, carry)

    def finish(with_prev):
        key = lax.broadcasted_iota(jnp.int32, (tq, tq), 0)
        qry = lax.broadcasted_iota(jnp.int32, (tq, tq), 1)
        for g, cols in heads:
            sts = [_dot_nt(kma_ref[g], qa[g])]
            vas = [vma_ref[g]]
            if with_prev:
                prev = pl.multiple_of(q0 - tq, tq)
                sts.append(scores(g, prev, tq))
                vas.append(va_ref[g, :, pl.ds(prev, tq)])
            sts.append(jnp.where(key <= qry, scores(g, q0, tq), NEG_BIG))
            vas.append(va_ref[g, :, pl.ds(q0, tq)])
            m, acc = update(carry[g], jnp.concatenate(sts, axis=0), jnp.concatenate(vas, axis=1))
            o_ref[:, cols] = (acc[0:HEAD_DIM, :] / acc[HEAD_DIM:HEAD_DIM + 1, :]).T.astype(o_ref.dtype)

    odd = (qi & 1) == 1

    @pl.when(odd)
    def _():
        finish(True)

    @pl.when(jnp.logical_not(odd))
    def _():
        finish(False)


def _attention(uqkv, uqkv_meta, cr, mbias, q_gain, k_gain, batch, n_heads, col_q, tq):
    t = uqkv.shape[0]
    seq = t // batch
    n_meta = uqkv_meta.shape[0]
    nq = seq // tq
    grp = HEAD_GROUP
    gw = grp * HEAD_DIM
    assert n_heads % grp == 0 and col_q % gw == 0 and (n_heads * HEAD_DIM) % gw == 0
    bq = col_q // gw
    bk = bq + n_heads // grp
    bv = bk + n_heads // grp
    return pl.pallas_call(
        _attn_kernel,
        grid=(batch, n_heads // grp, nq),
        in_specs=[pl.BlockSpec((tq, gw), lambda b, h, i: (b * nq + i, bq + h)),
                  pl.BlockSpec((seq, gw), lambda b, h, i: (b, bk + h)),
                  pl.BlockSpec((seq, gw), lambda b, h, i: (b, bv + h)),
                  pl.BlockSpec((n_meta, gw), lambda b, h, i: (0, bk + h)),
                  pl.BlockSpec((n_meta, gw), lambda b, h, i: (0, bv + h)),
                  pl.BlockSpec((None, grp, 1, seq), lambda b, h, i: (b, h, 0, 0)),
                  pl.BlockSpec((grp, 1, LANES), lambda b, h, i: (h, 0, 0)),
                  pl.BlockSpec((1, HEAD_DIM), lambda b, h, i: (0, 0)),
                  pl.BlockSpec((1, HEAD_DIM), lambda b, h, i: (0, 0))],
        out_specs=pl.BlockSpec((tq, gw), lambda b, h, i: (b * nq + i, h)),
        out_shape=jax.ShapeDtypeStruct((t, n_heads * HEAD_DIM), BF16),
        scratch_shapes=[pltpu.VMEM((grp, seq, HEAD_DIM + LANES), BF16),
                        pltpu.VMEM((grp, HEAD_DIM + ONES_ROWS, seq), BF16),
                        pltpu.VMEM((grp, LANES, HEAD_DIM + LANES), BF16),
                        pltpu.VMEM((grp, HEAD_DIM + ONES_ROWS, LANES), BF16)],
        compiler_params=_params("arbitrary", "arbitrary", "arbitrary"),
        name="forgetting_attention",
    )(uqkv, uqkv, uqkv, uqkv_meta, uqkv_meta, cr, mbias, q_gain, k_gain)


def _merge_kernel(yp_ref, ya_ref, wp_ref, wa_ref, gp_ref, ga_ref, o_ref, wpb_ref, wab_ref):
    @pl.when(pl.program_id(1) == 0)
    def _():
        wpb_ref[...] = wp_ref[...].astype(BF16)
        wab_ref[...] = wa_ref[...].astype(BF16)

    a = _dot(yp_ref[...], wpb_ref[...])
    b = _dot(ya_ref[...], wab_ref[...])
    o_ref[...] = (gp_ref[...].astype(F32) * a + ga_ref[...].astype(F32) * b).astype(o_ref.dtype)


def _merge(y_pool, y_attn, w_bp, w_ba, gates, tm, tn):
    t, kp = y_pool.shape
    ka = y_attn.shape[1]
    d = w_bp.shape[1]
    nj = d // tn
    return pl.pallas_call(
        _merge_kernel,
        grid=(nj, t // tm),
        in_specs=[pl.BlockSpec((tm, kp), lambda j, i: (i, 0)),
                  pl.BlockSpec((tm, ka), lambda j, i: (i, 0)),
                  pl.BlockSpec((kp, tn), lambda j, i: (0, j)),
                  pl.BlockSpec((ka, tn), lambda j, i: (0, j)),
                  pl.BlockSpec((tm, tn), lambda j, i: (i, j)),
                  pl.BlockSpec((tm, tn), lambda j, i: (i, nj + j))],
        out_specs=pl.BlockSpec((tm, tn), lambda j, i: (i, j)),
        out_shape=jax.ShapeDtypeStruct((t, d), BF16),
        scratch_shapes=[pltpu.VMEM((kp, tn), BF16), pltpu.VMEM((ka, tn), BF16)],
        compiler_params=_params("arbitrary", "arbitrary"),
        name="branch_merge",
    )(y_pool, y_attn, w_bp, w_ba, gates, gates)


def _outproj_kernel(m_ref, w_ref, x_ref, o_ref, wbf_ref):
    @pl.when(pl.program_id(1) == 0)
    def _():
        wbf_ref[...] = w_ref[...].astype(BF16)

    o_ref[...] = x_ref[...] + _dot(m_ref[...], wbf_ref[...])


def _outproj(merged, w_out, x, tm, tn):
    t, k = merged.shape
    d = w_out.shape[1]
    return pl.pallas_call(
        _outproj_kernel,
        grid=(d // tn, t // tm),
        in_specs=[pl.BlockSpec((tm, k), lambda j, i: (i, 0)),
                  pl.BlockSpec((k, tn), lambda j, i: (0, j)),
                  pl.BlockSpec((tm, tn), lambda j, i: (i, j))],
        out_specs=pl.BlockSpec((tm, tn), lambda j, i: (i, j)),
        out_shape=jax.ShapeDtypeStruct((t, d), F32),
        scratch_shapes=[pltpu.VMEM((k, tn), BF16)],
        compiler_params=_params("arbitrary", "arbitrary"),
        name="out_proj",
    )(merged, w_out, x)


def _router_kernel(h_ref, g_ref, wr_ref, br_ref, xp_ref, idx_ref, wt_ref):
    y = _rms(h_ref[...], g_ref[...])
    y_hi = y.astype(BF16)
    y_lo = (y - y_hi.astype(F32)).astype(BF16)
    w = wr_ref[...]
    w_hi = w.astype(BF16)
    w_lo = (w - w_hi.astype(F32)).astype(BF16)
    logits = _dot_nt(w_hi, y_hi) + (_dot_nt(w_hi, y_lo) + _dot_nt(w_lo, y_hi)) + br_ref[...]

    half = y.shape[1] // 2
    bits = pltpu.bitcast(y_hi.astype(F32), jnp.uint32)
    xp_ref[...] = (bits[:, :half] >> 16) | bits[:, half:]

    n_exp = logits.shape[0]
    eid = lax.broadcasted_iota(jnp.int32, logits.shape, 0)
    vals = logits
    top_v, top_i = [], []
    for _ in range(TOP_K):
        mx = jnp.max(vals, axis=0, keepdims=True)
        sel = jnp.min(jnp.where(vals == mx, eid, n_exp), axis=0, keepdims=True)
        top_v.append(mx)
        top_i.append(sel)
        vals = jnp.where(eid == sel, -jnp.inf, vals)
    ex = [jnp.exp(v - top_v[0]) for v in top_v]
    den = ex[0] + ex[1] + ex[2] + ex[3]
    for k in range(TOP_K):
        idx_ref[k:k + 1, :] = top_i[k]
        wt_ref[k:k + 1, :] = ex[k] / den


def _router(h1, gain, w_router_t, b_router, tm):
    t, d = h1.shape
    n_exp = w_router_t.shape[0]
    return pl.pallas_call(
        _router_kernel,
        grid=(t // tm,),
        in_specs=[pl.BlockSpec((tm, d), lambda i: (i, 0)),
                  pl.BlockSpec((1, d), lambda i: (0, 0)),
                  pl.BlockSpec((n_exp, d), lambda i: (0, 0)),
                  pl.BlockSpec((n_exp, 1), lambda i: (0, 0))],
        out_specs=[pl.BlockSpec((tm, d // 2), lambda i: (i, 0)),
                   pl.BlockSpec((TOP_K, tm), lambda i: (0, i)),
                   pl.BlockSpec((TOP_K, tm), lambda i: (0, i))],
        out_shape=[jax.ShapeDtypeStruct((t, d // 2), jnp.uint32),
                   jax.ShapeDtypeStruct((TOP_K, t), jnp.int32),
                   jax.ShapeDtypeStruct((TOP_K, t), F32)],
        compiler_params=_params("arbitrary"),
        name="router_topk",
    )(h1, gain, w_router_t, b_router)


def _plan_kernel(idx_ref, dest_ref, grp_ref, rank_ref, *, n_exp):
    t = idx_ref.shape[1]
    nblk = t // LANES
    eid = lax.broadcasted_iota(jnp.int32, (n_exp, LANES), 0)
    r = lax.broadcasted_iota(jnp.int32, (LANES, LANES), 0)
    c = lax.broadcasted_iota(jnp.int32, (LANES, LANES), 1)
    upper = (r <= c).astype(BF16)

    def member(off):
        m = jnp.zeros((n_exp, LANES), F32)
        for k in range(TOP_K):
            m = m + (idx_ref[k:k + 1, pl.ds(off, LANES)] == eid).astype(F32)
        return m

    def count(blk, carry):
        off = pl.multiple_of(blk * LANES, LANES)
        m = member(off)
        cs = _dot(m.astype(BF16), upper) + carry
        rank_ref[:, pl.ds(off, LANES)] = cs - m
        return carry + jnp.sum(m, axis=1, keepdims=True)

    counts = lax.fori_loop(0, nblk, count, jnp.zeros((n_exp, 1), F32))
    padded = jnp.floor((counts + (EXPERT_BLOCK - 1)) * (1.0 / EXPERT_BLOCK)) * EXPERT_BLOCK
    er = lax.broadcasted_iota(jnp.int32, (n_exp, n_exp), 0)
    ec = lax.broadcasted_iota(jnp.int32, (n_exp, n_exp), 1)
    lower = (ec <= er).astype(BF16)
    pad_end = _dot(lower, jnp.broadcast_to(padded, (n_exp, LANES)).astype(BF16))[:, 0:1]
    pad_start = pad_end - padded

    def place(blk, _):
        off = pl.multiple_of(blk * LANES, LANES)
        pos = pad_start + rank_ref[:, pl.ds(off, LANES)]
        for k in range(TOP_K):
            sel = idx_ref[k:k + 1, pl.ds(off, LANES)] == eid
            d = jnp.sum(jnp.where(sel, pos, 0.0), axis=0, keepdims=True)
            dest_ref[blk, k:k + 1, :] = d.astype(jnp.int32)
        return 0

    lax.fori_loop(0, nblk, place, 0)

    lane = lax.broadcasted_iota(jnp.int32, (n_exp, LANES), 1)
    first = pad_start * (1.0 / EXPERT_BLOCK)
    nblocks = padded * (1.0 / EXPERT_BLOCK)
    grp_ref[...] = jnp.where(lane == 0, first, jnp.where(lane == 1, nblocks, 0.0)).astype(jnp.int32)


def _plan(idx, n_exp):
    t = idx.shape[1]
    assert t <= 256 * EXPERT_BLOCK
    dest, grp = pl.pallas_call(
        functools.partial(_plan_kernel, n_exp=n_exp),
        grid=(1,),
        in_specs=[pl.BlockSpec(idx.shape, lambda i: (0, 0))],
        out_specs=[pl.BlockSpec((t // LANES, TOP_K, LANES), lambda i: (0, 0, 0)),
                   pl.BlockSpec((n_exp, LANES), lambda i: (0, 0))],
        out_shape=[jax.ShapeDtypeStruct((t // LANES, TOP_K, LANES), jnp.int32),
                   jax.ShapeDtypeStruct((n_exp, LANES), jnp.int32)],
        scratch_shapes=[pltpu.VMEM((n_exp, t), F32)],
        compiler_params=_params("arbitrary"),
        name="dispatch_plan",
    )(idx)
    return dest, grp[:, 0], grp[:, 1]


def _dispatch_kernel(gs_ref, gn_ref, dest_ref, xp_ref, xs_hbm, zero_ref, sem, zsem):
    n_exp = gs_ref.shape[0]

    def zero_block(m):
        return pltpu.make_async_copy(zero_ref, xs_hbm.at[pl.ds(m * EXPERT_BLOCK, EXPERT_BLOCK), :], zsem)

    def zero_copy(e):
        return zero_block(gs_ref[e] + gn_ref[e] - 1)

    @pl.when(pl.program_id(0) == 0)
    def _():
        zero_ref[...] = jnp.zeros_like(zero_ref)
        used = gs_ref[n_exp - 1] + gn_ref[n_exp - 1]
        total = xs_hbm.shape[0] // EXPERT_BLOCK

        def start(e, _):
            @pl.when(gn_ref[e] > 0)
            def _():
                zero_copy(e).start()
            return 0

        def wait(e, _):
            @pl.when(gn_ref[e] > 0)
            def _():
                zero_copy(e).wait()
            return 0

        lax.fori_loop(0, n_exp, start, 0)
        lax.fori_loop(used, total, lambda m, _: (zero_block(m).start(), 0)[1], 0)
        lax.fori_loop(0, n_exp, wait, 0)
        lax.fori_loop(used, total, lambda m, _: (zero_block(m).wait(), 0)[1], 0)

    def row_copy(r, k):
        return pltpu.make_async_copy(xp_ref.at[pl.ds(r, 1), :], xs_hbm.at[pl.ds(dest_ref[k, r], 1), :], sem)

    def issue(r, _):
        for k in range(TOP_K):
            row_copy(r, k).start(priority=k % 2)
        return 0

    def drain(r, _):
        for k in range(TOP_K):
            row_copy(r, k).wait()
        return 0

    lax.fori_loop(0, LANES, issue, 0, unroll=8)
    lax.fori_loop(0, LANES, drain, 0)


def _dispatch(grp_start, grp_blocks, dest, xp, n_rows):
    t, w = xp.shape
    return pl.pallas_call(
        _dispatch_kernel,
        grid_spec=pltpu.PrefetchScalarGridSpec(
            num_scalar_prefetch=2,
            grid=(t // LANES,),
            in_specs=[pl.BlockSpec((None, TOP_K, LANES), lambda i, gs, gn: (i, 0, 0), memory_space=pltpu.SMEM),
                      pl.BlockSpec((LANES, w), lambda i, gs, gn: (i, 0))],
            out_specs=pl.BlockSpec(memory_space=pl.ANY),
            scratch_shapes=[pltpu.VMEM((EXPERT_BLOCK, w), xp.dtype), pltpu.SemaphoreType.DMA,
                            pltpu.SemaphoreType.DMA]),
        out_shape=jax.ShapeDtypeStruct((n_rows, w), xp.dtype),
        compiler_params=_params("arbitrary"),
        name="dispatch_rows",
    )(grp_start, grp_blocks, dest, xp)


def _weight_stage(w_hbms, stages, wsem, tn):
    n, e = pl.program_id(0), pl.program_id(1)
    n_tiles, n_exp = pl.num_programs(0), pl.num_programs(1)

    def copies(nn, ee):
        cols = pl.ds(pl.multiple_of(nn * tn, tn), tn)
        return [pltpu.make_async_copy(w.at[ee, :, cols], st, wsem.at[k])
                for k, (w, st) in enumerate(zip(w_hbms, stages))]

    @pl.when((n == 0) & (e == 0))
    def _():
        for c in copies(0, 0):
            c.start(priority=1)

    def wait_current():
        for c in copies(n, e):
            c.wait()

    def start_next():
        wrap = e == n_exp - 1

        @pl.when(jnp.logical_not(wrap & (n == n_tiles - 1)))
        def _():
            for c in copies(jnp.where(wrap, n + 1, n), jnp.where(wrap, 0, e + 1)):
                c.start(priority=1)

    return wait_current, start_next


def _group_loop(gs_ref, gn_ref, src_hbm, dst_hbm, col0, in_buf, out_buf, sem_in, sem_out, compute, after_first_fetch):
    e = pl.program_id(1)
    nb = gn_ref[e]
    b0 = gs_ref[e]
    width = out_buf.shape[2]
    cb = in_buf.shape[1] // EXPERT_BLOCK
    sizes = [cb >> s for s in range(cb.bit_length())]
    assert sizes[-1] == 1 and sum(sizes) == 2 * cb - 1, "blocks per chunk must be a power of two"
    nch = lax.shift_right_logical(nb, cb.bit_length() - 1) + sum((nb >> s) & 1 for s in range(cb.bit_length() - 1))

    def rows(start, nblk):
        return pl.ds(pl.multiple_of((b0 + start) * EXPERT_BLOCK, EXPERT_BLOCK), nblk * EXPERT_BLOCK)

    def fetch(start, slot):
        return pltpu.make_async_copy(src_hbm.at[rows(start, cb), :], in_buf.at[slot], sem_in.at[slot])

    def store(start, slot, nblk):
        return pltpu.make_async_copy(out_buf.at[slot, pl.ds(0, nblk * EXPERT_BLOCK), :],
                                     dst_hbm.at[rows(start, nblk), pl.ds(col0, width)], sem_out.at[slot])

    def piece(remaining):
        z = jnp.int32(1)
        for size in reversed(sizes[:-1]):
            z = jnp.where(remaining >= size, size, z)
        return z

    def for_size(count, fn):
        for size in sizes:
            @pl.when(count == size)
            def _():
                fn(size)

    @pl.when(e == 0)
    def _():
        last = gs_ref.shape[0] - 1
        used = gs_ref[last] + gn_ref[last]
        total = dst_hbm.shape[0] // EXPERT_BLOCK
        out_buf[0, 0:EXPERT_BLOCK, :] = jnp.zeros((EXPERT_BLOCK, width), out_buf.dtype)

        def tail(m):
            blk = pl.ds(pl.multiple_of(m * EXPERT_BLOCK, EXPERT_BLOCK), EXPERT_BLOCK)
            return pltpu.make_async_copy(out_buf.at[0, pl.ds(0, EXPERT_BLOCK), :],
                                         dst_hbm.at[blk, pl.ds(col0, width)], sem_out.at[0])

        lax.fori_loop(used, total, lambda m, _: (tail(m).start(), 0)[1], 0)
        lax.fori_loop(used, total, lambda m, _: (tail(m).wait(), 0)[1], 0)

    @pl.when(nch > 0)
    def _():
        fetch(0, 0).start()

    after_first_fetch()

    def wait_store(slot, size):
        for_size(size, lambda s: store(0, slot, s).wait())

    def step(i, carry):
        start, prev1, prev2 = carry
        size = piece(nb - start)
        slot = i & 1
        fetch(start, slot).wait()

        @pl.when(start + size < nb)
        def _():
            fetch(start + size, 1 - slot).start()

        @pl.when(i >= 2)
        def _():
            wait_store(slot, prev2)

        def run(s):
            n = s * EXPERT_BLOCK
            out_buf[slot, 0:n, :] = compute(in_buf[slot, 0:n, :])
            store(start, slot, s).start()

        for_size(size, run)
        return start + size, size, prev1

    _, last1, last2 = lax.fori_loop(0, nch, step, (jnp.int32(0), jnp.int32(0), jnp.int32(0)))

    @pl.when(nch >= 2)
    def _():
        wait_store(nch & 1, last2)

    @pl.when(nch >= 1)
    def _():
        wait_store((nch - 1) & 1, last1)


def _gateup_kernel(gs_ref, gn_ref, xs_hbm, wg_hbm, wu_hbm, bg_ref, bu_ref, h_hbm,
                   wg_st, wu_st, wgb_ref, wub_ref, xbuf, hbuf, wsem, sem_in, sem_out):
    tn = wgb_ref.shape[1]
    half = xbuf.shape[2]
    wait_weights, start_next_weights = _weight_stage((wg_hbm, wu_hbm), (wg_st, wu_st), wsem, tn)

    def stage_weights():
        wait_weights()

        @pl.when(gn_ref[pl.program_id(1)] > 0)
        def _():
            wgb_ref[...] = wg_st[...].astype(BF16)
            wub_ref[...] = wu_st[...].astype(BF16)

        start_next_weights()

    def compute(xp):
        xa = pltpu.bitcast(xp << 16, F32).astype(BF16)
        xb = pltpu.bitcast(xp & jnp.uint32(0xFFFF0000), F32).astype(BF16)
        gte = _dot(xa, wgb_ref[0:half, :]) + _dot(xb, wgb_ref[half:, :]) + bg_ref[...]
        up = _dot(xa, wub_ref[0:half, :]) + _dot(xb, wub_ref[half:, :]) + bu_ref[...]
        gte = jnp.minimum(gte, SWIGLU_LIMIT)
        up = jnp.clip(up, -SWIGLU_LIMIT, SWIGLU_LIMIT)
        return (gte * jax.nn.sigmoid(SWIGLU_ALPHA * gte) * (up + 1.0)).astype(BF16)

    col0 = pl.multiple_of(pl.program_id(0) * tn, tn)
    _group_loop(gs_ref, gn_ref, xs_hbm, h_hbm, col0, xbuf, hbuf, sem_in, sem_out, compute, stage_weights)


def _gateup(grp_start, grp_blocks, xs, w_gate, w_up, b_gate, b_up, tn):
    n_rows, half = xs.shape
    n_exp, d, f = w_gate.shape
    chunk = CHUNK_BLOCKS * EXPERT_BLOCK
    any_spec = pl.BlockSpec(memory_space=pl.ANY)
    b_spec = pl.BlockSpec((None, 1, tn), lambda n, e, gs, gn: (e, 0, n))
    return pl.pallas_call(
        _gateup_kernel,
        grid_spec=pltpu.PrefetchScalarGridSpec(
            num_scalar_prefetch=2,
            grid=(f // tn, n_exp),
            in_specs=[any_spec, any_spec, any_spec, b_spec, b_spec],
            out_specs=any_spec,
            scratch_shapes=[pltpu.VMEM((d, tn), F32), pltpu.VMEM((d, tn), F32),
                            pltpu.VMEM((d, tn), BF16), pltpu.VMEM((d, tn), BF16),
                            pltpu.VMEM((2, chunk, half), xs.dtype), pltpu.VMEM((2, chunk, tn), BF16),
                            pltpu.SemaphoreType.DMA((2,)), pltpu.SemaphoreType.DMA((2,)),
                            pltpu.SemaphoreType.DMA((2,))]),
        out_shape=jax.ShapeDtypeStruct((n_rows, f), BF16),
        compiler_params=_params("arbitrary", "arbitrary"),
        name="expert_gate_up",
    )(grp_start, grp_blocks, xs, w_gate, w_up, b_gate, b_up)


def _down_kernel(gs_ref, gn_ref, h_hbm, wd_hbm, bd_ref, y_hbm, wd_st, wdb_ref, hbuf, ybuf, wsem, sem_in, sem_out):
    tn = wdb_ref.shape[1]
    wait_weights, start_next_weights = _weight_stage((wd_hbm,), (wd_st,), wsem, tn)

    def stage_weights():
        wait_weights()

        @pl.when(gn_ref[pl.program_id(1)] > 0)
        def _():
            wdb_ref[...] = wd_st[...].astype(BF16)

        start_next_weights()

    def compute(h):
        y = (_dot(h, wdb_ref[...]) + bd_ref[...]).astype(BF16).astype(F32)
        bits = pltpu.bitcast(y, jnp.uint32)
        return (bits[:, :tn // 2] >> 16) | bits[:, tn // 2:]

    col0 = pl.multiple_of(pl.program_id(0) * (tn // 2), tn // 2)
    _group_loop(gs_ref, gn_ref, h_hbm, y_hbm, col0, hbuf, ybuf, sem_in, sem_out, compute, stage_weights)


def _down(grp_start, grp_blocks, h, w_down, b_down, tn):
    n_rows, f = h.shape
    n_exp, _, d = w_down.shape
    return pl.pallas_call(
        _down_kernel,
        grid_spec=pltpu.PrefetchScalarGridSpec(
            num_scalar_prefetch=2,
            grid=(d // tn, n_exp),
            in_specs=[pl.BlockSpec(memory_space=pl.ANY),
                      pl.BlockSpec(memory_space=pl.ANY),
                      pl.BlockSpec((None, 1, tn), lambda n, e, gs, gn: (e, 0, n))],
            out_specs=pl.BlockSpec(memory_space=pl.ANY),
            scratch_shapes=[pltpu.VMEM((f, tn), F32), pltpu.VMEM((f, tn), BF16),
                            pltpu.VMEM((2, DOWN_CHUNK_BLOCKS * EXPERT_BLOCK, f), h.dtype),
                            pltpu.VMEM((2, DOWN_CHUNK_BLOCKS * EXPERT_BLOCK, tn // 2), jnp.uint32),
                            pltpu.SemaphoreType.DMA((1,)), pltpu.SemaphoreType.DMA((2,)),
                            pltpu.SemaphoreType.DMA((2,))]),
        out_shape=jax.ShapeDtypeStruct((n_rows, d // 2), jnp.uint32),
        compiler_params=_params("arbitrary", "arbitrary"),
        name="expert_down",
    )(grp_start, grp_blocks, h, w_down, b_down)


def _combine_kernel(dest_ref, dest_next_ref, wt_ref, h_ref, y_hbm, o_ref, buf_a, buf_b, wb_ref, sem, *, tile_cols):
    i = pl.program_id(0)
    n_cols = y_hbm.shape[1]
    half = tile_cols // 2
    n_chunks = n_cols // LANES
    rows_per_chunk = LANES // n_chunks

    def row_copy(idx_ref, t, buf, s, r, k):
        return pltpu.make_async_copy(y_hbm.at[pl.ds(idx_ref[t, k, r], 1), :], buf.at[k, pl.ds(r, 1), :], sem.at[s])

    def drain(idx_ref, t, buf, s):
        def wait(r, _):
            for k in range(TOP_K):
                row_copy(idx_ref, t, buf, s, r, k).wait()
            return 0

        lax.fori_loop(0, LANES, wait, 0)

    def combine_tile(t, buf, issue_next):
        rows = slice(t * LANES, (t + 1) * LANES)
        wt = wt_ref[:, rows]
        for k in range(TOP_K):
            wb_ref[k] = jnp.broadcast_to(wt[k:k + 1, :], (LANES, LANES)).T
        for c in range(n_chunks):
            col = (c * LANES // half) * tile_cols + (c * LANES) % half
            lo = h_ref[rows, col:col + LANES]
            hi = h_ref[rows, col + half:col + half + LANES]
            for k in range(TOP_K):
                u = buf[k, :, c * LANES:(c + 1) * LANES]
                lo = lo + wb_ref[k] * pltpu.bitcast(u << 16, F32)
                hi = hi + wb_ref[k] * pltpu.bitcast(u & jnp.uint32(0xFFFF0000), F32)
            o_ref[rows, col:col + LANES] = lo
            o_ref[rows, col + half:col + half + LANES] = hi
            for r in range(c * rows_per_chunk, (c + 1) * rows_per_chunk):
                for k in range(TOP_K):
                    issue_next(r, k)

    @pl.when(i == 0)
    def _():
        def issue(r, _):
            for k in range(TOP_K):
                row_copy(dest_ref, 0, buf_a, 0, r, k).start(priority=k % 2)
            return 0

        lax.fori_loop(0, LANES, issue, 0)

    drain(dest_ref, 0, buf_a, 0)
    combine_tile(0, buf_a, lambda r, k: row_copy(dest_ref, 1, buf_b, 1, r, k).start(priority=k % 2))
    drain(dest_ref, 1, buf_b, 1)
    combine_tile(1, buf_b, lambda r, k: row_copy(dest_next_ref, 0, buf_a, 0, r, k).start(priority=k % 2))

    @pl.when(i == pl.num_programs(0) - 1)
    def _():
        drain(dest_next_ref, 0, buf_a, 0)


def _combine(dest, wts, h1, y, tile_cols):
    t, d = h1.shape
    nt = t // LANES
    assert nt % 2 == 0 and LANES % (y.shape[1] // LANES) == 0
    smem = pltpu.SMEM
    return pl.pallas_call(
        functools.partial(_combine_kernel, tile_cols=tile_cols),
        grid=(nt // 2,),
        in_specs=[pl.BlockSpec((2, TOP_K, LANES), lambda i: (i, 0, 0), memory_space=smem),
                  pl.BlockSpec((1, TOP_K, LANES), lambda i: (jnp.minimum(2 * i + 2, nt - 1), 0, 0), memory_space=smem),
                  pl.BlockSpec((TOP_K, 2 * LANES), lambda i: (0, i)),
                  pl.BlockSpec((2 * LANES, d), lambda i: (i, 0)),
                  pl.BlockSpec(memory_space=pl.ANY)],
        out_specs=pl.BlockSpec((2 * LANES, d), lambda i: (i, 0)),
        out_shape=jax.ShapeDtypeStruct((t, d), F32),
        scratch_shapes=[pltpu.VMEM((TOP_K, LANES, y.shape[1]), y.dtype),
                        pltpu.VMEM((TOP_K, LANES, y.shape[1]), y.dtype),
                        pltpu.VMEM((TOP_K, LANES, LANES), F32),
                        pltpu.SemaphoreType.DMA((2,))],
        compiler_params=_params("arbitrary"),
        name="combine_rows",
    )(dest, dest, wts, h1, y)


def _layer(x, meta, norm_mix, w_in, b_forget, b_branch_gate, q_norm, k_norm, w_pool_group, pool_scale,
           w_branch_pool, w_branch_attn, w_out, norm_ffn, w_router, b_router, w_gate, b_gate, w_up, b_up,
           w_down, b_down):
    batch, seq, d = x.shape
    t = batch * seq
    n_meta = meta.shape[0]
    pool_w = w_pool_group.shape[0] * w_pool_group.shape[1]
    attn_w = w_branch_attn.shape[0]
    n_heads = attn_w // HEAD_DIM
    n_exp = w_router.shape[1]
    col_q = pool_w
    col_zf = pool_w + 3 * attn_w
    col_zg = col_zf + n_heads
    assert n_heads <= LANES and col_zf % LANES == 0

    tm = min(1024, t)
    x2 = x.reshape(t, d)

    hn = _rmsnorm(x2, norm_mix.reshape(1, d), min(256, t))
    hn_meta = _rmsnorm(meta, norm_mix.reshape(1, d), n_meta)

    w_in_t = w_in.T
    uqkv, uqkv_meta = _proj(hn, hn_meta, w_in_t, None, col0=0, ncols=col_zf, tn=512, tm=tm,
                            out_dtype=BF16, name="in_proj_uqkv")
    zf, zf_meta = _proj(hn, hn_meta, w_in_t, None, col0=col_zf, ncols=LANES, tn=LANES, tm=tm,
                        out_dtype=F32, name="in_proj_forget")
    gates = _proj(hn, None, w_in_t, b_branch_gate.reshape(1, 2 * d), col0=col_zf, ncols=2 * d, tn=512, tm=tm,
                  out_dtype=BF16, name="in_proj_gates", shift=col_zg - col_zf)

    bf_pad = jnp.pad(b_forget.reshape(1, n_heads), ((0, 0), (0, LANES - n_heads)))
    zfm_pad = jnp.pad(zf_meta, ((0, LANES - n_meta), (0, 0)))
    cr, mbias = _cumsum(zf, zfm_pad, bf_pad, batch, n_heads, n_meta)

    y_pool = _pool(uqkv, uqkv_meta, w_pool_group, pool_scale.reshape(1, pool_w), batch, min(512, seq))
    y_attn = _attention(uqkv, uqkv_meta, cr, mbias, q_norm.reshape(1, HEAD_DIM), k_norm.reshape(1, HEAD_DIM),
                        batch, n_heads, col_q, min(512, seq))

    merged = _merge(y_pool, y_attn, w_branch_pool, w_branch_attn, gates, tm, 512)
    h1 = _outproj(merged, w_out, x2, tm, 512)

    xp, idx, wts = _router(h1, norm_ffn.reshape(1, d), w_router.T, b_router.reshape(n_exp, 1), min(256, t))
    n_blocks = -(-(t * TOP_K + n_exp * (EXPERT_BLOCK - 1)) // EXPERT_BLOCK) + max(CHUNK_BLOCKS, DOWN_CHUNK_BLOCKS) - 1
    dest, grp_start, grp_blocks = _plan(idx, n_exp)
    xs = _dispatch(grp_start, grp_blocks, dest, xp, n_blocks * EXPERT_BLOCK)
    f = w_gate.shape[2]
    hmid = _gateup(grp_start, grp_blocks, xs, w_gate, w_up, b_gate.reshape(n_exp, 1, f), b_up.reshape(n_exp, 1, f), 512)
    down_tn = 2048
    y = _down(grp_start, grp_blocks, hmid, w_down, b_down.reshape(n_exp, 1, d), down_tn)
    out = _combine(dest, wts, h1, y, down_tn)
    return out.reshape(batch, seq, d)


def kernel(x, meta_tokens, norm_mix, w_in, b_forget, b_branch_gate, q_norm, k_norm, w_pool_group, pool_scale,
           w_branch_pool, w_branch_attn, w_out, norm_ffn, w_router, b_router, w_gate, b_gate, w_up, b_up,
           w_down, b_down):
    depth = norm_mix.shape[0]
    assert depth == 1, "the fused layer pipeline drops the meta rows after the (single) layer"
    return _layer(x, meta_tokens, norm_mix[0], w_in[0], b_forget[0], b_branch_gate[0], q_norm[0], k_norm[0],
                  w_pool_group[0], pool_scale[0], w_branch_pool[0], w_branch_attn[0], w_out[0], norm_ffn[0],
                  w_router[0], b_router[0], w_gate[0], b_gate[0], w_up[0], b_up[0], w_down[0], b_down[0])
```

```python
import functools

import jax
import jax.numpy as jnp
from jax import lax
from jax.experimental import pallas as pl
from jax.experimental.pallas import tpu as pltpu

NORM_EPS = 1e-6
HEAD_DIM = 128
POOL_WINDOWS = (2, 4, 8, 16)
TOP_K = 4
SWIGLU_LIMIT = 7.0
SWIGLU_ALPHA = 1.702
EXPERT_BLOCK = 128
CHUNK_BLOCKS = 4
DOWN_CHUNK_BLOCKS = 4
LANES = 128
NEG_BIG = -1e30
LOG2E = 1.4426950408889634
ONES_ROWS = 16
HEAD_GROUP = 4
VMEM_LIMIT = 56 * 1024 * 1024

F32 = jnp.float32
BF16 = jnp.bfloat16


def _params(*sem, vmem=VMEM_LIMIT):
    return pltpu.CompilerParams(dimension_semantics=sem, vmem_limit_bytes=vmem)


def _dot(a, b):
    return jnp.dot(a, b, preferred_element_type=F32)


def _dot_nt(a, b):
    return lax.dot_general(a, b, (((1,), (1,)), ((), ())), preferred_element_type=F32)


def _rms(x, gain):
    ms = jnp.mean(x * x, axis=-1, keepdims=True)
    return x * lax.rsqrt(ms + NORM_EPS) * gain


def _rmsnorm_kernel(x_ref, g_ref, o_ref):
    o_ref[...] = _rms(x_ref[...], g_ref[...]).astype(o_ref.dtype)


def _rmsnorm(x, gain, tm):
    rows, d = x.shape
    return pl.pallas_call(
        _rmsnorm_kernel,
        grid=(rows // tm,),
        in_specs=[pl.BlockSpec((tm, d), lambda i: (i, 0)), pl.BlockSpec((1, d), lambda i: (0, 0))],
        out_specs=pl.BlockSpec((tm, d), lambda i: (i, 0)),
        out_shape=jax.ShapeDtypeStruct((rows, d), BF16),
        compiler_params=_params("arbitrary"),
        name="rmsnorm",
    )(x, gain)


def _proj_kernel(*refs, has_meta, gate_bias, shift):
    it = iter(refs)
    x_ref = next(it)
    xm_ref = next(it) if has_meta else None
    w_ref = next(it)
    wn_ref = next(it) if shift else None
    b_ref = next(it) if gate_bias else None
    o_ref = next(it)
    om_ref = next(it) if has_meta else None
    wbf_ref = next(it)
    tn, k = w_ref.shape

    @pl.when(pl.program_id(1) == 0)
    def _():
        def chunk(c, _):
            off = pl.multiple_of(c * LANES, LANES)
            if shift:
                wt = jnp.concatenate([w_ref[shift:tn, pl.ds(off, LANES)], wn_ref[:, pl.ds(off, LANES)]], axis=0)
            else:
                wt = w_ref[:, pl.ds(off, LANES)]
            wbf_ref[pl.ds(off, LANES), :] = wt.T.astype(BF16)
            return 0

        lax.fori_loop(0, k // LANES, chunk, 0, unroll=4)
        if has_meta:
            om_ref[...] = _dot(xm_ref[...], wbf_ref[...]).astype(om_ref.dtype)

    acc = _dot(x_ref[...], wbf_ref[...])
    if gate_bias:
        acc = jax.nn.sigmoid(acc + b_ref[...])
    o_ref[...] = acc.astype(o_ref.dtype)


def _proj(x, xm, wt, bias, *, col0, ncols, tn, tm, out_dtype, name, shift=0):
    rows, k = x.shape
    assert col0 % tn == 0 and ncols % tn == 0 and rows % tm == 0
    jb = col0 // tn
    in_specs = [pl.BlockSpec((tm, k), lambda j, i: (i, 0))]
    args = [x]
    if xm is not None:
        in_specs.append(pl.BlockSpec(xm.shape, lambda j, i: (0, 0)))
        args.append(xm)
    in_specs.append(pl.BlockSpec((tn, k), lambda j, i: (jb + j, 0)))
    args.append(wt)
    if shift:
        assert shift % 8 == 0 and tn % shift == 0 and (col0 + ncols + shift) <= wt.shape[0]
        per = tn // shift
        in_specs.append(pl.BlockSpec((shift, k), lambda j, i: ((jb + j + 1) * per, 0)))
        args.append(wt)
    if bias is not None:
        in_specs.append(pl.BlockSpec((1, tn), lambda j, i: (0, j)))
        args.append(bias)
    out_specs = [pl.BlockSpec((tm, tn), lambda j, i: (i, j))]
    out_shape = [jax.ShapeDtypeStruct((rows, ncols), out_dtype)]
    if xm is not None:
        out_specs.append(pl.BlockSpec((xm.shape[0], tn), lambda j, i: (0, j)))
        out_shape.append(jax.ShapeDtypeStruct((xm.shape[0], ncols), out_dtype))
    res = pl.pallas_call(
        functools.partial(_proj_kernel, has_meta=xm is not None, gate_bias=bias is not None, shift=shift),
        grid=(ncols // tn, rows // tm),
        in_specs=in_specs,
        out_specs=out_specs,
        out_shape=out_shape,
        scratch_shapes=[pltpu.VMEM((k, tn), BF16)],
        compiler_params=_params("arbitrary", "arbitrary"),
        name=name,
    )(*args)
    return res if xm is not None else res[0]


def _cumsum_kernel(zf_ref, zfm_ref, bf_ref, cr_ref, mb_ref, *, n_meta):
    seq = zf_ref.shape[0]
    bf = bf_ref[...]
    r = lax.broadcasted_iota(jnp.int32, (LANES, LANES), 0)
    c = lax.broadcasted_iota(jnp.int32, (LANES, LANES), 1)
    tri = (c <= r).astype(F32)
    sfx = (c > r).astype(F32)

    def prefix(mat, x):
        return jnp.dot(mat, x, preferred_element_type=F32, precision=lax.Precision.HIGHEST)

    n_meta_pad = zfm_ref.shape[0]
    lfm = jax.nn.log_sigmoid(zfm_ref[...] + bf)
    rows = lax.broadcasted_iota(jnp.int32, (n_meta_pad, LANES), 0)
    lfm = jnp.where(rows < n_meta, lfm, 0.0)
    mbias = jnp.where(rows < n_meta, prefix(sfx, lfm), NEG_BIG)
    mbias_t = mbias.T
    for h in range(mb_ref.shape[0]):
        mb_ref[h] = mbias_t[h:h + 1, :]

    def body(blk, carry):
        off = pl.multiple_of(blk * LANES, LANES)
        lf = jax.nn.log_sigmoid(zf_ref[pl.ds(off, LANES), :] + bf)
        cs = prefix(tri, lf) + carry
        cs_t = cs.T
        for h in range(cr_ref.shape[0]):
            cr_ref[h, :, pl.ds(off, LANES)] = cs_t[h:h + 1, :]
        return cs[LANES - 1:LANES, :]

    lax.fori_loop(0, seq // LANES, body, jnp.zeros((1, LANES), F32))


def _cumsum(zf, zfm_pad, bf_pad, batch, n_heads, n_meta):
    t = zf.shape[0]
    seq = t // batch
    return pl.pallas_call(
        functools.partial(_cumsum_kernel, n_meta=n_meta),
        grid=(batch,),
        in_specs=[pl.BlockSpec((seq, LANES), lambda b: (b, 0)),
                  pl.BlockSpec(zfm_pad.shape, lambda b: (0, 0)),
                  pl.BlockSpec((1, LANES), lambda b: (0, 0))],
        out_specs=[pl.BlockSpec((None, n_heads, 1, seq), lambda b: (b, 0, 0, 0)),
                   pl.BlockSpec((n_heads, 1, LANES), lambda b: (0, 0, 0))],
        out_shape=[jax.ShapeDtypeStruct((batch, n_heads, 1, seq), F32),
                   jax.ShapeDtypeStruct((n_heads, 1, LANES), F32)],
        compiler_params=_params("arbitrary"),
        name="forget_cumsum",
    )(zf, zfm_pad, bf_pad)


def _pool_kernel(u_ref, um_ref, wp_ref, ps_ref, o_ref, ext_ref, wbf_ref):
    ts = u_ref.shape[0]
    halo = um_ref.shape[0]
    gw = wp_ref.shape[1]
    i = pl.program_id(1)

    @pl.when((pl.program_id(0) == 0) & (i == 0))
    def _():
        wbf_ref[...] = wp_ref[...].astype(BF16)

    @pl.when(i == 0)
    def _():
        ext_ref[0:halo, :] = um_ref[...].astype(F32)

    @pl.when(i > 0)
    def _():
        ext_ref[0:halo, :] = ext_ref[ts:ts + halo, :]

    ext_ref[halo:halo + ts, :] = u_ref[...].astype(F32)
    for g, w in enumerate(POOL_WINDOWS):
        cols = slice(g * gw, (g + 1) * gw)
        uf = ext_ref[halo:halo + ts, cols]
        acc = uf
        for d in range(1, w):
            acc = acc + ext_ref[halo - d:halo - d + ts, cols]
        pooled = acc * (1.0 / w) - uf
        y = _dot(pooled.astype(BF16), wbf_ref[g]) * ps_ref[:, cols]
        o_ref[:, cols] = y.astype(o_ref.dtype)


def _pool(uqkv, uqkv_meta, w_pool, pool_scale, batch, ts):
    t = uqkv.shape[0]
    seq = t // batch
    ng, gw, _ = w_pool.shape
    pw = ng * gw
    halo = uqkv_meta.shape[0]
    assert halo >= max(POOL_WINDOWS) and seq % ts == 0
    nt = seq // ts
    return pl.pallas_call(
        _pool_kernel,
        grid=(batch, nt),
        in_specs=[pl.BlockSpec((ts, pw), lambda b, i: (b * nt + i, 0)),
                  pl.BlockSpec((halo, pw), lambda b, i: (0, 0)),
                  pl.BlockSpec(w_pool.shape, lambda b, i: (0, 0, 0)),
                  pl.BlockSpec((1, pw), lambda b, i: (0, 0))],
        out_specs=pl.BlockSpec((ts, pw), lambda b, i: (b * nt + i, 0)),
        out_shape=jax.ShapeDtypeStruct((t, pw), BF16),
        scratch_shapes=[pltpu.VMEM((halo + ts, pw), F32), pltpu.VMEM(w_pool.shape, BF16)],
        compiler_params=_params("arbitrary", "arbitrary"),
        name="pool_mixer",
    )(uqkv, uqkv_meta, w_pool, pool_scale)


def _attn_kernel(q_ref, k_ref, v_ref, km_ref, vm_ref, cr_ref, mb_ref, qg_ref, kg_ref, o_ref,
                 ka_ref, va_ref, kma_ref, vma_ref):
    tq = q_ref.shape[0]
    seq = k_ref.shape[0]
    n_meta = km_ref.shape[0]
    n_group, va_rows = va_ref.shape[0], va_ref.shape[1]
    qi = pl.program_id(2)
    lane = lax.broadcasted_iota(jnp.int32, (LANES, LANES), 1)
    heads = [(g, slice(g * HEAD_DIM, (g + 1) * HEAD_DIM)) for g in range(n_group)]

    def bias_pieces(col):
        hi = col.astype(BF16).astype(F32)
        mid = (col - hi).astype(BF16).astype(F32)
        lo = (col - hi - mid).astype(BF16).astype(F32)
        return jnp.where(lane == 0, hi, jnp.where(lane == 1, mid, jnp.where(lane == 2, lo, 0.0))).astype(BF16)

    ones_row = (lax.broadcasted_iota(jnp.int32, (va_rows - HEAD_DIM, LANES), 0) == 0).astype(F32)

    @pl.when(qi == 0)
    def _():
        def chunk(i, _):
            off = pl.multiple_of(i * LANES, LANES)
            for g, cols in heads:
                vt = v_ref[pl.ds(off, LANES), cols].astype(F32).T
                va_ref[g, :, pl.ds(off, LANES)] = jnp.concatenate([vt, ones_row], axis=0).astype(BF16)
                col = jnp.broadcast_to(cr_ref[g, :, pl.ds(off, LANES)] * LOG2E, (LANES, LANES)).T
                ka_ref[g, pl.ds(off, LANES), HEAD_DIM:] = bias_pieces(col)
            return 0

        lax.fori_loop(0, seq // LANES, chunk, 0, unroll=2)
        pad = jnp.zeros((LANES - n_meta, HEAD_DIM), F32)
        for g, cols in heads:
            ka_ref[g, :, 0:HEAD_DIM] = _rms(k_ref[:, cols].astype(F32), kg_ref[...]).astype(BF16)
            kmn = jnp.concatenate([_rms(km_ref[:, cols].astype(F32), kg_ref[...]), pad], axis=0)
            kma_ref[g, :, 0:HEAD_DIM] = kmn.astype(BF16)
            kma_ref[g, :, HEAD_DIM:] = bias_pieces(-jnp.broadcast_to(mb_ref[g] * LOG2E, (LANES, LANES)).T)
            vmt = jnp.concatenate([vm_ref[:, cols].astype(F32), pad], axis=0).T
            vma_ref[g] = jnp.concatenate([vmt, ones_row], axis=0).astype(BF16)

    minus_ones = jnp.where(lax.broadcasted_iota(jnp.int32, (tq, LANES), 1) < 3, -1.0, 0.0).astype(BF16)
    qa = [jnp.concatenate([(_rms(q_ref[:, cols].astype(F32), qg_ref[...]) * (HEAD_DIM ** -0.5 * LOG2E)).astype(BF16),
                           minus_ones], axis=1) for g, cols in heads]
    q0 = pl.multiple_of(qi * tq, tq)

    def update(carry, st, va_blk):
        m, acc = carry
        m_new = jnp.maximum(m, jnp.max(st, axis=0, keepdims=True))
        p = jnp.exp2(st - m_new)
        acc = jnp.exp2(m - m_new) * acc + _dot(va_blk, p.astype(BF16))
        return m_new, acc

    def scores(g, off, size):
        return _dot_nt(ka_ref[g, pl.ds(off, size), :], qa[g])

    def body(j, carry):
        off = pl.multiple_of(j * (2 * tq), 2 * tq)
        return tuple(update(carry[g], scores(g, off, 2 * tq), va_ref[g, :, pl.ds(off, 2 * tq)]) for g, _ in heads)

    carry = tuple((jnp.full((1, tq), NEG_BIG, F32), jnp.zeros((va_rows, tq), F32)) for _ in heads)
    carry = lax.fori_loop(0, lax.shift_right_logical(qi, 1), body, carry)

    def finish(with_prev):
        key = lax.broadcasted_iota(jnp.int32, (tq, tq), 0)
        qry = lax.broadcasted_iota(jnp.int32, (tq, tq), 1)
        for g, cols in heads:
            sts = [_dot_nt(kma_ref[g], qa[g])]
            vas = [vma_ref[g]]
            if with_prev:
                prev = pl.multiple_of(q0 - tq, tq)
                sts.append(scores(g, prev, tq))
                vas.append(va_ref[g, :, pl.ds(prev, tq)])
            sts.append(jnp.where(key <= qry, scores(g, q0, tq), NEG_BIG))
            vas.append(va_ref[g, :, pl.ds(q0, tq)])
            m, acc = update(carry[g], jnp.concatenate(sts, axis=0), jnp.concatenate(vas, axis=1))
            o_ref[:, cols] = (acc[0:HEAD_DIM, :] / acc[HEAD_DIM:HEAD_DIM + 1, :]).T.astype(o_ref.dtype)

    odd = (qi & 1) == 1

    @pl.when(odd)
    def _():
        finish(True)

    @pl.when(jnp.logical_not(odd))
    def _():
        finish(False)


def _attention(uqkv, uqkv_meta, cr, mbias, q_gain, k_gain, batch, n_heads, col_q, tq):
    t = uqkv.shape[0]
    seq = t // batch
    n_meta = uqkv_meta.shape[0]
    nq = seq // tq
    grp = HEAD_GROUP
    gw = grp * HEAD_DIM
    assert n_heads % grp == 0 and col_q % gw == 0 and (n_heads * HEAD_DIM) % gw == 0
    bq = col_q // gw
    bk = bq + n_heads // grp
    bv = bk + n_heads // grp
    return pl.pallas_call(
        _attn_kernel,
        grid=(batch, n_heads // grp, nq),
        in_specs=[pl.BlockSpec((tq, gw), lambda b, h, i: (b * nq + i, bq + h)),
                  pl.BlockSpec((seq, gw), lambda b, h, i: (b, bk + h)),
                  pl.BlockSpec((seq, gw), lambda b, h, i: (b, bv + h)),
                  pl.BlockSpec((n_meta, gw), lambda b, h, i: (0, bk + h)),
                  pl.BlockSpec((n_meta, gw), lambda b, h, i: (0, bv + h)),
                  pl.BlockSpec((None, grp, 1, seq), lambda b, h, i: (b, h, 0, 0)),
                  pl.BlockSpec((grp, 1, LANES), lambda b, h, i: (h, 0, 0)),
                  pl.BlockSpec((1, HEAD_DIM), lambda b, h, i: (0, 0)),
                  pl.BlockSpec((1, HEAD_DIM), lambda b, h, i: (0, 0))],
        out_specs=pl.BlockSpec((tq, gw), lambda b, h, i: (b * nq + i, h)),
        out_shape=jax.ShapeDtypeStruct((t, n_heads * HEAD_DIM), BF16),
        scratch_shapes=[pltpu.VMEM((grp, seq, HEAD_DIM + LANES), BF16),
                        pltpu.VMEM((grp, HEAD_DIM + ONES_ROWS, seq), BF16),
                        pltpu.VMEM((grp, LANES, HEAD_DIM + LANES), BF16),
                        pltpu.VMEM((grp, HEAD_DIM + ONES_ROWS, LANES), BF16)],
        compiler_params=_params("arbitrary", "arbitrary", "arbitrary"),
        name="forgetting_attention",
    )(uqkv, uqkv, uqkv, uqkv_meta, uqkv_meta, cr, mbias, q_gain, k_gain)


def _merge_kernel(yp_ref, ya_ref, wp_ref, wa_ref, gp_ref, ga_ref, o_ref, wpb_ref, wab_ref):
    @pl.when(pl.program_id(1) == 0)
    def _():
        wpb_ref[...] = wp_ref[...].astype(BF16)
        wab_ref[...] = wa_ref[...].astype(BF16)

    a = _dot(yp_ref[...], wpb_ref[...])
    b = _dot(ya_ref[...], wab_ref[...])
    o_ref[...] = (gp_ref[...].astype(F32) * a + ga_ref[...].astype(F32) * b).astype(o_ref.dtype)


def _merge(y_pool, y_attn, w_bp, w_ba, gates, tm, tn):
    t, kp = y_pool.shape
    ka = y_attn.shape[1]
    d = w_bp.shape[1]
    nj = d // tn
    return pl.pallas_call(
        _merge_kernel,
        grid=(nj, t // tm),
        in_specs=[pl.BlockSpec((tm, kp), lambda j, i: (i, 0)),
                  pl.BlockSpec((tm, ka), lambda j, i: (i, 0)),
                  pl.BlockSpec((kp, tn), lambda j, i: (0, j)),
                  pl.BlockSpec((ka, tn), lambda j, i: (0, j)),
                  pl.BlockSpec((tm, tn), lambda j, i: (i, j)),
                  pl.BlockSpec((tm, tn), lambda j, i: (i, nj + j))],
        out_specs=pl.BlockSpec((tm, tn), lambda j, i: (i, j)),
        out_shape=jax.ShapeDtypeStruct((t, d), BF16),
        scratch_shapes=[pltpu.VMEM((kp, tn), BF16), pltpu.VMEM((ka, tn), BF16)],
        compiler_params=_params("arbitrary", "arbitrary"),
        name="branch_merge",
    )(y_pool, y_attn, w_bp, w_ba, gates, gates)


def _outproj_kernel(m_ref, w_ref, x_ref, o_ref, wbf_ref):
    @pl.when(pl.program_id(1) == 0)
    def _():
        wbf_ref[...] = w_ref[...].astype(BF16)

    o_ref[...] = x_ref[...] + _dot(m_ref[...], wbf_ref[...])


def _outproj(merged, w_out, x, tm, tn):
    t, k = merged.shape
    d = w_out.shape[1]
    return pl.pallas_call(
        _outproj_kernel,
        grid=(d // tn, t // tm),
        in_specs=[pl.BlockSpec((tm, k), lambda j, i: (i, 0)),
                  pl.BlockSpec((k, tn), lambda j, i: (0, j)),
                  pl.BlockSpec((tm, tn), lambda j, i: (i, j))],
        out_specs=pl.BlockSpec((tm, tn), lambda j, i: (i, j)),
        out_shape=jax.ShapeDtypeStruct((t, d), F32),
        scratch_shapes=[pltpu.VMEM((k, tn), BF16)],
        compiler_params=_params("arbitrary", "arbitrary"),
        name="out_proj",
    )(merged, w_out, x)


def _router_kernel(h_ref, g_ref, wr_ref, br_ref, xp_ref, idx_ref, wt_ref):
    y = _rms(h_ref[...], g_ref[...])
    y_hi = y.astype(BF16)
    y_lo = (y - y_hi.astype(F32)).astype(BF16)
    w = wr_ref[...]
    w_hi = w.astype(BF16)
    w_lo = (w - w_hi.astype(F32)).astype(BF16)
    logits = _dot_nt(w_hi, y_hi) + (_dot_nt(w_hi, y_lo) + _dot_nt(w_lo, y_hi)) + br_ref[...]

    half = y.shape[1] // 2
    bits = pltpu.bitcast(y_hi.astype(F32), jnp.uint32)
    xp_ref[...] = (bits[:, :half] >> 16) | bits[:, half:]

    n_exp = logits.shape[0]
    eid = lax.broadcasted_iota(jnp.int32, logits.shape, 0)
    vals = logits
    top_v, top_i = [], []
    for _ in range(TOP_K):
        mx = jnp.max(vals, axis=0, keepdims=True)
        sel = jnp.min(jnp.where(vals == mx, eid, n_exp), axis=0, keepdims=True)
        top_v.append(mx)
        top_i.append(sel)
        vals = jnp.where(eid == sel, -jnp.inf, vals)
    ex = [jnp.exp(v - top_v[0]) for v in top_v]
    den = ex[0] + ex[1] + ex[2] + ex[3]
    for k in range(TOP_K):
        idx_ref[k:k + 1, :] = top_i[k]
        wt_ref[k:k + 1, :] = ex[k] / den


def _router(h1, gain, w_router_t, b_router, tm):
    t, d = h1.shape
    n_exp = w_router_t.shape[0]
    return pl.pallas_call(
        _router_kernel,
        grid=(t // tm,),
        in_specs=[pl.BlockSpec((tm, d), lambda i: (i, 0)),
                  pl.BlockSpec((1, d), lambda i: (0, 0)),
                  pl.BlockSpec((n_exp, d), lambda i: (0, 0)),
                  pl.BlockSpec((n_exp, 1), lambda i: (0, 0))],
        out_specs=[pl.BlockSpec((tm, d // 2), lambda i: (i, 0)),
                   pl.BlockSpec((TOP_K, tm), lambda i: (0, i)),
                   pl.BlockSpec((TOP_K, tm), lambda i: (0, i))],
        out_shape=[jax.ShapeDtypeStruct((t, d // 2), jnp.uint32),
                   jax.ShapeDtypeStruct((TOP_K, t), jnp.int32),
                   jax.ShapeDtypeStruct((TOP_K, t), F32)],
        compiler_params=_params("arbitrary"),
        name="router_topk",
    )(h1, gain, w_router_t, b_router)


def _plan_kernel(idx_ref, dest_ref, grp_ref, rank_ref, *, n_exp):
    t = idx_ref.shape[1]
    nblk = t // LANES
    eid = lax.broadcasted_iota(jnp.int32, (n_exp, LANES), 0)
    r = lax.broadcasted_iota(jnp.int32, (LANES, LANES), 0)
    c = lax.broadcasted_iota(jnp.int32, (LANES, LANES), 1)
    upper = (r <= c).astype(BF16)

    def member(off):
        m = jnp.zeros((n_exp, LANES), F32)
        for k in range(TOP_K):
            m = m + (idx_ref[k:k + 1, pl.ds(off, LANES)] == eid).astype(F32)
        return m

    def count(blk, carry):
        off = pl.multiple_of(blk * LANES, LANES)
        m = member(off)
        cs = _dot(m.astype(BF16), upper) + carry
        rank_ref[:, pl.ds(off, LANES)] = cs - m
        return carry + jnp.sum(m, axis=1, keepdims=True)

    counts = lax.fori_loop(0, nblk, count, jnp.zeros((n_exp, 1), F32))
    padded = jnp.floor((counts + (EXPERT_BLOCK - 1)) * (1.0 / EXPERT_BLOCK)) * EXPERT_BLOCK
    er = lax.broadcasted_iota(jnp.int32, (n_exp, n_exp), 0)
    ec = lax.broadcasted_iota(jnp.int32, (n_exp, n_exp), 1)
    lower = (ec <= er).astype(BF16)
    pad_end = _dot(lower, jnp.broadcast_to(padded, (n_exp, LANES)).astype(BF16))[:, 0:1]
    pad_start = pad_end - padded

    def place(blk, _):
        off = pl.multiple_of(blk * LANES, LANES)
        pos = pad_start + rank_ref[:, pl.ds(off, LANES)]
        for k in range(TOP_K):
            sel = idx_ref[k:k + 1, pl.ds(off, LANES)] == eid
            d = jnp.sum(jnp.where(sel, pos, 0.0), axis=0, keepdims=True)
            dest_ref[blk, k:k + 1, :] = d.astype(jnp.int32)
        return 0

    lax.fori_loop(0, nblk, place, 0)

    lane = lax.broadcasted_iota(jnp.int32, (n_exp, LANES), 1)
    first = pad_start * (1.0 / EXPERT_BLOCK)
    nblocks = padded * (1.0 / EXPERT_BLOCK)
    grp_ref[...] = jnp.where(lane == 0, first, jnp.where(lane == 1, nblocks, 0.0)).astype(jnp.int32)


def _plan(idx, n_exp):
    t = idx.shape[1]
    assert t <= 256 * EXPERT_BLOCK
    dest, grp = pl.pallas_call(
        functools.partial(_plan_kernel, n_exp=n_exp),
        grid=(1,),
        in_specs=[pl.BlockSpec(idx.shape, lambda i: (0, 0))],
        out_specs=[pl.BlockSpec((t // LANES, TOP_K, LANES), lambda i: (0, 0, 0)),
                   pl.BlockSpec((n_exp, LANES), lambda i: (0, 0))],
        out_shape=[jax.ShapeDtypeStruct((t // LANES, TOP_K, LANES), jnp.int32),
                   jax.ShapeDtypeStruct((n_exp, LANES), jnp.int32)],
        scratch_shapes=[pltpu.VMEM((n_exp, t), F32)],
        compiler_params=_params("arbitrary"),
        name="dispatch_plan",
    )(idx)
    return dest, grp[:, 0], grp[:, 1]


def _dispatch_kernel(gs_ref, gn_ref, dest_ref, xp_ref, xs_hbm, zero_ref, sem, zsem):
    n_exp = gs_ref.shape[0]

    def zero_block(m):
        return pltpu.make_async_copy(zero_ref, xs_hbm.at[pl.ds(m * EXPERT_BLOCK, EXPERT_BLOCK), :], zsem)

    def zero_copy(e):
        return zero_block(gs_ref[e] + gn_ref[e] - 1)

    @pl.when(pl.program_id(0) == 0)
    def _():
        zero_ref[...] = jnp.zeros_like(zero_ref)
        used = gs_ref[n_exp - 1] + gn_ref[n_exp - 1]
        total = xs_hbm.shape[0] // EXPERT_BLOCK

        def start(e, _):
            @pl.when(gn_ref[e] > 0)
            def _():
                zero_copy(e).start()
            return 0

        def wait(e, _):
            @pl.when(gn_ref[e] > 0)
            def _():
                zero_copy(e).wait()
            return 0

        lax.fori_loop(0, n_exp, start, 0)
        lax.fori_loop(used, total, lambda m, _: (zero_block(m).start(), 0)[1], 0)
        lax.fori_loop(0, n_exp, wait, 0)
        lax.fori_loop(used, total, lambda m, _: (zero_block(m).wait(), 0)[1], 0)

    def row_copy(r, k):
        return pltpu.make_async_copy(xp_ref.at[pl.ds(r, 1), :], xs_hbm.at[pl.ds(dest_ref[k, r], 1), :], sem)

    def issue(r, _):
        for k in range(TOP_K):
            row_copy(r, k).start(priority=k % 2)
        return 0

    def drain(r, _):
        for k in range(TOP_K):
            row_copy(r, k).wait()
        return 0

    lax.fori_loop(0, LANES, issue, 0, unroll=8)
    lax.fori_loop(0, LANES, drain, 0)


def _dispatch(grp_start, grp_blocks, dest, xp, n_rows):
    t, w = xp.shape
    return pl.pallas_call(
        _dispatch_kernel,
        grid_spec=pltpu.PrefetchScalarGridSpec(
            num_scalar_prefetch=2,
            grid=(t // LANES,),
            in_specs=[pl.BlockSpec((None, TOP_K, LANES), lambda i, gs, gn: (i, 0, 0), memory_space=pltpu.SMEM),
                      pl.BlockSpec((LANES, w), lambda i, gs, gn: (i, 0))],
            out_specs=pl.BlockSpec(memory_space=pl.ANY),
            scratch_shapes=[pltpu.VMEM((EXPERT_BLOCK, w), xp.dtype), pltpu.SemaphoreType.DMA,
                            pltpu.SemaphoreType.DMA]),
        out_shape=jax.ShapeDtypeStruct((n_rows, w), xp.dtype),
        compiler_params=_params("arbitrary"),
        name="dispatch_rows",
    )(grp_start, grp_blocks, dest, xp)


def _weight_stage(w_hbms, stages, wsem, tn):
    n, e = pl.program_id(0), pl.program_id(1)
    n_tiles, n_exp = pl.num_programs(0), pl.num_programs(1)

    def copies(nn, ee):
        cols = pl.ds(pl.multiple_of(nn * tn, tn), tn)
        return [pltpu.make_async_copy(w.at[ee, :, cols], st, wsem.at[k])
                for k, (w, st) in enumerate(zip(w_hbms, stages))]

    @pl.when((n == 0) & (e == 0))
    def _():
        for c in copies(0, 0):
            c.start(priority=1)

    def wait_current():
        for c in copies(n, e):
            c.wait()

    def start_next():
        wrap = e == n_exp - 1

        @pl.when(jnp.logical_not(wrap & (n == n_tiles - 1)))
        def _():
            for c in copies(jnp.where(wrap, n + 1, n), jnp.where(wrap, 0, e + 1)):
                c.start(priority=1)

    return wait_current, start_next


def _group_loop(gs_ref, gn_ref, src_hbm, dst_hbm, col0, in_buf, out_buf, sem_in, sem_out, compute, after_first_fetch):
    e = pl.program_id(1)
    nb = gn_ref[e]
    b0 = gs_ref[e]
    width = out_buf.shape[2]
    cb = in_buf.shape[1] // EXPERT_BLOCK
    sizes = [cb >> s for s in range(cb.bit_length())]
    assert sizes[-1] == 1 and sum(sizes) == 2 * cb - 1, "blocks per chunk must be a power of two"
    nch = lax.shift_right_logical(nb, cb.bit_length() - 1) + sum((nb >> s) & 1 for s in range(cb.bit_length() - 1))

    def rows(start, nblk):
        return pl.ds(pl.multiple_of((b0 + start) * EXPERT_BLOCK, EXPERT_BLOCK), nblk * EXPERT_BLOCK)

    def fetch(start, slot):
        return pltpu.make_async_copy(src_hbm.at[rows(start, cb), :], in_buf.at[slot], sem_in.at[slot])

    def store(start, slot, nblk):
        return pltpu.make_async_copy(out_buf.at[slot, pl.ds(0, nblk * EXPERT_BLOCK), :],
                                     dst_hbm.at[rows(start, nblk), pl.ds(col0, width)], sem_out.at[slot])

    def piece(remaining):
        z = jnp.int32(1)
        for size in reversed(sizes[:-1]):
            z = jnp.where(remaining >= size, size, z)
        return z

    def for_size(count, fn):
        for size in sizes:
            @pl.when(count == size)
            def _():
                fn(size)

    @pl.when(e == 0)
    def _():
        last = gs_ref.shape[0] - 1
        used = gs_ref[last] + gn_ref[last]
        total = dst_hbm.shape[0] // EXPERT_BLOCK
        out_buf[0, 0:EXPERT_BLOCK, :] = jnp.zeros((EXPERT_BLOCK, width), out_buf.dtype)

        def tail(m):
            blk = pl.ds(pl.multiple_of(m * EXPERT_BLOCK, EXPERT_BLOCK), EXPERT_BLOCK)
            return pltpu.make_async_copy(out_buf.at[0, pl.ds(0, EXPERT_BLOCK), :],
                                         dst_hbm.at[blk, pl.ds(col0, width)], sem_out.at[0])

        lax.fori_loop(used, total, lambda m, _: (tail(m).start(), 0)[1], 0)
        lax.fori_loop(used, total, lambda m, _: (tail(m).wait(), 0)[1], 0)

    @pl.when(nch > 0)
    def _():
        fetch(0, 0).start()

    after_first_fetch()

    def wait_store(slot, size):
        for_size(size, lambda s: store(0, slot, s).wait())

    def step(i, carry):
        start, prev1, prev2 = carry
        size = piece(nb - start)
        slot = i & 1
        fetch(start, slot).wait()

        @pl.when(start + size < nb)
        def _():
            fetch(start + size, 1 - slot).start()

        @pl.when(i >= 2)
        def _():
            wait_store(slot, prev2)

        def run(s):
            n = s * EXPERT_BLOCK
            out_buf[slot, 0:n, :] = compute(in_buf[slot, 0:n, :])
            store(start, slot, s).start()

        for_size(size, run)
        return start + size, size, prev1

    _, last1, last2 = lax.fori_loop(0, nch, step, (jnp.int32(0), jnp.int32(0), jnp.int32(0)))

    @pl.when(nch >= 2)
    def _():
        wait_store(nch & 1, last2)

    @pl.when(nch >= 1)
    def _():
        wait_store((nch - 1) & 1, last1)


def _gateup_kernel(gs_ref, gn_ref, xs_hbm, wg_hbm, wu_hbm, bg_ref, bu_ref, h_hbm,
                   wg_st, wu_st, wgb_ref, wub_ref, xbuf, hbuf, wsem, sem_in, sem_out):
    tn = wgb_ref.shape[1]
    half = xbuf.shape[2]
    wait_weights, start_next_weights = _weight_stage((wg_hbm, wu_hbm), (wg_st, wu_st), wsem, tn)

    def stage_weights():
        wait_weights()

        @pl.when(gn_ref[pl.program_id(1)] > 0)
        def _():
            wgb_ref[...] = wg_st[...].astype(BF16)
            wub_ref[...] = wu_st[...].astype(BF16)

        start_next_weights()

    def compute(xp):
        xa = pltpu.bitcast(xp << 16, F32).astype(BF16)
        xb = pltpu.bitcast(xp & jnp.uint32(0xFFFF0000), F32).astype(BF16)
        gte = _dot(xa, wgb_ref[0:half, :]) + _dot(xb, wgb_ref[half:, :]) + bg_ref[...]
        up = _dot(xa, wub_ref[0:half, :]) + _dot(xb, wub_ref[half:, :]) + bu_ref[...]
        gte = jnp.minimum(gte, SWIGLU_LIMIT)
        up = jnp.clip(up, -SWIGLU_LIMIT, SWIGLU_LIMIT)
        return (gte * jax.nn.sigmoid(SWIGLU_ALPHA * gte) * (up + 1.0)).astype(BF16)

    col0 = pl.multiple_of(pl.program_id(0) * tn, tn)
    _group_loop(gs_ref, gn_ref, xs_hbm, h_hbm, col0, xbuf, hbuf, sem_in, sem_out, compute, stage_weights)


def _gateup(grp_start, grp_blocks, xs, w_gate, w_up, b_gate, b_up, tn):
    n_rows, half = xs.shape
    n_exp, d, f = w_gate.shape
    chunk = CHUNK_BLOCKS * EXPERT_BLOCK
    any_spec = pl.BlockSpec(memory_space=pl.ANY)
    b_spec = pl.BlockSpec((None, 1, tn), lambda n, e, gs, gn: (e, 0, n))
    return pl.pallas_call(
        _gateup_kernel,
        grid_spec=pltpu.PrefetchScalarGridSpec(
            num_scalar_prefetch=2,
            grid=(f // tn, n_exp),
            in_specs=[any_spec, any_spec, any_spec, b_spec, b_spec],
            out_specs=any_spec,
            scratch_shapes=[pltpu.VMEM((d, tn), F32), pltpu.VMEM((d, tn), F32),
                            pltpu.VMEM((d, tn), BF16), pltpu.VMEM((d, tn), BF16),
                            pltpu.VMEM((2, chunk, half), xs.dtype), pltpu.VMEM((2, chunk, tn), BF16),
                            pltpu.SemaphoreType.DMA((2,)), pltpu.SemaphoreType.DMA((2,)),
                            pltpu.SemaphoreType.DMA((2,))]),
        out_shape=jax.ShapeDtypeStruct((n_rows, f), BF16),
        compiler_params=_params("arbitrary", "arbitrary"),
        name="expert_gate_up",
    )(grp_start, grp_blocks, xs, w_gate, w_up, b_gate, b_up)


def _down_kernel(gs_ref, gn_ref, h_hbm, wd_hbm, bd_ref, y_hbm, wd_st, wdb_ref, hbuf, ybuf, wsem, sem_in, sem_out):
    tn = wdb_ref.shape[1]
    wait_weights, start_next_weights = _weight_stage((wd_hbm,), (wd_st,), wsem, tn)

    def stage_weights():
        wait_weights()

        @pl.when(gn_ref[pl.program_id(1)] > 0)
        def _():
            wdb_ref[...] = wd_st[...].astype(BF16)

        start_next_weights()

    def compute(h):
        y = (_dot(h, wdb_ref[...]) + bd_ref[...]).astype(BF16).astype(F32)
        bits = pltpu.bitcast(y, jnp.uint32)
        return (bits[:, :tn // 2] >> 16) | bits[:, tn // 2:]

    col0 = pl.multiple_of(pl.program_id(0) * (tn // 2), tn // 2)
    _group_loop(gs_ref, gn_ref, h_hbm, y_hbm, col0, hbuf, ybuf, sem_in, sem_out, compute, stage_weights)


def _down(grp_start, grp_blocks, h, w_down, b_down, tn):
    n_rows, f = h.shape
    n_exp, _, d = w_down.shape
    return pl.pallas_call(
        _down_kernel,
        grid_spec=pltpu.PrefetchScalarGridSpec(
            num_scalar_prefetch=2,
            grid=(d // tn, n_exp),
            in_specs=[pl.BlockSpec(memory_space=pl.ANY),
                      pl.BlockSpec(memory_space=pl.ANY),
                      pl.BlockSpec((None, 1, tn), lambda n, e, gs, gn: (e, 0, n))],
            out_specs=pl.BlockSpec(memory_space=pl.ANY),
            scratch_shapes=[pltpu.VMEM((f, tn), F32), pltpu.VMEM((f, tn), BF16),
                            pltpu.VMEM((2, DOWN_CHUNK_BLOCKS * EXPERT_BLOCK, f), h.dtype),
                            pltpu.VMEM((2, DOWN_CHUNK_BLOCKS * EXPERT_BLOCK, tn // 2), jnp.uint32),
                            pltpu.SemaphoreType.DMA((1,)), pltpu.SemaphoreType.DMA((2,)),
                            pltpu.SemaphoreType.DMA((2,))]),
        out_shape=jax.ShapeDtypeStruct((n_rows, d // 2), jnp.uint32),
        compiler_params=_params("arbitrary", "arbitrary"),
        name="expert_down",
    )(grp_start, grp_blocks, h, w_down, b_down)


def _combine_kernel(dest_ref, dest_next_ref, wt_ref, h_ref, y_hbm, o_ref, buf_a, buf_b, wb_ref, sem, *, tile_cols):
    i = pl.program_id(0)
    n_cols = y_hbm.shape[1]
    half = tile_cols // 2
    n_chunks = n_cols // LANES
    rows_per_chunk = LANES // n_chunks

    def row_copy(idx_ref, t, buf, s, r, k):
        return pltpu.make_async_copy(y_hbm.at[pl.ds(idx_ref[t, k, r], 1), :], buf.at[k, pl.ds(r, 1), :], sem.at[s])

    def drain(idx_ref, t, buf, s):
        def wait(r, _):
            for k in range(TOP_K):
                row_copy(idx_ref, t, buf, s, r, k).wait()
            return 0

        lax.fori_loop(0, LANES, wait, 0)

    def combine_tile(t, buf, issue_next):
        rows = slice(t * LANES, (t + 1) * LANES)
        wt = wt_ref[:, rows]
        for k in range(TOP_K):
            wb_ref[k] = jnp.broadcast_to(wt[k:k + 1, :], (LANES, LANES)).T
        for c in range(n_chunks):
            col = (c * LANES // half) * tile_cols + (c * LANES) % half
            lo = h_ref[rows, col:col + LANES]
            hi = h_ref[rows, col + half:col + half + LANES]
            for k in range(TOP_K):
                u = buf[k, :, c * LANES:(c + 1) * LANES]
                lo = lo + wb_ref[k] * pltpu.bitcast(u << 16, F32)
                hi = hi + wb_ref[k] * pltpu.bitcast(u & jnp.uint32(0xFFFF0000), F32)
            o_ref[rows, col:col + LANES] = lo
            o_ref[rows, col + half:col + half + LANES] = hi
            for r in range(c * rows_per_chunk, (c + 1) * rows_per_chunk):
                for k in range(TOP_K):
                    issue_next(r, k)

    @pl.when(i == 0)
    def _():
        def issue(r, _):
            for k in range(TOP_K):
                row_copy(dest_ref, 0, buf_a, 0, r, k).start(priority=k % 2)
            return 0

        lax.fori_loop(0, LANES, issue, 0)

    drain(dest_ref, 0, buf_a, 0)
    combine_tile(0, buf_a, lambda r, k: row_copy(dest_ref, 1, buf_b, 1, r, k).start(priority=k % 2))
    drain(dest_ref, 1, buf_b, 1)
    combine_tile(1, buf_b, lambda r, k: row_copy(dest_next_ref, 0, buf_a, 0, r, k).start(priority=k % 2))

    @pl.when(i == pl.num_programs(0) - 1)
    def _():
        drain(dest_next_ref, 0, buf_a, 0)


def _combine(dest, wts, h1, y, tile_cols):
    t, d = h1.shape
    nt = t // LANES
    assert nt % 2 == 0 and LANES % (y.shape[1] // LANES) == 0
    smem = pltpu.SMEM
    return pl.pallas_call(
        functools.partial(_combine_kernel, tile_cols=tile_cols),
        grid=(nt // 2,),
        in_specs=[pl.BlockSpec((2, TOP_K, LANES), lambda i: (i, 0, 0), memory_space=smem),
                  pl.BlockSpec((1, TOP_K, LANES), lambda i: (jnp.minimum(2 * i + 2, nt - 1), 0, 0), memory_space=smem),
                  pl.BlockSpec((TOP_K, 2 * LANES), lambda i: (0, i)),
                  pl.BlockSpec((2 * LANES, d), lambda i: (i, 0)),
                  pl.BlockSpec(memory_space=pl.ANY)],
        out_specs=pl.BlockSpec((2 * LANES, d), lambda i: (i, 0)),
        out_shape=jax.ShapeDtypeStruct((t, d), F32),
        scratch_shapes=[pltpu.VMEM((TOP_K, LANES, y.shape[1]), y.dtype),
                        pltpu.VMEM((TOP_K, LANES, y.shape[1]), y.dtype),
                        pltpu.VMEM((TOP_K, LANES, LANES), F32),
                        pltpu.SemaphoreType.DMA((2,))],
        compiler_params=_params("arbitrary"),
        name="combine_rows",
    )(dest, dest, wts, h1, y)


def _layer(x, meta, norm_mix, w_in, b_forget, b_branch_gate, q_norm, k_norm, w_pool_group, pool_scale,
           w_branch_pool, w_branch_attn, w_out, norm_ffn, w_router, b_router, w_gate, b_gate, w_up, b_up,
           w_down, b_down):
    batch, seq, d = x.shape
    t = batch * seq
    n_meta = meta.shape[0]
    pool_w = w_pool_group.shape[0] * w_pool_group.shape[1]
    attn_w = w_branch_attn.shape[0]
    n_heads = attn_w // HEAD_DIM
    n_exp = w_router.shape[1]
    col_q = pool_w
    col_zf = pool_w + 3 * attn_w
    col_zg = col_zf + n_heads
    assert n_heads <= LANES and col_zf % LANES == 0

    tm = min(1024, t)
    x2 = x.reshape(t, d)

    hn = _rmsnorm(x2, norm_mix.reshape(1, d), min(256, t))
    hn_meta = _rmsnorm(meta, norm_mix.reshape(1, d), n_meta)

    w_in_t = w_in.T
    uqkv, uqkv_meta = _proj(hn, hn_meta, w_in_t, None, col0=0, ncols=col_zf, tn=512, tm=tm,
                            out_dtype=BF16, name="in_proj_uqkv")
    zf, zf_meta = _proj(hn, hn_meta, w_in_t, None, col0=col_zf, ncols=LANES, tn=LANES, tm=tm,
                        out_dtype=F32, name="in_proj_forget")
    gates = _proj(hn, None, w_in_t, b_branch_gate.reshape(1, 2 * d), col0=col_zf, ncols=2 * d, tn=512, tm=tm,
                  out_dtype=BF16, name="in_proj_gates", shift=col_zg - col_zf)

    bf_pad = jnp.pad(b_forget.reshape(1, n_heads), ((0, 0), (0, LANES - n_heads)))
    zfm_pad = jnp.pad(zf_meta, ((0, LANES - n_meta), (0, 0)))
    cr, mbias = _cumsum(zf, zfm_pad, bf_pad, batch, n_heads, n_meta)

    y_pool = _pool(uqkv, uqkv_meta, w_pool_group, pool_scale.reshape(1, pool_w), batch, min(512, seq))
    y_attn = _attention(uqkv, uqkv_meta, cr, mbias, q_norm.reshape(1, HEAD_DIM), k_norm.reshape(1, HEAD_DIM),
                        batch, n_heads, col_q, min(512, seq))

    merged = _merge(y_pool, y_attn, w_branch_pool, w_branch_attn, gates, tm, 512)
    h1 = _outproj(merged, w_out, x2, tm, 512)

    xp, idx, wts = _router(h1, norm_ffn.reshape(1, d), w_router.T, b_router.reshape(n_exp, 1), min(256, t))
    n_blocks = -(-(t * TOP_K + n_exp * (EXPERT_BLOCK - 1)) // EXPERT_BLOCK) + max(CHUNK_BLOCKS, DOWN_CHUNK_BLOCKS) - 1
    dest, grp_start, grp_blocks = _plan(idx, n_exp)
    xs = _dispatch(grp_start, grp_blocks, dest, xp, n_blocks * EXPERT_BLOCK)
    f = w_gate.shape[2]
    hmid = _gateup(grp_start, grp_blocks, xs, w_gate, w_up, b_gate.reshape(n_exp, 1, f), b_up.reshape(n_exp, 1, f), 512)
    down_tn = 2048
    y = _down(grp_start, grp_blocks, hmid, w_down, b_down.reshape(n_exp, 1, d), down_tn)
    out = _combine(dest, wts, h1, y, down_tn)
    return out.reshape(batch, seq, d)


def kernel(x, meta_tokens, norm_mix, w_in, b_forget, b_branch_gate, q_norm, k_norm, w_pool_group, pool_scale,
           w_branch_pool, w_branch_attn, w_out, norm_ffn, w_router, b_router, w_gate, b_gate, w_up, b_up,
           w_down, b_down):
    depth = norm_mix.shape[0]
    assert depth == 1, "the fused layer pipeline drops the meta rows after the (single) layer"
    return _layer(x, meta_tokens, norm_mix[0], w_in[0], b_forget[0], b_branch_gate[0], q_norm[0], k_norm[0],
                  w_pool_group[0], pool_scale[0], w_branch_pool[0], w_branch_attn[0], w_out[0], norm_ffn[0],
                  w_router[0], b_router[0], w_gate[0], b_gate[0], w_up[0], b_up[0], w_down[0], b_down[0])
```
